```python
import math
import jax
import jax.numpy as jnp
from jax import lax
import numpy as np

D_MODEL = 1024
BATCH = 8
SEQ = 2048
DEPTH = 4
DEC_BATCH = 32
DEC_SEQ = 4
PAST_LEN = 16384
PAGE_SIZE = 128

N_AB = (DEPTH + 1) // 2
N_CD = DEPTH // 2

S5_WIDTH = D_MODEL // 2
S5_GROUP = 16
S5_GROUPS = S5_WIDTH // S5_GROUP
S5_STATE = 64
S5_DT_MIN = 1e-3
S5_DT_MAX = 1e-1

MLA_HEADS = 8
MLA_NOPE = 64
MLA_ROPE = 32
MLA_V = 64
MLA_Q_RANK = 384
MLA_KV_RANK = 256
ROPE_BASE = 10000.0

SB_HEADS = 8
SB_HEAD_DIM = 64
SB_WIDTH = SB_HEADS * SB_HEAD_DIM

RW_HEADS = 8
RW_HEAD_DIM = 64
RW_WIDTH = RW_HEADS * RW_HEAD_DIM
RW_DECAY_RANK = 64
RW_A_RANK = 64
RW_GATE_RANK = 128
RW_COLS = 3 * RW_WIDTH + RW_DECAY_RANK + RW_A_RANK + RW_GATE_RANK
RW_GN_EPS = 64e-5

AB_IN = S5_WIDTH + MLA_Q_RANK + MLA_KV_RANK + MLA_ROPE
AB_MIX = S5_WIDTH + MLA_HEADS * MLA_V
CD_IN = 3 * SB_WIDTH + RW_COLS
CD_MIX = SB_WIDTH + RW_WIDTH

PEER_HEADS = 8
PEER_N_KEYS = 128
PEER_EXPERTS = PEER_N_KEYS * PEER_N_KEYS
PEER_QUERY = 256
PEER_TOPK = 16
PEER_CHUNK = 256

Q_BLOCK = 128
NORM_EPS = 1e-6
POOL_EXTRA_DEN = 4

kernel_name = 'hybrid_s5_mla_stickbreak_rwkv7_peer_step'


def rmsnorm(x, g):
    xf = x.astype(jnp.float32)
    y = xf * lax.rsqrt(jnp.mean(xf * xf, axis=-1, keepdims=True) + NORM_EPS)
    return (y * g.astype(jnp.float32)).astype(x.dtype)


def rope(x, pos):
    half = x.shape[-1] // 2
    inv = ROPE_BASE ** (-jnp.arange(half, dtype=jnp.float32) / half)
    ang = pos.astype(jnp.float32)[:, None] * inv[None, :]
    ang = ang.reshape((1, pos.shape[0]) + (1,) * (x.ndim - 3) + (half,))
    cos, sin = jnp.cos(ang), jnp.sin(ang)
    xf = x.astype(jnp.float32)
    x1, x2 = xf[..., :half], xf[..., half:]
    return jnp.concatenate([x1 * cos - x2 * sin, x1 * sin + x2 * cos], axis=-1).astype(x.dtype)


def map_query_blocks(fn, q_pos, *qs):
    T = q_pos.shape[0]
    blk = Q_BLOCK if T % Q_BLOCK == 0 else T
    nb = T // blk
    def split(a):
        return jnp.moveaxis(a.reshape((a.shape[0], nb, blk) + a.shape[2:]), 1, 0)
    out = lax.map(lambda xs: fn(*xs), (q_pos.reshape(nb, blk),) + tuple(split(a) for a in qs))
    out = jnp.moveaxis(out, 0, 1)
    return out.reshape((out.shape[0], T) + out.shape[3:])


def gather_pages(pool, layer, page_table):
    rows = pool[layer, page_table]
    return rows.reshape((rows.shape[0], rows.shape[1] * rows.shape[2]) + rows.shape[3:])


def s5_combine(e1, e2):
    a1r, a1i, b1r, b1i = e1
    a2r, a2i, b2r, b2i = e2
    return (a2r * a1r - a2i * a1i, a2r * a1i + a2i * a1r,
            a2r * b1r - a2i * b1i + b2r, a2r * b1i + a2i * b1r + b2i)


def s5_mixer(u, s0_re, s0_im, lam_re, lam_im, log_dt, b_re, b_im, c_re, c_im, d, w_glu, b_glu):
    f32 = jnp.float32
    Bsz, T, _ = u.shape
    ug = u.astype(f32).reshape(Bsz, T, S5_GROUPS, S5_GROUP)
    lr, li = lam_re.astype(f32), lam_im.astype(f32)
    dt = jnp.exp(log_dt.astype(f32))[:, None]
    mag = jnp.exp(lr * dt)
    ab_re, ab_im = mag * jnp.cos(li * dt), mag * jnp.sin(li * dt)
    den = lr * lr + li * li
    nr = ab_re - 1.0
    co_re = (nr * lr + ab_im * li) / den
    co_im = (ab_im * lr - nr * li) / den
    bu_re = jnp.einsum('btgh,gph->tbgp', ug, b_re.astype(f32))
    bu_im = jnp.einsum('btgh,gph->tbgp', ug, b_im.astype(f32))
    bb_re = co_re * bu_re - co_im * bu_im
    bb_im = co_re * bu_im + co_im * bu_re
    s0r, s0i = s0_re.astype(f32), s0_im.astype(f32)
    bb_re = bb_re.at[0].add(ab_re * s0r - ab_im * s0i)
    bb_im = bb_im.at[0].add(ab_re * s0i + ab_im * s0r)
    a_re = jnp.broadcast_to(ab_re, (T, 1) + ab_re.shape)
    a_im = jnp.broadcast_to(ab_im, (T, 1) + ab_im.shape)
    _, _, x_re, x_im = lax.associative_scan(s5_combine, (a_re, a_im, bb_re, bb_im), axis=0)
    y = (jnp.einsum('tbgp,ghp->btgh', x_re, c_re.astype(f32))
         - jnp.einsum('tbgp,ghp->btgh', x_im, c_im.astype(f32))
         + d.astype(f32) * ug)
    z = jax.nn.gelu(y.reshape(Bsz, T, S5_WIDTH))
    out = z * jax.nn.sigmoid(z @ w_glu.astype(f32) + b_glu.astype(f32))
    return out.astype(u.dtype), x_re[-1], x_im[-1]


def mla_attention(q_lat, q_rope, ckv, krope, q_pos, k_pos):
    scale = (MLA_NOPE + MLA_ROPE) ** -0.5
    def block(pos, ql, qr):
        s = jnp.einsum('bthc,bsc->bhts', ql, ckv) + jnp.einsum('bthr,bsr->bhts', qr, krope)
        s = s.astype(jnp.float32) * scale
        s = jnp.where(k_pos[None, None, None, :] <= pos[None, None, :, None], s, -jnp.inf)
        p = jax.nn.softmax(s, axis=-1).astype(ckv.dtype)
        return jnp.einsum('bhts,bsc->bthc', p, ckv)
    return map_query_blocks(block, q_pos, q_lat, q_rope)


def stick_breaking_attention(q, k, v, q_pos, k_pos):
    scale = SB_HEAD_DIM ** -0.5
    def block(pos, qb):
        z = jnp.einsum('bthd,bshd->bhts', qb, k).astype(jnp.float32) * scale
        mask = (k_pos[None, :] < pos[:, None])[None, None]
        log_not = jnp.where(mask, -jax.nn.softplus(z), 0.0)
        after = lax.cumsum(log_not, axis=3, reverse=True) - log_not
        a = jnp.where(mask, jnp.exp(jax.nn.log_sigmoid(z) + after), 0.0)
        return jnp.einsum('bhts,bshd->bthd', a.astype(v.dtype), v)
    return map_query_blocks(block, q_pos, q)


def rwkv7_mixer(p, wkv0, shift0, w, i):
    f32 = jnp.float32
    Bsz, T, _ = p.shape
    W = RW_WIDTH
    prev = jnp.concatenate([shift0[:, None, :].astype(p.dtype), p[:, :-1]], axis=1)
    ps = (p + (prev - p) * w['rwkv_mu'][i]).astype(f32)
    r, k, v, xw, xa, xg = jnp.split(
        ps, [W, 2 * W, 3 * W, 3 * W + RW_DECAY_RANK, 3 * W + RW_DECAY_RANK + RW_A_RANK], axis=-1)
    w_log = -jax.nn.softplus(-(w['rwkv_w0'][i] + jnp.tanh(xw) @ w['rwkv_w2'][i])) - 0.5
    decay = jnp.exp(-jnp.exp(w_log.astype(f32)))
    a = jax.nn.sigmoid(w['rwkv_a0'][i] + xa @ w['rwkv_a2'][i]).astype(f32)
    g = (jax.nn.sigmoid(xg) @ w['rwkv_g2'][i]).astype(f32)
    hd = lambda t: t.reshape(Bsz, T, RW_HEADS, RW_HEAD_DIM)
    r, k, v, a, decay = hd(r), hd(k), hd(v), hd(a), hd(decay)
    kk = k * w['rwkv_k_k'][i].astype(f32).reshape(RW_HEADS, RW_HEAD_DIM)
    kk = kk / jnp.maximum(jnp.sqrt(jnp.sum(kk * kk, axis=-1, keepdims=True)), 1e-12)
    k = k * (1.0 + (a - 1.0) * w['rwkv_k_a'][i].astype(f32).reshape(RW_HEADS, RW_HEAD_DIM))
    def step(S, inp):
        r_t, w_t, k_t, v_t, kk_t, b_t = inp
        sa = jnp.einsum('bhvk,bhk->bhv', S, -kk_t)
        S = S * w_t[:, :, None, :] + sa[..., None] * b_t[:, :, None, :] + v_t[..., None] * k_t[:, :, None, :]
        return S, jnp.einsum('bhvk,bhk->bhv', S, r_t)
    xs = tuple(jnp.moveaxis(t, 1, 0) for t in (r, decay, k, v, kk, kk * a))
    S, y = lax.scan(step, wkv0.astype(f32), xs)
    y = jnp.moveaxis(y, 0, 1)
    mean = jnp.mean(y, axis=-1, keepdims=True)
    var = jnp.mean(jnp.square(y - mean), axis=-1, keepdims=True)
    y = ((y - mean) * lax.rsqrt(var + RW_GN_EPS)).reshape(Bsz, T, W)
    y = y * w['rwkv_ln_w'][i].astype(f32) + w['rwkv_ln_b'][i].astype(f32)
    bonus = jnp.sum(r * k * w['rwkv_r_k'][i].astype(f32), axis=-1, keepdims=True) * v
    y = (y + bonus.reshape(Bsz, T, W)) * g
    return y.astype(p.dtype), S, p[:, -1]


def peer_ffn(h, w_q, k1, k2, u_tab, v_tab):
    Bsz, T, D = h.shape
    xt = h.reshape(Bsz * T, D)
    n = xt.shape[0]
    chunk = min(PEER_CHUNK, n)
    n_pad = -(-n // chunk) * chunk
    xt = jnp.pad(xt, ((0, n_pad - n), (0, 0)))
    half = PEER_QUERY // 2
    def block(xb):
        m = xb.shape[0]
        q = (xb @ w_q).reshape(m, PEER_HEADS, PEER_QUERY)
        s1 = jnp.einsum('nhd,hkd->nhk', q[..., :half], k1).astype(jnp.float32)
        s2 = jnp.einsum('nhd,hkd->nhk', q[..., half:], k2).astype(jnp.float32)
        v1, i1 = lax.top_k(s1, PEER_TOPK)
        v2, i2 = lax.top_k(s2, PEER_TOPK)
        cand = (v1[..., :, None] + v2[..., None, :]).reshape(m, PEER_HEADS, PEER_TOPK * PEER_TOPK)
        cidx = (i1[..., :, None] * PEER_N_KEYS + i2[..., None, :]).reshape(m, PEER_HEADS, PEER_TOPK * PEER_TOPK)
        best, sel = lax.top_k(cand, PEER_TOPK)
        e = jnp.take_along_axis(cidx, sel, axis=-1)
        gate = jax.nn.softmax(best, axis=-1)
        act = jax.nn.gelu(jnp.einsum('nd,nhkd->nhk', xb, u_tab[e]).astype(jnp.float32))
        return jnp.einsum('nhk,nhkd->nd', (gate * act).astype(xb.dtype), v_tab[e])
    out = lax.map(block, xt.reshape(n_pad // chunk, chunk, D))
    return out.reshape(n_pad, D)[:n].reshape(Bsz, T, D)


def ab_layer(h, pos, past_ckv, past_kr, s0_re, s0_im, w, i):
    Bsz, T, _ = h.shape
    proj = h @ w['ab_w_in'][i]
    u, cq, ckv, kr = jnp.split(
        proj, [S5_WIDTH, S5_WIDTH + MLA_Q_RANK, S5_WIDTH + MLA_Q_RANK + MLA_KV_RANK], axis=-1)
    s5_out, s_re, s_im = s5_mixer(
        u, s0_re, s0_im, w['s5_lambda_re'][i], w['s5_lambda_im'][i], w['s5_log_dt'][i],
        w['s5_b_re'][i], w['s5_b_im'][i], w['s5_c_re'][i], w['s5_c_im'][i], w['s5_d'][i],
        w['s5_w_glu'][i], w['s5_b_glu'][i])
    cq = rmsnorm(cq, w['mla_g_q'][i])
    q = jnp.einsum('btc,chd->bthd', cq, w['mla_w_uq'][i])
    q_nope, q_rope = q[..., :MLA_NOPE], rope(q[..., MLA_NOPE:], pos)
    ckv = rmsnorm(ckv, w['mla_g_kv'][i])
    kr = rope(kr, pos)
    q_lat = jnp.einsum('bthn,chn->bthc', q_nope, w['mla_w_uk'][i])
    keys_ckv = jnp.concatenate([past_ckv.astype(ckv.dtype), ckv], axis=1)
    keys_kr = jnp.concatenate([past_kr.astype(kr.dtype), kr], axis=1)
    k_pos = jnp.arange(keys_ckv.shape[1], dtype=jnp.int32)
    o_lat = mla_attention(q_lat, q_rope, keys_ckv, keys_kr, pos, k_pos)
    o = jnp.einsum('bthc,chv->bthv', o_lat, w['mla_w_uv'][i]).reshape(Bsz, T, MLA_HEADS * MLA_V)
    out = jnp.concatenate([s5_out.astype(o.dtype), o], axis=-1) @ w['ab_w_out'][i]
    return out, ckv, kr, s_re, s_im


def cd_layer(h, pos, past_k, past_v, wkv0, shift0, w, i):
    Bsz, T, _ = h.shape
    proj = h @ w['cd_w_in'][i]
    q, k, v, rw = jnp.split(proj, [SB_WIDTH, 2 * SB_WIDTH, 3 * SB_WIDTH], axis=-1)
    q = q.reshape(Bsz, T, SB_HEADS, SB_HEAD_DIM)
    k = k.reshape(Bsz, T, SB_HEADS, SB_HEAD_DIM)
    v = v.reshape(Bsz, T, SB_HEADS, SB_HEAD_DIM)
    keys = jnp.concatenate([past_k.astype(k.dtype), k], axis=1)
    vals = jnp.concatenate([past_v.astype(v.dtype), v], axis=1)
    k_pos = jnp.arange(keys.shape[1], dtype=jnp.int32)
    sb_out = stick_breaking_attention(q, keys, vals, pos, k_pos).reshape(Bsz, T, SB_WIDTH)
    rw_out, wkv, shift = rwkv7_mixer(rw, wkv0, shift0, w, i)
    out = jnp.concatenate([sb_out, rw_out.astype(sb_out.dtype)], axis=-1) @ w['cd_w_out'][i]
    return out, k, v, wkv, shift


def trunk(x, c, w, cache):
    Bsz, T, _ = x.shape
    if cache is None:
        past_len = 0
    else:
        (c_ckv, c_kr, c_k, c_v, page_table, s_re, s_im, s_wkv, s_shift) = cache
        past_len = page_table.shape[1] * PAGE_SIZE
    pos = past_len + jnp.arange(T, dtype=jnp.int32)
    ckv_rows, kr_rows, s5r, s5i, k_rows, v_rows, wkvs, shifts = [], [], [], [], [], [], [], []
    for l in range(DEPTH):
        i = l // 2
        mod = jax.nn.silu(c) @ w['ada_w'][l] + w['ada_b'][l]
        sh1, sc1, g1, sh2, sc2, g2 = jnp.split(mod[:, None, :], 6, axis=-1)
        h = rmsnorm(x, w['norm1_g'][l]) * (1 + sc1) + sh1
        if l % 2 == 0:
            if cache is None:
                past_ckv = jnp.zeros((Bsz, 0, MLA_KV_RANK), x.dtype)
                past_kr = jnp.zeros((Bsz, 0, MLA_ROPE), x.dtype)
                s0_re = jnp.zeros((Bsz, S5_GROUPS, S5_STATE), jnp.float32)
                s0_im = jnp.zeros((Bsz, S5_GROUPS, S5_STATE), jnp.float32)
            else:
                past_ckv = gather_pages(c_ckv, i, page_table)
                past_kr = gather_pages(c_kr, i, page_table)
                s0_re, s0_im = s_re[i], s_im[i]
            out, ckv, kr, sr, si = ab_layer(h, pos, past_ckv, past_kr, s0_re, s0_im, w, i)
            ckv_rows.append(ckv)
            kr_rows.append(kr)
            s5r.append(sr)
            s5i.append(si)
        else:
            if cache is None:
                past_k = jnp.zeros((Bsz, 0, SB_HEADS, SB_HEAD_DIM), x.dtype)
                past_v = jnp.zeros((Bsz, 0, SB_HEADS, SB_HEAD_DIM), x.dtype)
                wkv0 = jnp.zeros((Bsz, RW_HEADS, RW_HEAD_DIM, RW_HEAD_DIM), jnp.float32)
                shift0 = jnp.zeros((Bsz, RW_COLS), x.dtype)
            else:
                past_k = gather_pages(c_k, i, page_table)
                past_v = gather_pages(c_v, i, page_table)
                wkv0, shift0 = s_wkv[i], s_shift[i]
            out, k, v, wkv, shift = cd_layer(h, pos, past_k, past_v, wkv0, shift0, w, i)
            k_rows.append(k)
            v_rows.append(v)
            wkvs.append(wkv)
            shifts.append(shift)
        x = x + g1 * out
        h = rmsnorm(x, w['norm2_g'][l]) * (1 + sc2) + sh2
        x = x + g2 * peer_ffn(h, w['peer_w_q'][l], w['peer_k1'][l], w['peer_k2'][l],
                              w['peer_u'][l], w['peer_v'][l])
    y = rmsnorm(x, w['final_g'])
    return (y, jnp.stack(ckv_rows), jnp.stack(kr_rows), jnp.stack(k_rows), jnp.stack(v_rows),
            jnp.stack(s5r), jnp.stack(s5i), jnp.stack(wkvs), jnp.stack(shifts))


def setup_inputs(seed: int = 0) -> dict:
    key = jax.random.key(seed)
    ks = iter(jax.random.split(key, 64))
    def nrm(shape, scale=1.0):
        return jax.random.normal(next(ks), shape, jnp.float32) * scale
    def gain(shape):
        return 1.0 + nrm(shape, 0.02)
    def unif(shape, lo, hi):
        return jax.random.uniform(next(ks), shape, jnp.float32, minval=lo, maxval=hi)
    n_pages = PAST_LEN // PAGE_SIZE
    n_used = DEC_BATCH * n_pages
    n_phys = n_used + n_used // POOL_EXTRA_DEN
    page_table = jax.random.permutation(next(ks), n_phys)[:n_used].reshape(DEC_BATCH, n_pages).astype(jnp.int32)
    s5_n = jnp.arange(S5_STATE, dtype=jnp.float32)
    D = D_MODEL
    return {
        'x_prompt': nrm((BATCH, SEQ, D)),
        'x_sample': nrm((DEC_BATCH, DEC_SEQ, D)),
        'c_prompt': nrm((BATCH, D)),
        'c_sample': nrm((DEC_BATCH, D)),
        'cache_mla_ckv': nrm((N_AB, n_phys, PAGE_SIZE, MLA_KV_RANK)),
        'cache_mla_krope': nrm((N_AB, n_phys, PAGE_SIZE, MLA_ROPE)),
        'cache_sb_k': nrm((N_CD, n_phys, PAGE_SIZE, SB_HEADS, SB_HEAD_DIM)),
        'cache_sb_v': nrm((N_CD, n_phys, PAGE_SIZE, SB_HEADS, SB_HEAD_DIM)),
        'page_table': page_table,
        'state_s5_re': nrm((N_AB, DEC_BATCH, S5_GROUPS, S5_STATE), 0.2),
        'state_s5_im': nrm((N_AB, DEC_BATCH, S5_GROUPS, S5_STATE), 0.2),
        'state_rwkv_wkv': nrm((N_CD, DEC_BATCH, RW_HEADS, RW_HEAD_DIM, RW_HEAD_DIM), 0.2),
        'state_rwkv_shift': nrm((N_CD, DEC_BATCH, RW_COLS)),
        'ada_w': nrm((DEPTH, D, 6 * D), 0.3 * D ** -0.5),
        'ada_b': nrm((DEPTH, 6 * D), 0.02),
        'norm1_g': gain((DEPTH, D)),
        'norm2_g': gain((DEPTH, D)),
        'final_g': gain((D,)),
        'ab_w_in': nrm((N_AB, D, AB_IN), D ** -0.5),
        'ab_w_out': nrm((N_AB, AB_MIX, D), AB_MIX ** -0.5),
        's5_lambda_re': -0.5 + nrm((N_AB, S5_GROUPS, S5_STATE), 0.01),
        's5_lambda_im': math.pi * s5_n + nrm((N_AB, S5_GROUPS, S5_STATE), 0.01),
        's5_log_dt': unif((N_AB, S5_GROUPS), math.log(S5_DT_MIN), math.log(S5_DT_MAX)),
        's5_b_re': nrm((N_AB, S5_GROUPS, S5_STATE, S5_GROUP), (2 * S5_GROUP) ** -0.5),
        's5_b_im': nrm((N_AB, S5_GROUPS, S5_STATE, S5_GROUP), (2 * S5_GROUP) ** -0.5),
        's5_c_re': nrm((N_AB, S5_GROUPS, S5_GROUP, S5_STATE), (2 * S5_STATE) ** -0.5),
        's5_c_im': nrm((N_AB, S5_GROUPS, S5_GROUP, S5_STATE), (2 * S5_STATE) ** -0.5),
        's5_d': nrm((N_AB, S5_GROUPS, S5_GROUP)),
        's5_w_glu': nrm((N_AB, S5_WIDTH, S5_WIDTH), S5_WIDTH ** -0.5),
        's5_b_glu': nrm((N_AB, S5_WIDTH), 0.02),
        'mla_g_q': gain((N_AB, MLA_Q_RANK)),
        'mla_w_uq': nrm((N_AB, MLA_Q_RANK, MLA_HEADS, MLA_NOPE + MLA_ROPE), MLA_Q_RANK ** -0.5),
        'mla_g_kv': gain((N_AB, MLA_KV_RANK)),
        'mla_w_uk': nrm((N_AB, MLA_KV_RANK, MLA_HEADS, MLA_NOPE), MLA_KV_RANK ** -0.5),
        'mla_w_uv': nrm((N_AB, MLA_KV_RANK, MLA_HEADS, MLA_V), MLA_KV_RANK ** -0.5),
        'cd_w_in': nrm((N_CD, D, CD_IN), D ** -0.5),
        'cd_w_out': nrm((N_CD, CD_MIX, D), CD_MIX ** -0.5),
        'rwkv_mu': unif((N_CD, RW_COLS), 0.0, 1.0),
        'rwkv_w0': unif((N_CD, RW_WIDTH), -6.0, 1.0),
        'rwkv_w2': nrm((N_CD, RW_DECAY_RANK, RW_WIDTH), 0.5 * RW_DECAY_RANK ** -0.5),
        'rwkv_a0': nrm((N_CD, RW_WIDTH), 0.1),
        'rwkv_a2': nrm((N_CD, RW_A_RANK, RW_WIDTH), 0.5 * RW_A_RANK ** -0.5),
        'rwkv_g2': nrm((N_CD, RW_GATE_RANK, RW_WIDTH), RW_GATE_RANK ** -0.5),
        'rwkv_k_k': 0.85 + nrm((N_CD, RW_WIDTH), 0.05),
        'rwkv_k_a': 1.0 + nrm((N_CD, RW_WIDTH), 0.05),
        'rwkv_r_k': nrm((N_CD, RW_HEADS, RW_HEAD_DIM), 0.1),
        'rwkv_ln_w': gain((N_CD, RW_WIDTH)),
        'rwkv_ln_b': nrm((N_CD, RW_WIDTH), 0.02),
        'peer_w_q': nrm((DEPTH, D, PEER_HEADS * PEER_QUERY), D ** -0.5),
        'peer_k1': nrm((DEPTH, PEER_HEADS, PEER_N_KEYS, PEER_QUERY // 2), (PEER_QUERY // 2) ** -0.5),
        'peer_k2': nrm((DEPTH, PEER_HEADS, PEER_N_KEYS, PEER_QUERY // 2), (PEER_QUERY // 2) ** -0.5),
        'peer_u': nrm((DEPTH, PEER_EXPERTS, D), D ** -0.5),
        'peer_v': nrm((DEPTH, PEER_EXPERTS, D), 1.0),
    }


def reference(x_prompt, x_sample, c_prompt, c_sample, cache_mla_ckv, cache_mla_krope, cache_sb_k,
              cache_sb_v, page_table, state_s5_re, state_s5_im, state_rwkv_wkv, state_rwkv_shift,
              ada_w, ada_b, norm1_g, norm2_g, final_g, ab_w_in, ab_w_out, s5_lambda_re, s5_lambda_im,
              s5_log_dt, s5_b_re, s5_b_im, s5_c_re, s5_c_im, s5_d, s5_w_glu, s5_b_glu, mla_g_q,
              mla_w_uq, mla_g_kv, mla_w_uk, mla_w_uv, cd_w_in, cd_w_out, rwkv_mu, rwkv_w0, rwkv_w2,
              rwkv_a0, rwkv_a2, rwkv_g2, rwkv_k_k, rwkv_k_a, rwkv_r_k, rwkv_ln_w, rwkv_ln_b,
              peer_w_q, peer_k1, peer_k2, peer_u, peer_v):
    w = dict(ada_w=ada_w, ada_b=ada_b, norm1_g=norm1_g, norm2_g=norm2_g, final_g=final_g,
             ab_w_in=ab_w_in, ab_w_out=ab_w_out, s5_lambda_re=s5_lambda_re, s5_lambda_im=s5_lambda_im,
             s5_log_dt=s5_log_dt, s5_b_re=s5_b_re, s5_b_im=s5_b_im, s5_c_re=s5_c_re, s5_c_im=s5_c_im,
             s5_d=s5_d, s5_w_glu=s5_w_glu, s5_b_glu=s5_b_glu, mla_g_q=mla_g_q, mla_w_uq=mla_w_uq,
             mla_g_kv=mla_g_kv, mla_w_uk=mla_w_uk, mla_w_uv=mla_w_uv, cd_w_in=cd_w_in,
             cd_w_out=cd_w_out, rwkv_mu=rwkv_mu, rwkv_w0=rwkv_w0, rwkv_w2=rwkv_w2, rwkv_a0=rwkv_a0,
             rwkv_a2=rwkv_a2, rwkv_g2=rwkv_g2, rwkv_k_k=rwkv_k_k, rwkv_k_a=rwkv_k_a,
             rwkv_r_k=rwkv_r_k, rwkv_ln_w=rwkv_ln_w, rwkv_ln_b=rwkv_ln_b, peer_w_q=peer_w_q,
             peer_k1=peer_k1, peer_k2=peer_k2, peer_u=peer_u, peer_v=peer_v)
    (y_prompt, ckv_p, kr_p, sbk_p, sbv_p, s5r_p, s5i_p, wkv_p, sh_p) = trunk(x_prompt, c_prompt, w, None)
    (y_sample, ckv_s, kr_s, sbk_s, sbv_s, s5r_s, s5i_s, wkv_s, sh_s) = trunk(
        x_sample, c_sample, w,
        (cache_mla_ckv, cache_mla_krope, cache_sb_k, cache_sb_v, page_table,
         state_s5_re, state_s5_im, state_rwkv_wkv, state_rwkv_shift))
    return (y_prompt, y_sample, ckv_p, kr_p, sbk_p, sbv_p, s5r_p, s5i_p, wkv_p, sh_p,
            ckv_s, kr_s, sbk_s, sbv_s, s5r_s, s5i_s, wkv_s, sh_s)
```

```python
import functools
import math

import jax
import jax.numpy as jnp
from jax import lax
from jax.experimental import pallas as pl
from jax.experimental.pallas import tpu as pltpu

F32 = jnp.float32
BF16 = jnp.bfloat16

NORM_EPS = 1e-6
ROPE_BASE = 10000.0
PAGE_SIZE = 128
S5_GROUP = 16
S5_STATE = 64
MLA_HEADS = 8
MLA_NOPE = 64
MLA_ROPE = 32
MLA_V = 64
MLA_KV_RANK = 256
MLA_KPAD = 384
SB_HEADS = 8
SB_HEAD_DIM = 64
RW_HEADS = 8
RW_HEAD_DIM = 64
RW_DECAY_RANK = 64
RW_A_RANK = 64
RW_GATE_RANK = 128
RW_GN_EPS = 64e-5
PEER_HEADS = 8
PEER_N_KEYS = 128
PEER_TOPK = 16
NEG_BIG = -3.0e38

VMEM_LIMIT = 48 * 1024 * 1024


def _params(*sem):
    return pltpu.CompilerParams(dimension_semantics=sem, vmem_limit_bytes=VMEM_LIMIT)


def _dot(a, b):
    return jnp.dot(a, b, preferred_element_type=F32)


def _dot_nt(a, b):
    return lax.dot_general(a, b, (((1,), (1,)), ((), ())), preferred_element_type=F32)


def _dot_tn(a, b):
    return lax.dot_general(a, b, (((0,), (0,)), ((), ())), preferred_element_type=F32)


def _split3(a):
    hi = a.astype(BF16)
    r1 = a - hi.astype(F32)
    mid = r1.astype(BF16)
    lo = (r1 - mid.astype(F32)).astype(BF16)
    return hi, mid, lo


def _dotx(dot, a, b):
    a0, a1, a2 = _split3(a)
    b0, b1, b2 = _split3(b)
    return (dot(a0, b0) + (dot(a0, b1) + dot(a1, b0))
            + (dot(a1, b1) + dot(a0, b2) + dot(a2, b0)))


def _dot_exact_rhs(a, b_bf16):
    a0, a1, a2 = _split3(a)
    return _dot(a0, b_bf16) + _dot(a1, b_bf16) + _dot(a2, b_bf16)


def _gelu(x):
    c = math.sqrt(2.0 / math.pi)
    return 0.5 * x * (1.0 + jnp.tanh(c * (x + 0.044715 * (x * x * x))))


def _sigmoid(x):
    return 1.0 / (1.0 + jnp.exp(-x))


def _softplus(x):
    return jnp.maximum(x, 0.0) + jnp.log1p(jnp.exp(-jnp.abs(x)))


def _rms(x):
    return x * lax.rsqrt(jnp.mean(x * x, axis=-1, keepdims=True) + NORM_EPS)


def _ada_kernel(c_ref, w_ref, b_ref, o_ref):
    c = c_ref[...]
    s = (c * _sigmoid(c)).astype(BF16)
    o_ref[...] = _dot(s, w_ref[...].astype(BF16)) + b_ref[...]


def _ada(c, ada_w, ada_b):
    L, D, N = ada_w.shape
    Bc = c.shape[0]
    tn = 1024
    return pl.pallas_call(
        _ada_kernel,
        grid=(L, N // tn),
        in_specs=[pl.BlockSpec((Bc, D), lambda l, j: (0, 0)),
                  pl.BlockSpec((None, D, tn), lambda l, j: (l, 0, j)),
                  pl.BlockSpec((None, 1, tn), lambda l, j: (l, 0, j))],
        out_specs=pl.BlockSpec((None, Bc, tn), lambda l, j: (l, 0, j)),
        out_shape=jax.ShapeDtypeStruct((L, Bc, N), F32),
        compiler_params=_params("arbitrary", "arbitrary"),
        name="ada",
    )(c, ada_w, ada_b.reshape(L, 1, N))


def _mod_spec(T, tm, D):
    if tm <= T:
        return pl.BlockSpec((None, 1, D), lambda i: (i * tm // T, 0, 0))
    return pl.BlockSpec((None, tm, D), lambda i: (0, i, 0))


def _mod_rows(m, T, tm):
    if tm <= T:
        return m[:, None, :]
    return jnp.repeat(m, T, axis=0)[None]


def _norm_mod_mm_kernel(x_ref, g_ref, sc_ref, sh_ref, w_ref, *o_refs):
    h = _rms(x_ref[...]) * g_ref[...] * (1.0 + sc_ref[...]) + sh_ref[...]
    res = _dot(h.astype(BF16), w_ref[...])
    off = 0
    for o_ref in o_refs:
        n = o_ref.shape[1]
        o_ref[...] = res[:, off:off + n]
        off += n


def _norm_mod_mm(x, g, sc, sh, w, T, tm, widths):
    M, D = x.shape
    N = w.shape[1]
    assert sum(widths) == N
    return pl.pallas_call(
        _norm_mod_mm_kernel,
        grid=(M // tm,),
        in_specs=[pl.BlockSpec((tm, D), lambda i: (i, 0)),
                  pl.BlockSpec((1, D), lambda i: (0, 0)),
                  _mod_spec(T, tm, D), _mod_spec(T, tm, D),
                  pl.BlockSpec((D, N), lambda i: (0, 0))],
        out_specs=[pl.BlockSpec((tm, n), lambda i: (i, 0)) for n in widths],
        out_shape=[jax.ShapeDtypeStruct((M, n), F32) for n in widths],
        compiler_params=_params("arbitrary"),
        name="norm_mod_mm",
    )(x, g.reshape(1, D), _mod_rows(sc, T, tm), _mod_rows(sh, T, tm), w)


def _mm_resid_kernel(a_ref, w_ref, x_ref, gate_ref, o_ref):
    o_ref[...] = x_ref[...] + gate_ref[...] * _dot(a_ref[...].astype(BF16), w_ref[...])


def _mm_resid(a, w, x, gate, T, tm):
    M, K = a.shape
    D = w.shape[1]
    return pl.pallas_call(
        _mm_resid_kernel,
        grid=(M // tm,),
        in_specs=[pl.BlockSpec((tm, K), lambda i: (i, 0)),
                  pl.BlockSpec((K, D), lambda i: (0, 0)),
                  pl.BlockSpec((tm, D), lambda i: (i, 0)),
                  _mod_spec(T, tm, D)],
        out_specs=pl.BlockSpec((tm, D), lambda i: (i, 0)),
        out_shape=jax.ShapeDtypeStruct((M, D), F32),
        compiler_params=_params("arbitrary"),
        name="mm_resid",
    )(a, w, x, _mod_rows(gate, T, tm))


def _rmsnorm_kernel(x_ref, g_ref, o_ref):
    o_ref[...] = _rms(x_ref[...]) * g_ref[...]


def _rmsnorm(x, g, tm):
    M, D = x.shape
    return pl.pallas_call(
        _rmsnorm_kernel,
        grid=(M // tm,),
        in_specs=[pl.BlockSpec((tm, D), lambda i: (i, 0)),
                  pl.BlockSpec((1, D), lambda i: (0, 0))],
        out_specs=pl.BlockSpec((tm, D), lambda i: (i, 0)),
        out_shape=jax.ShapeDtypeStruct((M, D), F32),
        compiler_params=_params("arbitrary"),
        name="final_norm",
    )(x, g.reshape(1, D))


def _s5_kernel(u_ref, s0r_ref, s0i_ref, abr_ref, abi_ref, cor_ref, coi_ref,
               wbr_ref, wbi_ref, wcr_ref, wci_ref, d_ref, wg_ref, bg_ref,
               o_ref, sr_ref, si_ref, xr_buf, xi_buf, *, nb, tc):
    c = pl.program_id(0)

    @pl.when(c == 0)
    def _():
        sr_ref[...] = s0r_ref[...]
        si_ref[...] = s0i_ref[...]

    u = u_ref[...]
    ub = u.astype(BF16)
    bu_r = _dot(ub, wbr_ref[...])
    bu_i = _dot(ub, wbi_ref[...])
    cor, coi = cor_ref[...], coi_ref[...]
    xr_buf[...] = cor * bu_r - coi * bu_i
    xi_buf[...] = cor * bu_i + coi * bu_r
    abr, abi = abr_ref[...], abi_ref[...]

    def step(t, carry):
        xr, xi = carry
        rows = pl.ds(pl.multiple_of(t * nb, nb), nb)
        nr = abr * xr - abi * xi + xr_buf[rows, :]
        ni = abr * xi + abi * xr + xi_buf[rows, :]
        xr_buf[rows, :] = nr
        xi_buf[rows, :] = ni
        return nr, ni

    xr, xi = lax.fori_loop(0, tc, step, (sr_ref[...], si_ref[...]))
    sr_ref[...] = xr
    si_ref[...] = xi

    y = (_dot(xr_buf[...].astype(BF16), wcr_ref[...])
         - _dot(xi_buf[...].astype(BF16), wci_ref[...]) + d_ref[...] * u)
    z = _gelu(y)
    o_ref[...] = z * _sigmoid(_dot(z.astype(BF16), wg_ref[...]) + bg_ref[...])


def _s5_consts(lam_re, lam_im, log_dt, b_re, b_im, c_re, c_im):
    G, P = lam_re.shape
    Hc = b_re.shape[-1]
    dt = jnp.exp(log_dt)[:, None]
    mag = jnp.exp(lam_re * dt)
    ab_re, ab_im = mag * jnp.cos(lam_im * dt), mag * jnp.sin(lam_im * dt)
    den = lam_re * lam_re + lam_im * lam_im
    nr = ab_re - 1.0
    co_re = (nr * lam_re + ab_im * lam_im) / den
    co_im = (ab_im * lam_re - nr * lam_im) / den
    eye = jnp.eye(G, dtype=F32)
    wb = lambda b: jnp.einsum('gph,gk->ghkp', b, eye).reshape(G * Hc, G * P).astype(BF16)
    wc = lambda cc: jnp.einsum('ghp,gk->gpkh', cc, eye).reshape(G * P, G * Hc).astype(BF16)
    flat = lambda a: a.reshape(1, G * P)
    return (flat(ab_re), flat(ab_im), flat(co_re), flat(co_im),
            wb(b_re), wb(b_im), wc(c_re), wc(c_im))


def _s5(u_tb, s0_re, s0_im, consts, d, w_glu, b_glu, nb, tc):
    M, W = u_tb.shape
    T = M // nb
    GP = s0_re.shape[1]
    ab_re, ab_im, co_re, co_im, wbr, wbi, wcr, wci = consts
    rows = tc * nb
    full = lambda shape: pl.BlockSpec(shape, lambda c: (0,) * len(shape))
    return pl.pallas_call(
        functools.partial(_s5_kernel, nb=nb, tc=tc),
        grid=(T // tc,),
        in_specs=[pl.BlockSpec((rows, W), lambda c: (c, 0)),
                  full((nb, GP)), full((nb, GP)),
                  full((1, GP)), full((1, GP)), full((1, GP)), full((1, GP)),
                  full((W, GP)), full((W, GP)), full((GP, W)), full((GP, W)),
                  full((1, W)), full((W, W)), full((1, W))],
        out_specs=[pl.BlockSpec((rows, W), lambda c: (c, 0)), full((nb, GP)), full((nb, GP))],
        out_shape=[jax.ShapeDtypeStruct((M, W), F32),
                   jax.ShapeDtypeStruct((nb, GP), F32),
                   jax.ShapeDtypeStruct((nb, GP), F32)],
        scratch_shapes=[pltpu.VMEM((rows, GP), F32), pltpu.VMEM((rows, GP), F32)],
        compiler_params=_params("arbitrary"),
        name="s5",
    )(u_tb, s0_re, s0_im, ab_re, ab_im, co_re, co_im, wbr, wbi, wcr, wci,
      d.reshape(1, W), w_glu.astype(BF16), b_glu.reshape(1, W))


def _rope_tables(pos, width):
    half = MLA_ROPE // 2
    inv = ROPE_BASE ** (-jnp.arange(half, dtype=F32) / half)
    ang = pos.astype(F32)[:, None] * inv[None, :]
    cos, sin = jnp.cos(ang), jnp.sin(ang)
    pad = jnp.zeros((pos.shape[0], width - MLA_ROPE), F32)
    return (jnp.concatenate([cos, cos, pad], axis=1), jnp.concatenate([-sin, sin, pad], axis=1))


def _swap_halves(w):
    half = w.shape[-1] // 2
    return jnp.concatenate([w[..., half:], w[..., :half]], axis=-1)


def _pad_cols(w, width):
    return jnp.pad(w, [(0, 0)] * (w.ndim - 1) + [(0, width - w.shape[-1])])


def _mla_prep_kernel(cq_ref, ckv_ref, kr_ref, krs_ref, c2_ref, s2_ref, gq_ref, gkv_ref,
                     wq_ref, wuk_ref, q_ref, kcat_ref, ckvn_ref, krot_ref):
    H = q_ref.shape[0]
    c2, s2 = c2_ref[...], s2_ref[...]
    cqn = (_rms(cq_ref[...]) * gq_ref[...]).astype(BF16)
    qall = _dot(cqn, wq_ref[...])
    for h in range(H):
        base = h * 384
        nope = qall[:, base:base + 128].astype(BF16)
        rot = qall[:, base + 128:base + 256] * c2 + qall[:, base + 256:base + 384] * s2
        q_ref[h, :, 0:MLA_KV_RANK] = _dot(nope, wuk_ref[h]).astype(BF16)
        q_ref[h, :, MLA_KV_RANK:MLA_KPAD] = rot.astype(BF16)
    ckvn = _rms(ckv_ref[...]) * gkv_ref[...]
    krot = kr_ref[...] * c2 + krs_ref[...] * s2
    ckvn_ref[...] = ckvn
    krot_ref[...] = krot[:, :MLA_ROPE]
    kcat_ref[:, 0:MLA_KV_RANK] = ckvn.astype(BF16)
    kcat_ref[:, MLA_KV_RANK:MLA_KPAD] = krot.astype(BF16)


def _mla_prep(cq, ckv, kr, krs, c2, s2, g_q, g_kv, wq, wuk, tm):
    M = cq.shape[0]
    H = MLA_HEADS
    col = lambda w: pl.BlockSpec((tm, w), lambda i: (i, 0))
    full = lambda shape: pl.BlockSpec(shape, lambda i: (0,) * len(shape))
    return pl.pallas_call(
        _mla_prep_kernel,
        grid=(M // tm,),
        in_specs=[col(384), col(256), col(128), col(128), col(128), col(128),
                  full((1, 384)), full((1, 256)), full((384, H * 384)), full((H, 128, 256))],
        out_specs=[pl.BlockSpec((H, tm, MLA_KPAD), lambda i: (0, i, 0)),
                   col(MLA_KPAD), col(MLA_KV_RANK), col(MLA_ROPE)],
        out_shape=[jax.ShapeDtypeStruct((H, M, MLA_KPAD), BF16),
                   jax.ShapeDtypeStruct((M, MLA_KPAD), BF16),
                   jax.ShapeDtypeStruct((M, MLA_KV_RANK), F32),
                   jax.ShapeDtypeStruct((M, MLA_ROPE), F32)],
        compiler_params=_params("arbitrary"),
        name="mla_prep",
    )(cq, ckv, kr, krs, c2, s2, g_q.reshape(1, -1), g_kv.reshape(1, -1), wq, wuk)


def _mla_weights(w_uq, w_uk):
    C, H, _ = w_uq.shape
    nope = _pad_cols(w_uq[..., :MLA_NOPE], 128)
    rope = w_uq[..., MLA_NOPE:]
    wq = jnp.concatenate([nope, _pad_cols(rope, 128), _pad_cols(_swap_halves(rope), 128)], axis=-1)
    wuk = jnp.pad(jnp.transpose(w_uk, (1, 2, 0)), ((0, 0), (0, 128 - MLA_NOPE), (0, 0)))
    return wq.reshape(C, H * 384).astype(BF16), wuk.astype(BF16)


def _mla_attn_kernel(q_ref, k_ref, wuv_ref, o_ref, m_s, l_s, acc_s, *, tq, tk, scale):
    qi = pl.program_id(1)
    H = q_ref.shape[0]
    rows = H * tq
    q = q_ref[...].reshape(rows, MLA_KPAD)
    m_s[...] = jnp.full((rows, 1), NEG_BIG, F32)
    l_s[...] = jnp.zeros((rows, 1), F32)
    acc_s[...] = jnp.zeros((rows, MLA_KV_RANK), F32)
    row_t = qi * tq + lax.broadcasted_iota(jnp.int32, (H, tq, 1), 1).reshape(rows, 1)

    def body(kb, carry):
        k = k_ref[pl.ds(pl.multiple_of(kb * tk, tk), tk), :]
        s = _dot_nt(q, k) * scale
        kpos = kb * tk + lax.broadcasted_iota(jnp.int32, (1, tk), 1)
        s = jnp.where(kpos <= row_t, s, NEG_BIG)
        m_prev = m_s[...]
        m_new = jnp.maximum(m_prev, jnp.max(s, axis=-1, keepdims=True))
        alpha = jnp.exp(m_prev - m_new)
        p = jnp.exp(s - m_new)
        l_s[...] = alpha * l_s[...] + jnp.sum(p, axis=-1, keepdims=True)
        acc_s[...] = alpha * acc_s[...] + _dot(p.astype(BF16), k[:, :MLA_KV_RANK])
        m_s[...] = m_new
        return carry

    lax.fori_loop(0, (qi * tq + tq + tk - 1) // tk, body, 0)
    o = (acc_s[...] / l_s[...]).astype(BF16)
    for h in range(H):
        o_ref[:, h * MLA_V:(h + 1) * MLA_V] = _dot(o[h * tq:(h + 1) * tq], wuv_ref[h])


def _mla_attn(q, kcat, wuv, B, T, tq, tk):
    H = q.shape[0]
    nq = T // tq
    scale = (MLA_NOPE + MLA_ROPE) ** -0.5
    return pl.pallas_call(
        functools.partial(_mla_attn_kernel, tq=tq, tk=tk, scale=scale),
        grid=(B, nq),
        in_specs=[pl.BlockSpec((H, tq, MLA_KPAD), lambda b, i: (0, b * nq + i, 0)),
                  pl.BlockSpec((T, MLA_KPAD), lambda b, i: (b, 0)),
                  pl.BlockSpec((H, MLA_KV_RANK, MLA_V), lambda b, i: (0, 0, 0))],
        out_specs=pl.BlockSpec((tq, H * MLA_V), lambda b, i: (b * nq + i, 0)),
        out_shape=jax.ShapeDtypeStruct((B * T, H * MLA_V), F32),
        scratch_shapes=[pltpu.VMEM((H * tq, 1), F32), pltpu.VMEM((H * tq, 1), F32),
                        pltpu.VMEM((H * tq, MLA_KV_RANK), F32)],
        compiler_params=_params("arbitrary", "arbitrary"),
        name="mla_attn",
    )(q, kcat, wuv)


def _mla_decode_kernel(pt_ref, q_ref, knew_ref, *refs, npg, nt, scale):
    ckv_refs = refs[:npg]
    kr_refs = refs[npg:2 * npg]
    o_ref, m_s, l_s, acc_s, kbuf = refs[2 * npg:]
    j = pl.program_id(1)
    q = q_ref[...]
    rows = q.shape[0]

    def update(s, vals):
        m_prev = m_s[...]
        m_new = jnp.maximum(m_prev, jnp.max(s, axis=-1, keepdims=True))
        alpha = jnp.exp(m_prev - m_new)
        p = jnp.exp(s - m_new)
        l_s[...] = alpha * l_s[...] + jnp.sum(p, axis=-1, keepdims=True)
        acc_s[...] = alpha * acc_s[...] + _dot(p.astype(BF16), vals)
        m_s[...] = m_new

    @pl.when(j == 0)
    def _():
        kbuf[...] = jnp.zeros(kbuf.shape, BF16)
        m_s[...] = jnp.full((rows, 1), NEG_BIG, F32)
        l_s[...] = jnp.zeros((rows, 1), F32)
        acc_s[...] = jnp.zeros((rows, MLA_KV_RANK), F32)
        knew = knew_ref[...]
        s = _dot_nt(q, knew) * scale
        t_row = lax.broadcasted_iota(jnp.int32, (rows, 1), 0) % nt
        col = lax.broadcasted_iota(jnp.int32, (1, PAGE_SIZE), 1)
        update(jnp.where(col <= t_row, s, NEG_BIG), knew[:, :MLA_KV_RANK])

    for r in range(npg):
        lat = ckv_refs[r][...].astype(BF16)
        kbuf[:, 0:MLA_KV_RANK] = lat
        kbuf[:, MLA_KV_RANK:MLA_KV_RANK + MLA_ROPE] = kr_refs[r][...].astype(BF16)
        update(_dot_nt(q, kbuf[...]) * scale, lat)

    @pl.when(j == pl.num_programs(1) - 1)
    def _():
        o_ref[...] = acc_s[...] / l_s[...]


def _mla_decode(q, knew, cache_ckv, cache_kr, page_table, layer, nt, npg=8):
    B, rows, _ = q.shape
    n_pages = page_table.shape[1]
    page = lambda r, w: pl.BlockSpec(
        (None, None, PAGE_SIZE, w), lambda b, j, pt: (layer, pt[b, j * npg + r], 0, 0))
    grid_spec = pltpu.PrefetchScalarGridSpec(
        num_scalar_prefetch=1,
        grid=(B, n_pages // npg),
        in_specs=[pl.BlockSpec((None, rows, MLA_KPAD), lambda b, j, pt: (b, 0, 0)),
                  pl.BlockSpec((None, PAGE_SIZE, MLA_KPAD), lambda b, j, pt: (b, 0, 0))]
        + [page(r, MLA_KV_RANK) for r in range(npg)] + [page(r, MLA_ROPE) for r in range(npg)],
        out_specs=pl.BlockSpec((None, rows, MLA_KV_RANK), lambda b, j, pt: (b, 0, 0)),
        scratch_shapes=[pltpu.VMEM((rows, 1), F32), pltpu.VMEM((rows, 1), F32),
                        pltpu.VMEM((rows, MLA_KV_RANK), F32),
                        pltpu.VMEM((PAGE_SIZE, MLA_KPAD), BF16)])
    scale = (MLA_NOPE + MLA_ROPE) ** -0.5
    return pl.pallas_call(
        functools.partial(_mla_decode_kernel, npg=npg, nt=nt, scale=scale),
        grid_spec=grid_spec,
        out_shape=jax.ShapeDtypeStruct((B, rows, MLA_KV_RANK), F32),
        compiler_params=_params("arbitrary", "arbitrary"),
        name="mla_decode",
    )(page_table, q, knew, *([cache_ckv] * npg), *([cache_kr] * npg))


def _bmm_kernel(x_ref, w_ref, o_ref):
    o_ref[...] = _dot(x_ref[...].astype(BF16), w_ref[...])


def _bmm(x, w):
    H, M, K = x.shape
    N = w.shape[2]
    return pl.pallas_call(
        _bmm_kernel,
        grid=(H,),
        in_specs=[pl.BlockSpec((None, M, K), lambda h: (h, 0, 0)),
                  pl.BlockSpec((None, K, N), lambda h: (h, 0, 0))],
        out_specs=pl.BlockSpec((None, M, N), lambda h: (h, 0, 0)),
        out_shape=jax.ShapeDtypeStruct((H, M, N), F32),
        compiler_params=_params("arbitrary"),
        name="bmm",
    )(x, w)


def _ab_in_weights(w_in, s5_width):
    a, b, c = s5_width, s5_width + 384, s5_width + 384 + MLA_KV_RANK
    u, cq, ckv, kr = w_in[:, :a], w_in[:, a:b], w_in[:, b:c], w_in[:, c:]
    return jnp.concatenate(
        [cq, u, ckv, _pad_cols(kr, 128), _pad_cols(_swap_halves(kr), 128)], axis=1).astype(BF16)


def _ab_layer(x, mods, g_norm, pos, B, T, tm, cache, wts, tq=128, tk=256):
    sh1, sc1, g1 = mods
    M = B * T
    W = wts['s5_d'].size
    GP = wts['s5_consts'][0].size
    cq, u, ckv, kr, krs = _norm_mod_mm(x, g_norm, sc1, sh1, wts['w_in'], T, tm,
                                       (384, W, MLA_KV_RANK, 128, 128))
    u_tb = jnp.transpose(u.reshape(B, T, W), (1, 0, 2)).reshape(M, W)
    if cache is None:
        s0_re = jnp.zeros((B, GP), F32)
        s0_im = jnp.zeros((B, GP), F32)
    else:
        s0_re, s0_im = cache['s5_re'].reshape(B, GP), cache['s5_im'].reshape(B, GP)
    s5_tb, s_re, s_im = _s5(u_tb, s0_re, s0_im, wts['s5_consts'], wts['s5_d'], wts['s5_w_glu'],
                            wts['s5_b_glu'], nb=B, tc=min(T, 64))
    s5_out = jnp.transpose(s5_tb.reshape(T, B, W), (1, 0, 2)).reshape(M, W)
    c2, s2 = _rope_tables(pos, 128)
    c2, s2 = jnp.tile(c2, (B, 1)), jnp.tile(s2, (B, 1))
    q, kcat, ckvn, krot = _mla_prep(cq, ckv, kr, krs, c2, s2, wts['mla_g_q'], wts['mla_g_kv'],
                                    wts['mla_wq'], wts['mla_wuk'], tm)
    H = MLA_HEADS
    if cache is None:
        o = _mla_attn(q, kcat, wts['mla_wuv'], B, T, tq, tk)
    else:
        qd = jnp.transpose(q.reshape(H, B, T, MLA_KPAD), (1, 0, 2, 3)).reshape(B, H * T, MLA_KPAD)
        knew = jnp.pad(kcat.reshape(B, T, MLA_KPAD), ((0, 0), (0, PAGE_SIZE - T), (0, 0)))
        o_lat = _mla_decode(qd, knew, cache['ckv'], cache['kr'], cache['page_table'],
                            cache['layer'], T)
        o_lat = jnp.transpose(o_lat.reshape(B, H, T, MLA_KV_RANK), (1, 0, 2, 3)).reshape(H, M, -1)
        o = jnp.transpose(_bmm(o_lat, wts['mla_wuv']), (1, 0, 2)).reshape(M, H * MLA_V)
    mix = jnp.concatenate([s5_out, o], axis=1)
    x = _mm_resid(mix, wts['w_out'], x, g1, T, tm)
    return x, ckvn, krot, s_re, s_im


def _suffix_sums(ln, tri):
    hi = ln.astype(BF16)
    mid = (ln - hi.astype(F32)).astype(BF16)
    return _dot(hi, tri) + _dot(mid, tri)


def _sb_attn_kernel(q_ref, k_ref, v_ref, tri_ref, o_ref, acc_s, r_s, *, tq, scale):
    qi = pl.program_id(2)
    q = q_ref[...]
    tri = tri_ref[...]
    acc_s[...] = jnp.zeros(acc_s.shape, F32)
    r_s[...] = jnp.zeros(r_s.shape, F32)

    def block(kb, diag):
        rows = pl.ds(pl.multiple_of(kb * tq, tq), tq)
        z = _dot_nt(q, k_ref[rows, :]) * scale
        sp = _softplus(z)
        if diag:
            mask = (lax.broadcasted_iota(jnp.int32, (tq, tq), 1)
                    < lax.broadcasted_iota(jnp.int32, (tq, tq), 0))
            ln = jnp.where(mask, -sp, 0.0)
        else:
            ln = -sp
        a = jnp.exp(z - sp + _suffix_sums(ln, tri) + r_s[...])
        if diag:
            a = jnp.where(mask, a, 0.0)
        acc_s[...] += _dot(a.astype(BF16), v_ref[rows, :])
        r_s[...] += jnp.sum(ln, axis=-1, keepdims=True)

    block(qi, True)

    def body(i, carry):
        block(qi - 1 - i, False)
        return carry

    lax.fori_loop(0, qi, body, 0)
    o_ref[...] = acc_s[...]


def _tri(n):
    return (lax.broadcasted_iota(jnp.int32, (n, n), 0)
            > lax.broadcasted_iota(jnp.int32, (n, n), 1)).astype(BF16)


def _sb_attn(q, k, v, tq):
    B, H, T, d = q.shape
    seq = pl.BlockSpec((None, None, T, d), lambda b, h, i: (b, h, 0, 0))
    blk = pl.BlockSpec((None, None, tq, d), lambda b, h, i: (b, h, i, 0))
    return pl.pallas_call(
        functools.partial(_sb_attn_kernel, tq=tq, scale=d ** -0.5),
        grid=(B, H, T // tq),
        in_specs=[blk, seq, seq, pl.BlockSpec((tq, tq), lambda b, h, i: (0, 0))],
        out_specs=blk,
        out_shape=jax.ShapeDtypeStruct((B, H, T, d), F32),
        scratch_shapes=[pltpu.VMEM((tq, d), F32), pltpu.VMEM((tq, 1), F32)],
        compiler_params=_params("arbitrary", "arbitrary", "arbitrary"),
        name="sb_attn",
    )(q, k, v, _tri(tq))


def _sb_decode_kernel(pt_ref, q_ref, knew_ref, vnew_ref, tri_ref, bd_ref, *refs, npg, nt, scale):
    k_refs = refs[:npg]
    v_refs = refs[npg:2 * npg]
    o_ref, acc_s, r_s = refs[2 * npg:]
    j = pl.program_id(1)
    q = q_ref[...]
    rows = q.shape[0]
    nh = rows // nt
    tri = tri_ref[...]

    def block(keys, vals, mask):
        z = _dot_nt(q, keys) * scale
        sp = _softplus(z)
        ln = -sp if mask is None else jnp.where(mask, -sp, 0.0)
        a = jnp.exp(z - sp + _suffix_sums(ln, tri) + r_s[...])
        if mask is not None:
            a = jnp.where(mask, a, 0.0)
        acc_s[...] += _dot(a.astype(BF16), vals)
        r_s[...] += jnp.sum(ln, axis=-1, keepdims=True)

    @pl.when(j == 0)
    def _():
        acc_s[...] = jnp.zeros(acc_s.shape, F32)
        r_s[...] = jnp.zeros(r_s.shape, F32)
        t_row = lax.broadcasted_iota(jnp.int32, (rows, 1), 0) // nh
        col = lax.broadcasted_iota(jnp.int32, (1, PAGE_SIZE), 1)
        block(knew_ref[...], vnew_ref[...], col < t_row)

    for r in range(npg):
        block(k_refs[r][...].astype(BF16), v_refs[r][...].astype(BF16), None)

    @pl.when(j == pl.num_programs(1) - 1)
    def _():
        own = acc_s[...] * bd_ref[...]
        o_ref[...] = jnp.sum(own.reshape(nt, nh, own.shape[1]), axis=1)


def _sb_decode(qbd, knew, vnew, cache_k, cache_v, page_table, layer, nt, npg=4):
    B, rows, HD = qbd.shape
    nh = rows // nt
    n_pages = page_table.shape[1]
    page = lambda r: pl.BlockSpec(
        (None, None, PAGE_SIZE, HD),
        lambda b, j, pt: (layer, pt[b, n_pages - 1 - (j * npg + r)], 0, 0))
    per_b = lambda n: pl.BlockSpec((None, n, HD), lambda b, j, pt: (b, 0, 0))
    bd = (lax.broadcasted_iota(jnp.int32, (rows, HD), 1) // (HD // nh)
          == lax.broadcasted_iota(jnp.int32, (rows, HD), 0) % nh).astype(F32)
    grid_spec = pltpu.PrefetchScalarGridSpec(
        num_scalar_prefetch=1,
        grid=(B, n_pages // npg),
        in_specs=[per_b(rows), per_b(PAGE_SIZE), per_b(PAGE_SIZE),
                  pl.BlockSpec((PAGE_SIZE, PAGE_SIZE), lambda b, j, pt: (0, 0)),
                  pl.BlockSpec((rows, HD), lambda b, j, pt: (0, 0))]
        + [page(r) for r in range(npg)] * 2,
        out_specs=per_b(nt),
        scratch_shapes=[pltpu.VMEM((rows, HD), F32), pltpu.VMEM((rows, 1), F32)])
    return pl.pallas_call(
        functools.partial(_sb_decode_kernel, npg=npg, nt=nt, scale=(HD // nh) ** -0.5),
        grid_spec=grid_spec,
        out_shape=jax.ShapeDtypeStruct((B, nt, HD), F32),
        compiler_params=_params("arbitrary", "arbitrary"),
        name="sb_decode",
    )(page_table, qbd, knew, vnew, _tri(PAGE_SIZE), bd, *([cache_k] * npg), *([cache_v] * npg))


def _rwkv_prep_kernel(p_ref, prev_ref, mu_ref, w0_ref, w2_ref, a0_ref, a2_ref, g2_ref,
                      kkw_ref, ka_ref, ones_ref, r_o, lw_o, k_o, v_o, kk_o, b_o, g_o):
    H, _, N = r_o.shape
    Wd = H * N
    p = p_ref[...]
    ps = p + (prev_ref[...] - p) * mu_ref[...]
    r, k, v = ps[:, :Wd], ps[:, Wd:2 * Wd], ps[:, 2 * Wd:3 * Wd]
    o = 3 * Wd
    xw = ps[:, o:o + RW_DECAY_RANK]
    xa = ps[:, o + RW_DECAY_RANK:o + RW_DECAY_RANK + RW_A_RANK]
    xg = ps[:, o + RW_DECAY_RANK + RW_A_RANK:]
    w_log = -_softplus(-(w0_ref[...] + _dot(jnp.tanh(xw).astype(BF16), w2_ref[...]))) - 0.5
    lw = -jnp.exp(w_log)
    a = _sigmoid(a0_ref[...] + _dot(xa.astype(BF16), a2_ref[...]))
    g = _dot(_sigmoid(xg).astype(BF16), g2_ref[...])
    kk = k * kkw_ref[...]
    ss = _dot_exact_rhs(kk * kk, ones_ref[...])
    kk = kk / jnp.maximum(jnp.sqrt(ss), 1e-12)
    k2 = k * (1.0 + (a - 1.0) * ka_ref[...])
    b = kk * a
    for h in range(H):
        sl = slice(h * N, (h + 1) * N)
        r_o[h] = r[:, sl]
        lw_o[h] = lw[:, sl]
        k_o[h] = k2[:, sl]
        v_o[h] = v[:, sl]
        kk_o[h] = kk[:, sl]
        b_o[h] = b[:, sl]
        g_o[h] = g[:, sl]


def _rwkv_prep(p, prev, wts, tm):
    M, cols = p.shape
    H, N = RW_HEADS, RW_HEAD_DIM
    Wd = H * N
    row = lambda n: pl.BlockSpec((1, n), lambda i: (0, 0))
    mat = lambda a, b: pl.BlockSpec((a, b), lambda i: (0, 0))
    out = pl.BlockSpec((H, tm, N), lambda i: (0, i, 0))
    return pl.pallas_call(
        _rwkv_prep_kernel,
        grid=(M // tm,),
        in_specs=[pl.BlockSpec((tm, cols), lambda i: (i, 0)), pl.BlockSpec((tm, cols), lambda i: (i, 0)),
                  row(cols), row(Wd), mat(RW_DECAY_RANK, Wd), row(Wd), mat(RW_A_RANK, Wd),
                  mat(RW_GATE_RANK, Wd), row(Wd), row(Wd), mat(Wd, Wd)],
        out_specs=[out] * 7,
        out_shape=[jax.ShapeDtypeStruct((H, M, N), F32)] * 7,
        compiler_params=_params("arbitrary"),
        name="rwkv_prep",
    )(p, prev, wts['mu'], wts['w0'], wts['w2'], wts['a0'], wts['a2'], wts['g2'],
      wts['k_k'], wts['k_a'], wts['head_ones'])


def _dot3(dot, a, b):
    a0 = a.astype(BF16)
    a1 = (a - a0.astype(F32)).astype(BF16)
    b0 = b.astype(BF16)
    b1 = (b - b0.astype(F32)).astype(BF16)
    return dot(a0, b0) + (dot(a0, b1) + dot(a1, b0))


def _rwkv_chunk_kernel(r_ref, lw_ref, k_ref, v_ref, kk_ref, b_ref, g_ref, s0_ref,
                       lnw_ref, lnb_ref, rk_ref, y_ref, s_ref, *, C):
    c = pl.program_id(2)

    @pl.when(c == 0)
    def _():
        s_ref[...] = s0_ref[...]

    Hb = r_ref.shape[0]
    ti = lax.broadcasted_iota(jnp.int32, (C, C), 0)
    si = lax.broadcasted_iota(jnp.int32, (C, C), 1)
    strict, incl = si < ti, si <= ti
    tril = incl.astype(BF16)
    for h in range(Hb):
        r, lw, k, v, kk, b = r_ref[h], lw_ref[h], k_ref[h], v_ref[h], kk_ref[h], b_ref[h]
        l0, l1, l2 = _split3(lw)
        cum = _dot(tril, l0) + _dot(tril, l1) + _dot(tril, l2)
        g_in, g_inv = jnp.exp(cum), jnp.exp(-cum)
        qt = kk * jnp.exp(cum - lw)
        bt, kt, rt = b * g_inv, k * g_inv, r * g_in
        S0 = s_ref[h]
        Lb = jnp.where(strict, _dot3(_dot_nt, qt, bt), 0.0)
        Lk = jnp.where(strict, _dot3(_dot_nt, qt, kt), 0.0)
        Ab = jnp.where(incl, _dot3(_dot_nt, rt, bt), 0.0)
        Ak = jnp.where(incl, _dot3(_dot_nt, rt, kt), 0.0)
        U = -(_dot3(_dot_nt, qt, S0) + _dot3(_dot, Lk, v))
        for s in range(C - 1):
            U = U - Lb[:, s:s + 1] * U[s:s + 1, :]
        y = _dot3(_dot_nt, rt, S0) + _dot3(_dot, Ab, U) + _dot3(_dot, Ak, v)
        g_end = g_in[C - 1:C, :]
        s_ref[h] = (S0 + _dot3(_dot_tn, U, bt) + _dot3(_dot_tn, v, kt)) * g_end
        mean = jnp.mean(y, axis=-1, keepdims=True)
        var = jnp.mean(jnp.square(y - mean), axis=-1, keepdims=True)
        yn = (y - mean) * lax.rsqrt(var + RW_GN_EPS) * lnw_ref[h] + lnb_ref[h]
        bonus = jnp.sum(r * k * rk_ref[h], axis=-1, keepdims=True) * v
        y_ref[h] = (yn + bonus) * g_ref[h]


def _rwkv_chunks(feats, s0, ln_w, ln_b, r_k, B, T, C, Hb=8):
    H, M, N = feats[0].shape
    nc = T // C
    seq = pl.BlockSpec((Hb, C, N), lambda b, hb, c: (hb, b * nc + c, 0))
    st = pl.BlockSpec((None, Hb, N, N), lambda b, hb, c: (b, hb, 0, 0))
    par = pl.BlockSpec((Hb, 1, N), lambda b, hb, c: (hb, 0, 0))
    return pl.pallas_call(
        functools.partial(_rwkv_chunk_kernel, C=C),
        grid=(B, H // Hb, nc),
        in_specs=[seq] * 7 + [st, par, par, par],
        out_specs=[seq, st],
        out_shape=[jax.ShapeDtypeStruct((H, M, N), F32), jax.ShapeDtypeStruct((B, H, N, N), F32)],
        compiler_params=_params("arbitrary", "arbitrary", "arbitrary"),
        name="rwkv_chunks",
    )(*feats, s0, ln_w.reshape(H, 1, N), ln_b.reshape(H, 1, N), r_k.reshape(H, 1, N))


def _cd_weights(w):
    H, N = RW_HEADS, RW_HEAD_DIM
    Wd = H * N
    row = lambda a: a.reshape(1, -1)
    head = jnp.arange(Wd) // N
    return {
        'w_in': w['cd_w_in'].astype(BF16), 'w_out': w['cd_w_out'].astype(BF16),
        'mu': row(w['rwkv_mu']), 'w0': row(w['rwkv_w0']), 'w2': w['rwkv_w2'].astype(BF16),
        'a0': row(w['rwkv_a0']), 'a2': w['rwkv_a2'].astype(BF16), 'g2': w['rwkv_g2'].astype(BF16),
        'k_k': row(w['rwkv_k_k']), 'k_a': row(w['rwkv_k_a']),
        'head_ones': (head[:, None] == head[None, :]).astype(BF16),
        'ln_w': w['rwkv_ln_w'], 'ln_b': w['rwkv_ln_b'], 'r_k': w['rwkv_r_k'],
    }


def _ab_weights(w):
    wq, wuk = _mla_weights(w['mla_w_uq'], w['mla_w_uk'])
    return {
        'w_in': _ab_in_weights(w['ab_w_in'], w['s5_d'].size), 'w_out': w['ab_w_out'].astype(BF16),
        's5_consts': _s5_consts(w['s5_lambda_re'], w['s5_lambda_im'], w['s5_log_dt'],
                                w['s5_b_re'], w['s5_b_im'], w['s5_c_re'], w['s5_c_im']),
        's5_d': w['s5_d'], 's5_w_glu': w['s5_w_glu'], 's5_b_glu': w['s5_b_glu'],
        'mla_g_q': w['mla_g_q'], 'mla_g_kv': w['mla_g_kv'], 'mla_wq': wq, 'mla_wuk': wuk,
        'mla_wuv': jnp.transpose(w['mla_w_uv'], (1, 0, 2)).astype(BF16),
    }


def _cd_layer(x, mods, g_norm, B, T, tm, cache, wts, tq=256):
    sh1, sc1, g1 = mods
    M = B * T
    H, d = SB_HEADS, SB_HEAD_DIM
    Wsb = H * d
    cols = wts['mu'].shape[1]
    q, k, v, rw = _norm_mod_mm(x, g_norm, sc1, sh1, wts['w_in'], T, tm, (Wsb, Wsb, Wsb, cols))
    if cache is None:
        heads = lambda a: jnp.transpose(a.reshape(B, T, H, d), (0, 2, 1, 3)).astype(BF16)
        sb = _sb_attn(heads(q), heads(k), heads(v), tq)
        sb = jnp.transpose(sb, (0, 2, 1, 3)).reshape(M, Wsb)
    else:
        eye = jnp.eye(H, dtype=F32)
        qbd = jnp.einsum('bthd,hg->bthgd', q.reshape(B, T, H, d), eye).reshape(B, T * H, Wsb)
        padded = lambda a: jnp.pad(a.reshape(B, T, Wsb), ((0, 0), (0, PAGE_SIZE - T), (0, 0)))
        n_phys = cache['sb_k'].shape[1]
        flat = lambda a: a.reshape(a.shape[0], n_phys, PAGE_SIZE, Wsb)
        sb = _sb_decode(qbd.astype(BF16), padded(k).astype(BF16), padded(v).astype(BF16),
                        flat(cache['sb_k']), flat(cache['sb_v']), cache['page_table'],
                        cache['layer'], T).reshape(M, Wsb)
    rw3 = rw.reshape(B, T, cols)
    shift0 = jnp.zeros((B, cols), F32) if cache is None else cache['shift']
    prev = jnp.concatenate([shift0[:, None, :], rw3[:, :-1]], axis=1).reshape(M, cols)
    feats = _rwkv_prep(rw, prev, wts, tm)
    N = RW_HEAD_DIM
    C = min(T, 64)
    if T < 8:
        C = 8
        padt = lambda a: jnp.pad(a.reshape(RW_HEADS, B, T, N),
                                 ((0, 0), (0, 0), (0, C - T), (0, 0))).reshape(RW_HEADS, B * C, N)
        feats = [padt(f) for f in feats]
    s0 = (jnp.zeros((B, RW_HEADS, N, N), F32) if cache is None else cache['wkv'])
    Tp = max(T, C)
    y, wkv = _rwkv_chunks(feats, s0, wts['ln_w'], wts['ln_b'], wts['r_k'], B, Tp, C)
    y = y.reshape(RW_HEADS, B, Tp, N)[:, :, :T]
    rw_out = jnp.transpose(y, (1, 2, 0, 3)).reshape(M, RW_HEADS * N)
    mix = jnp.concatenate([sb, rw_out], axis=1)
    x = _mm_resid(mix, wts['w_out'], x, g1, T, tm)
    return x, k, v, wkv, rw3[:, -1]


LANES = 128


def _top_values(s, n):
    out = []
    for _ in range(n):
        m = jnp.max(s, axis=0, keepdims=True)
        out.append(m)
        s = jnp.where(s == m, NEG_BIG, s)
    return out


def _peer_route_kernel(x_ref, g_ref, sc_ref, sh_ref, wq_ref, k1_ref, k2_ref,
                       h_ref, e1_ref, j_ref, r2_ref, e2_ref, s1_s, s2_s):
    H, NK, tm = s1_s.shape
    half = k1_ref.shape[2]
    K = PEER_TOPK
    h = (_rms(x_ref[...]) * g_ref[...] * (1.0 + sc_ref[...]) + sh_ref[...]).astype(BF16)
    h_ref[...] = h
    q = _dot(h, wq_ref[...]).astype(BF16)
    for hd in range(H):
        base = hd * 2 * half
        s1_s[hd] = _dot_nt(k1_ref[hd], q[:, base:base + half])
        s2_s[hd] = _dot_nt(k2_ref[hd], q[:, base + half:base + 2 * half])
    row16 = lax.broadcasted_iota(jnp.int32, (K, LANES), 0)
    row8 = lax.broadcasted_iota(jnp.int32, (8, LANES), 0)

    def lane_group(c, carry):
        lanes = pl.ds(pl.multiple_of(c * LANES, LANES), LANES)
        for hd in range(H):
            s1 = s1_s[hd, :, lanes]
            s2 = s2_s[hd, :, lanes]
            v1 = _top_values(s1, K)
            v2 = _top_values(s2, K)
            v2t = jnp.zeros((K, LANES), F32)
            for i in range(K):
                v2t = jnp.where(row16 == i, v2[i], v2t)
            cands = [v1[0] + v2t]
            for i in range(1, K):
                cands.append(jnp.where(row8 < K // (i + 1), v1[i] + v2t[:8], NEG_BIG))
            best = []
            for _ in range(K):
                m = jnp.max(cands[0], axis=0, keepdims=True)
                for cnd in cands[1:]:
                    m = jnp.maximum(m, jnp.max(cnd, axis=0, keepdims=True))
                best.append(m)
                cands = [jnp.where(cnd == m, NEG_BIG, cnd) for cnd in cands]
            tau = best[K - 1]
            z = jnp.ones((1, LANES), F32)
            for n in range(1, K):
                z = z + jnp.exp(best[n] - best[0])
            cnt = jnp.zeros((NK, LANES), F32)
            rank = jnp.zeros((NK, LANES), F32)
            for i in range(K):
                cnt = cnt + jnp.where(s1 + v2[i] >= tau, 1.0, 0.0)
                rank = rank + jnp.where(v2[i] > s2, 1.0, 0.0)
            e1_ref[hd, :, lanes] = jnp.exp(s1 - v1[0]) / z
            j_ref[hd, :, lanes] = cnt
            r2_ref[hd, :, lanes] = rank
            e2_ref[hd, :, lanes] = jnp.exp(s2 - v2[0])
        return carry

    lax.fori_loop(0, tm // LANES, lane_group, 0)


def _peer_route(x, g, sc, sh, wq, k1, k2, T, tm):
    M, D = x.shape
    H, NK, half = k1.shape
    tab = pl.BlockSpec((H, NK, tm), lambda i: (0, 0, i))
    return pl.pallas_call(
        _peer_route_kernel,
        grid=(M // tm,),
        in_specs=[pl.BlockSpec((tm, D), lambda i: (i, 0)),
                  pl.BlockSpec((1, D), lambda i: (0, 0)),
                  _mod_spec(T, tm, D), _mod_spec(T, tm, D),
                  pl.BlockSpec((D, wq.shape[1]), lambda i: (0, 0)),
                  pl.BlockSpec((H, NK, half), lambda i: (0, 0, 0)),
                  pl.BlockSpec((H, NK, half), lambda i: (0, 0, 0))],
        out_specs=[pl.BlockSpec((tm, D), lambda i: (i, 0)), tab, tab, tab, tab],
        out_shape=[jax.ShapeDtypeStruct((M, D), BF16)] + [jax.ShapeDtypeStruct((H, NK, M), F32)] * 4,
        scratch_shapes=[pltpu.VMEM((H, NK, tm), F32), pltpu.VMEM((H, NK, tm), F32)],
        compiler_params=_params("arbitrary"),
        name="peer_route",
    )(x, g.reshape(1, D), _mod_rows(sc, T, tm), _mod_rows(sh, T, tm), wq, k1, k2)


def _peer_expert_kernel(h_ref, u_ref, vt_ref, e1_ref, j_ref, r2_ref, e2_ref, x_ref, gate_ref,
                        o_ref, acc_s, act_s, g_s, *, A):
    j = pl.program_id(1)
    H, NK, tm = r2_ref.shape

    @pl.when(j == 0)
    def _():
        acc_s[...] = jnp.zeros(acc_s.shape, F32)

    act_s[...] = _dot_nt(u_ref[...], h_ref[...])
    for al in range(A):
        rows = slice(al * NK, (al + 1) * NK)
        for c in range(tm // LANES):
            lanes = slice(c * LANES, (c + 1) * LANES)
            w = jnp.zeros((NK, LANES), F32)
            for hd in range(H):
                e1 = e1_ref[hd, al:al + 1, lanes]
                cnt = j_ref[hd, al:al + 1, lanes]
                w = w + jnp.where(r2_ref[hd, :, lanes] < cnt, e2_ref[hd, :, lanes], 0.0) * e1
            g_s[rows, lanes] = (w * _gelu(act_s[rows, lanes])).astype(BF16)
    acc_s[...] += _dot(vt_ref[...], g_s[...])

    @pl.when(j == pl.num_programs(1) - 1)
    def _():
        o_ref[...] = x_ref[...] + gate_ref[...] * acc_s[...].T


def _peer_experts(h, u, vt, tabs, x, gate, T, tm, te=1024):
    M, D = x.shape
    E = u.shape[0]
    H, NK, _ = tabs[0].shape
    A = te // NK
    tab = pl.BlockSpec((H, NK, tm), lambda i, j: (0, 0, i))
    tab_a = pl.BlockSpec((H, A, tm), lambda i, j: (0, j, i))
    return pl.pallas_call(
        functools.partial(_peer_expert_kernel, A=A),
        grid=(M // tm, E // te),
        in_specs=[pl.BlockSpec((tm, D), lambda i, j: (i, 0)),
                  pl.BlockSpec((te, D), lambda i, j: (j, 0)),
                  pl.BlockSpec((D, te), lambda i, j: (0, j)),
                  tab_a, tab_a, tab, tab,
                  pl.BlockSpec((tm, D), lambda i, j: (i, 0)),
                  _mod_spec2(T, tm, D)],
        out_specs=pl.BlockSpec((tm, D), lambda i, j: (i, 0)),
        out_shape=jax.ShapeDtypeStruct((M, D), F32),
        scratch_shapes=[pltpu.VMEM((D, tm), F32), pltpu.VMEM((te, tm), F32),
                        pltpu.VMEM((te, tm), BF16)],
        compiler_params=_params("arbitrary", "arbitrary"),
        name="peer_experts",
    )(h, u, vt, *tabs, x, _mod_rows(gate, T, tm))


def _mod_spec2(T, tm, D):
    if tm <= T:
        return pl.BlockSpec((None, 1, D), lambda i, j: (i * tm // T, 0, 0))
    return pl.BlockSpec((None, tm, D), lambda i, j: (0, i, 0))


ROW_TILE = 512


def _trunk(x, c, w, ab_wts, cd_wts, peer_wts, cache):
    B, T, D = x.shape
    M = B * T
    tm = min(ROW_TILE, M)
    depth = w['ada_w'].shape[0]
    past_len = 0 if cache is None else cache['page_table'].shape[1] * PAGE_SIZE
    pos = past_len + jnp.arange(T, dtype=jnp.int32)
    mod = _ada(c, w['ada_w'], w['ada_b'])
    xf = x.reshape(M, D)
    ckv_rows, kr_rows, s5r, s5i, k_rows, v_rows, wkvs, shifts = [], [], [], [], [], [], [], []
    for l in range(depth):
        i = l // 2
        sh1, sc1, g1, sh2, sc2, g2 = jnp.split(mod[l], 6, axis=-1)
        if l % 2 == 0:
            lc = None if cache is None else dict(
                ckv=cache['mla_ckv'], kr=cache['mla_krope'], page_table=cache['page_table'], layer=i,
                s5_re=cache['s5_re'][i], s5_im=cache['s5_im'][i])
            xf, ckv, kr, sr, si = _ab_layer(xf, (sh1, sc1, g1), w['norm1_g'][l], pos, B, T, tm, lc,
                                            ab_wts[i])
            ckv_rows.append(ckv.reshape(B, T, -1))
            kr_rows.append(kr.reshape(B, T, -1))
            state_shape = (B,) + w['s5_lambda_re'].shape[1:]
            s5r.append(sr.reshape(state_shape))
            s5i.append(si.reshape(state_shape))
        else:
            lc = None if cache is None else dict(
                sb_k=cache['sb_k'], sb_v=cache['sb_v'], page_table=cache['page_table'], layer=i,
                wkv=cache['wkv'][i], shift=cache['shift'][i])
            xf, k, v, wkv, shift = _cd_layer(xf, (sh1, sc1, g1), w['norm1_g'][l], B, T, tm, lc,
                                             cd_wts[i])
            k_rows.append(k.reshape(B, T, SB_HEADS, SB_HEAD_DIM))
            v_rows.append(v.reshape(B, T, SB_HEADS, SB_HEAD_DIM))
            wkvs.append(wkv)
            shifts.append(shift)
        pw = peer_wts[l]
        h2, *tabs = _peer_route(xf, w['norm2_g'][l], sc2, sh2, pw['wq'], pw['k1'], pw['k2'], T, tm)
        xf = _peer_experts(h2, pw['u'], pw['vt'], tabs, xf, g2, T, tm)
    y = _rmsnorm(xf, w['final_g'], tm).reshape(B, T, D)
    return (y, jnp.stack(ckv_rows), jnp.stack(kr_rows), jnp.stack(k_rows), jnp.stack(v_rows),
            jnp.stack(s5r), jnp.stack(s5i), jnp.stack(wkvs), jnp.stack(shifts))


def kernel(x_prompt, x_sample, c_prompt, c_sample, cache_mla_ckv, cache_mla_krope, cache_sb_k,
           cache_sb_v, page_table, state_s5_re, state_s5_im, state_rwkv_wkv, state_rwkv_shift,
           ada_w, ada_b, norm1_g, norm2_g, final_g, ab_w_in, ab_w_out, s5_lambda_re, s5_lambda_im,
           s5_log_dt, s5_b_re, s5_b_im, s5_c_re, s5_c_im, s5_d, s5_w_glu, s5_b_glu, mla_g_q,
           mla_w_uq, mla_g_kv, mla_w_uk, mla_w_uv, cd_w_in, cd_w_out, rwkv_mu, rwkv_w0, rwkv_w2,
           rwkv_a0, rwkv_a2, rwkv_g2, rwkv_k_k, rwkv_k_a, rwkv_r_k, rwkv_ln_w, rwkv_ln_b,
           peer_w_q, peer_k1, peer_k2, peer_u, peer_v):
    w = dict(ada_w=ada_w, ada_b=ada_b, norm1_g=norm1_g, norm2_g=norm2_g, final_g=final_g,
             s5_lambda_re=s5_lambda_re)
    ab = dict(ab_w_in=ab_w_in, ab_w_out=ab_w_out, s5_lambda_re=s5_lambda_re,
              s5_lambda_im=s5_lambda_im, s5_log_dt=s5_log_dt, s5_b_re=s5_b_re, s5_b_im=s5_b_im,
              s5_c_re=s5_c_re, s5_c_im=s5_c_im, s5_d=s5_d, s5_w_glu=s5_w_glu, s5_b_glu=s5_b_glu,
              mla_g_q=mla_g_q, mla_w_uq=mla_w_uq, mla_g_kv=mla_g_kv, mla_w_uk=mla_w_uk,
              mla_w_uv=mla_w_uv)
    cd = dict(cd_w_in=cd_w_in, cd_w_out=cd_w_out, rwkv_mu=rwkv_mu, rwkv_w0=rwkv_w0,
              rwkv_w2=rwkv_w2, rwkv_a0=rwkv_a0, rwkv_a2=rwkv_a2, rwkv_g2=rwkv_g2,
              rwkv_k_k=rwkv_k_k, rwkv_k_a=rwkv_k_a, rwkv_r_k=rwkv_r_k, rwkv_ln_w=rwkv_ln_w,
              rwkv_ln_b=rwkv_ln_b)
    depth = ada_w.shape[0]
    ab_wts = [_ab_weights({k: v[i] for k, v in ab.items()}) for i in range((depth + 1) // 2)]
    cd_wts = [_cd_weights({k: v[i] for k, v in cd.items()}) for i in range(depth // 2)]
    peer_wts = [dict(wq=peer_w_q[l].astype(BF16), k1=peer_k1[l].astype(BF16),
                     k2=peer_k2[l].astype(BF16), u=peer_u[l].astype(BF16),
                     vt=peer_v[l].T.astype(BF16)) for l in range(depth)]
    cache = dict(mla_ckv=cache_mla_ckv, mla_krope=cache_mla_krope, sb_k=cache_sb_k, sb_v=cache_sb_v,
                 page_table=page_table, s5_re=state_s5_re, s5_im=state_s5_im, wkv=state_rwkv_wkv,
                 shift=state_rwkv_shift)
    p = _trunk(x_prompt, c_prompt, w, ab_wts, cd_wts, peer_wts, None)
    s = _trunk(x_sample, c_sample, w, ab_wts, cd_wts, peer_wts, cache)
    return (p[0], s[0]) + p[1:] + s[1:]
```

```python
import functools
import math

import jax
import jax.numpy as jnp
from jax import lax
from jax.experimental import pallas as pl
from jax.experimental.pallas import tpu as pltpu

F32 = jnp.float32
BF16 = jnp.bfloat16

NORM_EPS = 1e-6
ROPE_BASE = 10000.0
PAGE_SIZE = 128
S5_GROUP = 16
S5_STATE = 64
MLA_HEADS = 8
MLA_NOPE = 64
MLA_ROPE = 32
MLA_V = 64
MLA_KV_RANK = 256
MLA_KPAD = 384
SB_HEADS = 8
SB_HEAD_DIM = 64
RW_HEADS = 8
RW_HEAD_DIM = 64
RW_DECAY_RANK = 64
RW_A_RANK = 64
RW_GATE_RANK = 128
RW_GN_EPS = 64e-5
PEER_HEADS = 8
PEER_N_KEYS = 128
PEER_TOPK = 16
NEG_BIG = -3.0e38

VMEM_LIMIT = 48 * 1024 * 1024


def _params(*sem):
    return pltpu.CompilerParams(dimension_semantics=sem, vmem_limit_bytes=VMEM_LIMIT)


def _dot(a, b):
    return jnp.dot(a, b, preferred_element_type=F32)


def _dot_nt(a, b):
    return lax.dot_general(a, b, (((1,), (1,)), ((), ())), preferred_element_type=F32)


def _dot_tn(a, b):
    return lax.dot_general(a, b, (((0,), (0,)), ((), ())), preferred_element_type=F32)


def _split3(a):
    hi = a.astype(BF16)
    r1 = a - hi.astype(F32)
    mid = r1.astype(BF16)
    lo = (r1 - mid.astype(F32)).astype(BF16)
    return hi, mid, lo


def _dotx(dot, a, b):
    a0, a1, a2 = _split3(a)
    b0, b1, b2 = _split3(b)
    return (dot(a0, b0) + (dot(a0, b1) + dot(a1, b0))
            + (dot(a1, b1) + dot(a0, b2) + dot(a2, b0)))


def _dot_exact_rhs(a, b_bf16):
    a0, a1, a2 = _split3(a)
    return _dot(a0, b_bf16) + _dot(a1, b_bf16) + _dot(a2, b_bf16)


def _gelu(x):
    c = math.sqrt(2.0 / math.pi)
    return 0.5 * x * (1.0 + jnp.tanh(c * (x + 0.044715 * (x * x * x))))


def _sigmoid(x):
    return 1.0 / (1.0 + jnp.exp(-x))


def _softplus(x):
    return jnp.maximum(x, 0.0) + jnp.log1p(jnp.exp(-jnp.abs(x)))


def _rms(x):
    return x * lax.rsqrt(jnp.mean(x * x, axis=-1, keepdims=True) + NORM_EPS)


def _ada_kernel(c_ref, w_ref, b_ref, o_ref):
    c = c_ref[...]
    s = (c * _sigmoid(c)).astype(BF16)
    o_ref[...] = _dot(s, w_ref[...].astype(BF16)) + b_ref[...]


def _ada(c, ada_w, ada_b):
    L, D, N = ada_w.shape
    Bc = c.shape[0]
    tn = 1024
    return pl.pallas_call(
        _ada_kernel,
        grid=(L, N // tn),
        in_specs=[pl.BlockSpec((Bc, D), lambda l, j: (0, 0)),
                  pl.BlockSpec((None, D, tn), lambda l, j: (l, 0, j)),
                  pl.BlockSpec((None, 1, tn), lambda l, j: (l, 0, j))],
        out_specs=pl.BlockSpec((None, Bc, tn), lambda l, j: (l, 0, j)),
        out_shape=jax.ShapeDtypeStruct((L, Bc, N), F32),
        compiler_params=_params("arbitrary", "arbitrary"),
        name="ada",
    )(c, ada_w, ada_b.reshape(L, 1, N))


def _mod_spec(T, tm, D):
    if tm <= T:
        return pl.BlockSpec((None, 1, D), lambda i: (i * tm // T, 0, 0))
    return pl.BlockSpec((None, tm, D), lambda i: (0, i, 0))


def _mod_rows(m, T, tm):
    if tm <= T:
        return m[:, None, :]
    return jnp.repeat(m, T, axis=0)[None]


def _norm_mod_mm_kernel(x_ref, g_ref, sc_ref, sh_ref, w_ref, *o_refs):
    h = _rms(x_ref[...]) * g_ref[...] * (1.0 + sc_ref[...]) + sh_ref[...]
    res = _dot(h.astype(BF16), w_ref[...])
    off = 0
    for o_ref in o_refs:
        n = o_ref.shape[1]
        o_ref[...] = res[:, off:off + n]
        off += n


def _norm_mod_mm(x, g, sc, sh, w, T, tm, widths):
    M, D = x.shape
    N = w.shape[1]
    assert sum(widths) == N
    return pl.pallas_call(
        _norm_mod_mm_kernel,
        grid=(M // tm,),
        in_specs=[pl.BlockSpec((tm, D), lambda i: (i, 0)),
                  pl.BlockSpec((1, D), lambda i: (0, 0)),
                  _mod_spec(T, tm, D), _mod_spec(T, tm, D),
                  pl.BlockSpec((D, N), lambda i: (0, 0))],
        out_specs=[pl.BlockSpec((tm, n), lambda i: (i, 0)) for n in widths],
        out_shape=[jax.ShapeDtypeStruct((M, n), F32) for n in widths],
        compiler_params=_params("arbitrary"),
        name="norm_mod_mm",
    )(x, g.reshape(1, D), _mod_rows(sc, T, tm), _mod_rows(sh, T, tm), w)


def _mm_resid_kernel(a_ref, w_ref, x_ref, gate_ref, o_ref):
    o_ref[...] = x_ref[...] + gate_ref[...] * _dot(a_ref[...].astype(BF16), w_ref[...])


def _mm_resid(a, w, x, gate, T, tm):
    M, K = a.shape
    D = w.shape[1]
    return pl.pallas_call(
        _mm_resid_kernel,
        grid=(M // tm,),
        in_specs=[pl.BlockSpec((tm, K), lambda i: (i, 0)),
                  pl.BlockSpec((K, D), lambda i: (0, 0)),
                  pl.BlockSpec((tm, D), lambda i: (i, 0)),
                  _mod_spec(T, tm, D)],
        out_specs=pl.BlockSpec((tm, D), lambda i: (i, 0)),
        out_shape=jax.ShapeDtypeStruct((M, D), F32),
        compiler_params=_params("arbitrary"),
        name="mm_resid",
    )(a, w, x, _mod_rows(gate, T, tm))


def _rmsnorm_kernel(x_ref, g_ref, o_ref):
    o_ref[...] = _rms(x_ref[...]) * g_ref[...]


def _rmsnorm(x, g, tm):
    M, D = x.shape
    return pl.pallas_call(
        _rmsnorm_kernel,
        grid=(M // tm,),
        in_specs=[pl.BlockSpec((tm, D), lambda i: (i, 0)),
                  pl.BlockSpec((1, D), lambda i: (0, 0))],
        out_specs=pl.BlockSpec((tm, D), lambda i: (i, 0)),
        out_shape=jax.ShapeDtypeStruct((M, D), F32),
        compiler_params=_params("arbitrary"),
        name="final_norm",
    )(x, g.reshape(1, D))


def _s5_kernel(u_ref, s0r_ref, s0i_ref, abr_ref, abi_ref, cor_ref, coi_ref,
               wbr_ref, wbi_ref, wcr_ref, wci_ref, d_ref, wg_ref, bg_ref,
               o_ref, sr_ref, si_ref, xr_buf, xi_buf, *, nb, tc):
    c = pl.program_id(0)

    @pl.when(c == 0)
    def _():
        sr_ref[...] = s0r_ref[...]
        si_ref[...] = s0i_ref[...]

    u = u_ref[...]
    ub = u.astype(BF16)
    bu_r = _dot(ub, wbr_ref[...])
    bu_i = _dot(ub, wbi_ref[...])
    cor, coi = cor_ref[...], coi_ref[...]
    xr_buf[...] = cor * bu_r - coi * bu_i
    xi_buf[...] = cor * bu_i + coi * bu_r
    abr, abi = abr_ref[...], abi_ref[...]

    def step(t, carry):
        xr, xi = carry
        rows = pl.ds(pl.multiple_of(t * nb, nb), nb)
        nr = abr * xr - abi * xi + xr_buf[rows, :]
        ni = abr * xi + abi * xr + xi_buf[rows, :]
        xr_buf[rows, :] = nr
        xi_buf[rows, :] = ni
        return nr, ni

    xr, xi = lax.fori_loop(0, tc, step, (sr_ref[...], si_ref[...]))
    sr_ref[...] = xr
    si_ref[...] = xi

    y = (_dot(xr_buf[...].astype(BF16), wcr_ref[...])
         - _dot(xi_buf[...].astype(BF16), wci_ref[...]) + d_ref[...] * u)
    z = _gelu(y)
    o_ref[...] = z * _sigmoid(_dot(z.astype(BF16), wg_ref[...]) + bg_ref[...])


def _s5_consts(lam_re, lam_im, log_dt, b_re, b_im, c_re, c_im):
    G, P = lam_re.shape
    Hc = b_re.shape[-1]
    dt = jnp.exp(log_dt)[:, None]
    mag = jnp.exp(lam_re * dt)
    ab_re, ab_im = mag * jnp.cos(lam_im * dt), mag * jnp.sin(lam_im * dt)
    den = lam_re * lam_re + lam_im * lam_im
    nr = ab_re - 1.0
    co_re = (nr * lam_re + ab_im * lam_im) / den
    co_im = (ab_im * lam_re - nr * lam_im) / den
    eye = jnp.eye(G, dtype=F32)
    wb = lambda b: jnp.einsum('gph,gk->ghkp', b, eye).reshape(G * Hc, G * P).astype(BF16)
    wc = lambda cc: jnp.einsum('ghp,gk->gpkh', cc, eye).reshape(G * P, G * Hc).astype(BF16)
    flat = lambda a: a.reshape(1, G * P)
    return (flat(ab_re), flat(ab_im), flat(co_re), flat(co_im),
            wb(b_re), wb(b_im), wc(c_re), wc(c_im))


def _s5(u_tb, s0_re, s0_im, consts, d, w_glu, b_glu, nb, tc):
    M, W = u_tb.shape
    T = M // nb
    GP = s0_re.shape[1]
    ab_re, ab_im, co_re, co_im, wbr, wbi, wcr, wci = consts
    rows = tc * nb
    full = lambda shape: pl.BlockSpec(shape, lambda c: (0,) * len(shape))
    return pl.pallas_call(
        functools.partial(_s5_kernel, nb=nb, tc=tc),
        grid=(T // tc,),
        in_specs=[pl.BlockSpec((rows, W), lambda c: (c, 0)),
                  full((nb, GP)), full((nb, GP)),
                  full((1, GP)), full((1, GP)), full((1, GP)), full((1, GP)),
                  full((W, GP)), full((W, GP)), full((GP, W)), full((GP, W)),
                  full((1, W)), full((W, W)), full((1, W))],
        out_specs=[pl.BlockSpec((rows, W), lambda c: (c, 0)), full((nb, GP)), full((nb, GP))],
        out_shape=[jax.ShapeDtypeStruct((M, W), F32),
                   jax.ShapeDtypeStruct((nb, GP), F32),
                   jax.ShapeDtypeStruct((nb, GP), F32)],
        scratch_shapes=[pltpu.VMEM((rows, GP), F32), pltpu.VMEM((rows, GP), F32)],
        compiler_params=_params("arbitrary"),
        name="s5",
    )(u_tb, s0_re, s0_im, ab_re, ab_im, co_re, co_im, wbr, wbi, wcr, wci,
      d.reshape(1, W), w_glu.astype(BF16), b_glu.reshape(1, W))


def _rope_tables(pos, width):
    half = MLA_ROPE // 2
    inv = ROPE_BASE ** (-jnp.arange(half, dtype=F32) / half)
    ang = pos.astype(F32)[:, None] * inv[None, :]
    cos, sin = jnp.cos(ang), jnp.sin(ang)
    pad = jnp.zeros((pos.shape[0], width - MLA_ROPE), F32)
    return (jnp.concatenate([cos, cos, pad], axis=1), jnp.concatenate([-sin, sin, pad], axis=1))


def _swap_halves(w):
    half = w.shape[-1] // 2
    return jnp.concatenate([w[..., half:], w[..., :half]], axis=-1)


def _pad_cols(w, width):
    return jnp.pad(w, [(0, 0)] * (w.ndim - 1) + [(0, width - w.shape[-1])])


def _mla_prep_kernel(cq_ref, ckv_ref, kr_ref, krs_ref, c2_ref, s2_ref, gq_ref, gkv_ref,
                     wq_ref, wuk_ref, q_ref, kcat_ref, ckvn_ref, krot_ref):
    H = q_ref.shape[0]
    c2, s2 = c2_ref[...], s2_ref[...]
    cqn = (_rms(cq_ref[...]) * gq_ref[...]).astype(BF16)
    qall = _dot(cqn, wq_ref[...])
    for h in range(H):
        base = h * 384
        nope = qall[:, base:base + 128].astype(BF16)
        rot = qall[:, base + 128:base + 256] * c2 + qall[:, base + 256:base + 384] * s2
        q_ref[h, :, 0:MLA_KV_RANK] = _dot(nope, wuk_ref[h]).astype(BF16)
        q_ref[h, :, MLA_KV_RANK:MLA_KPAD] = rot.astype(BF16)
    ckvn = _rms(ckv_ref[...]) * gkv_ref[...]
    krot = kr_ref[...] * c2 + krs_ref[...] * s2
    ckvn_ref[...] = ckvn
    krot_ref[...] = krot[:, :MLA_ROPE]
    kcat_ref[:, 0:MLA_KV_RANK] = ckvn.astype(BF16)
    kcat_ref[:, MLA_KV_RANK:MLA_KPAD] = krot.astype(BF16)


def _mla_prep(cq, ckv, kr, krs, c2, s2, g_q, g_kv, wq, wuk, tm):
    M = cq.shape[0]
    H = MLA_HEADS
    col = lambda w: pl.BlockSpec((tm, w), lambda i: (i, 0))
    full = lambda shape: pl.BlockSpec(shape, lambda i: (0,) * len(shape))
    return pl.pallas_call(
        _mla_prep_kernel,
        grid=(M // tm,),
        in_specs=[col(384), col(256), col(128), col(128), col(128), col(128),
                  full((1, 384)), full((1, 256)), full((384, H * 384)), full((H, 128, 256))],
        out_specs=[pl.BlockSpec((H, tm, MLA_KPAD), lambda i: (0, i, 0)),
                   col(MLA_KPAD), col(MLA_KV_RANK), col(MLA_ROPE)],
        out_shape=[jax.ShapeDtypeStruct((H, M, MLA_KPAD), BF16),
                   jax.ShapeDtypeStruct((M, MLA_KPAD), BF16),
                   jax.ShapeDtypeStruct((M, MLA_KV_RANK), F32),
                   jax.ShapeDtypeStruct((M, MLA_ROPE), F32)],
        compiler_params=_params("arbitrary"),
        name="mla_prep",
    )(cq, ckv, kr, krs, c2, s2, g_q.reshape(1, -1), g_kv.reshape(1, -1), wq, wuk)


def _mla_weights(w_uq, w_uk):
    C, H, _ = w_uq.shape
    nope = _pad_cols(w_uq[..., :MLA_NOPE], 128)
    rope = w_uq[..., MLA_NOPE:]
    wq = jnp.concatenate([nope, _pad_cols(rope, 128), _pad_cols(_swap_halves(rope), 128)], axis=-1)
    wuk = jnp.pad(jnp.transpose(w_uk, (1, 2, 0)), ((0, 0), (0, 128 - MLA_NOPE), (0, 0)))
    return wq.reshape(C, H * 384).astype(BF16), wuk.astype(BF16)


def _mla_attn_kernel(q_ref, k_ref, wuv_ref, o_ref, m_s, l_s, acc_s, *, tq, tk, scale):
    qi = pl.program_id(1)
    H = q_ref.shape[0]
    rows = H * tq
    q = q_ref[...].reshape(rows, MLA_KPAD)
    m_s[...] = jnp.full((rows, 1), NEG_BIG, F32)
    l_s[...] = jnp.zeros((rows, 1), F32)
    acc_s[...] = jnp.zeros((rows, MLA_KV_RANK), F32)
    row_t = qi * tq + lax.broadcasted_iota(jnp.int32, (H, tq, 1), 1).reshape(rows, 1)
    groups = [slice(g * rows // 2, (g + 1) * rows // 2) for g in range(2)]

    def block(kb, masked):
        k = k_ref[pl.ds(pl.multiple_of(kb * tk, tk), tk), :]
        vals = k[:, :MLA_KV_RANK]
        s = [_dot_nt(q[g], k) * scale for g in groups]
        if masked:
            kpos = kb * tk + lax.broadcasted_iota(jnp.int32, (1, tk), 1)
            s = [jnp.where(kpos <= row_t[g], sg, NEG_BIG) for g, sg in zip(groups, s)]
        m_prev = [m_s[g] for g in groups]
        m_new = [jnp.maximum(mp, jnp.max(sg, axis=-1, keepdims=True)) for mp, sg in zip(m_prev, s)]
        alpha = [jnp.exp(mp - mn) for mp, mn in zip(m_prev, m_new)]
        p = [jnp.exp(sg - mn) for sg, mn in zip(s, m_new)]
        pv = [_dot(pg.astype(BF16), vals) for pg in p]
        for g, al, pg, pvg, mn in zip(groups, alpha, p, pv, m_new):
            l_s[g] = al * l_s[g] + jnp.sum(pg, axis=-1, keepdims=True)
            acc_s[g] = al * acc_s[g] + pvg
            m_s[g] = mn

    n_full = (qi * tq) // tk

    def body(kb, carry):
        block(kb, False)
        return carry

    lax.fori_loop(0, n_full, body, 0)
    block(n_full, True)
    o = (acc_s[...] / l_s[...]).astype(BF16)
    for h in range(H):
        o_ref[:, h * MLA_V:(h + 1) * MLA_V] = _dot(o[h * tq:(h + 1) * tq], wuv_ref[h])


def _mla_attn(q, kcat, wuv, B, T, tq, tk):
    H = q.shape[0]
    nq = T // tq
    scale = (MLA_NOPE + MLA_ROPE) ** -0.5
    return pl.pallas_call(
        functools.partial(_mla_attn_kernel, tq=tq, tk=tk, scale=scale),
        grid=(B, nq),
        in_specs=[pl.BlockSpec((H, tq, MLA_KPAD), lambda b, i: (0, b * nq + i, 0)),
                  pl.BlockSpec((T, MLA_KPAD), lambda b, i: (b, 0)),
                  pl.BlockSpec((H, MLA_KV_RANK, MLA_V), lambda b, i: (0, 0, 0))],
        out_specs=pl.BlockSpec((tq, H * MLA_V), lambda b, i: (b * nq + i, 0)),
        out_shape=jax.ShapeDtypeStruct((B * T, H * MLA_V), F32),
        scratch_shapes=[pltpu.VMEM((H * tq, 1), F32), pltpu.VMEM((H * tq, 1), F32),
                        pltpu.VMEM((H * tq, MLA_KV_RANK), F32)],
        compiler_params=_params("arbitrary", "arbitrary"),
        name="mla_attn",
    )(q, kcat, wuv)


def _mla_decode_kernel(pt_ref, q_ref, knew_ref, *refs, npg, nt, scale):
    ckv_refs = refs[:npg]
    kr_refs = refs[npg:2 * npg]
    o_ref, m_s, l_s, acc_s, kbuf, rbuf = refs[2 * npg:]
    j = pl.program_id(1)
    q = q_ref[...]
    rows = q.shape[0]

    def update(s, vals):
        m_prev = m_s[...]
        m_new = jnp.maximum(m_prev, jnp.max(s, axis=-1, keepdims=True))
        alpha = jnp.exp(m_prev - m_new)
        p = jnp.exp(s - m_new)
        l_s[...] = alpha * l_s[...] + jnp.sum(p, axis=-1, keepdims=True)
        acc_s[...] = alpha * acc_s[...] + _dot(p.astype(BF16), vals)
        m_s[...] = m_new

    @pl.when(j == 0)
    def _():
        m_s[...] = jnp.full((rows, 1), NEG_BIG, F32)
        l_s[...] = jnp.zeros((rows, 1), F32)
        acc_s[...] = jnp.zeros((rows, MLA_KV_RANK), F32)
        knew = knew_ref[...]
        s = _dot_nt(q, knew) * scale
        t_row = lax.broadcasted_iota(jnp.int32, (rows, 1), 0) % nt
        col = lax.broadcasted_iota(jnp.int32, (1, PAGE_SIZE), 1)
        update(jnp.where(col <= t_row, s, NEG_BIG), knew[:, :MLA_KV_RANK])

    for r in range(npg):
        rows_r = slice(r * PAGE_SIZE, (r + 1) * PAGE_SIZE)
        kbuf[rows_r, :] = ckv_refs[r][...].astype(BF16)
        rbuf[:, rows_r] = kr_refs[r][...].astype(BF16)
    lat = kbuf[...]
    s = _dot_nt(q[:, :MLA_KV_RANK], lat) + _dot(q[:, MLA_KV_RANK:MLA_KV_RANK + MLA_ROPE], rbuf[...])
    update(s * scale, lat)

    @pl.when(j == pl.num_programs(1) - 1)
    def _():
        o_ref[...] = acc_s[...] / l_s[...]


def _mla_decode(q, knew, cache_ckv, cache_kr, page_table, layer, nt, npg=16):
    B, rows, _ = q.shape
    n_pages = page_table.shape[1]
    page = lambda r, a, c: pl.BlockSpec(
        (None, None, a, c), lambda b, j, pt: (layer, pt[b, j * npg + r], 0, 0))
    grid_spec = pltpu.PrefetchScalarGridSpec(
        num_scalar_prefetch=1,
        grid=(B, n_pages // npg),
        in_specs=[pl.BlockSpec((None, rows, MLA_KPAD), lambda b, j, pt: (b, 0, 0)),
                  pl.BlockSpec((None, PAGE_SIZE, MLA_KPAD), lambda b, j, pt: (b, 0, 0))]
        + [page(r, PAGE_SIZE, MLA_KV_RANK) for r in range(npg)]
        + [page(r, MLA_ROPE, PAGE_SIZE) for r in range(npg)],
        out_specs=pl.BlockSpec((None, rows, MLA_KV_RANK), lambda b, j, pt: (b, 0, 0)),
        scratch_shapes=[pltpu.VMEM((rows, 1), F32), pltpu.VMEM((rows, 1), F32),
                        pltpu.VMEM((rows, MLA_KV_RANK), F32),
                        pltpu.VMEM((npg * PAGE_SIZE, MLA_KV_RANK), BF16),
                        pltpu.VMEM((MLA_ROPE, npg * PAGE_SIZE), BF16)])
    scale = (MLA_NOPE + MLA_ROPE) ** -0.5
    return pl.pallas_call(
        functools.partial(_mla_decode_kernel, npg=npg, nt=nt, scale=scale),
        grid_spec=grid_spec,
        out_shape=jax.ShapeDtypeStruct((B, rows, MLA_KV_RANK), F32),
        compiler_params=_params("arbitrary", "arbitrary"),
        name="mla_decode",
    )(page_table, q, knew, *([cache_ckv] * npg), *([cache_kr] * npg))


def _bmm_kernel(x_ref, w_ref, o_ref):
    o_ref[...] = _dot(x_ref[...].astype(BF16), w_ref[...])


def _bmm(x, w):
    H, M, K = x.shape
    N = w.shape[2]
    return pl.pallas_call(
        _bmm_kernel,
        grid=(H,),
        in_specs=[pl.BlockSpec((None, M, K), lambda h: (h, 0, 0)),
                  pl.BlockSpec((None, K, N), lambda h: (h, 0, 0))],
        out_specs=pl.BlockSpec((None, M, N), lambda h: (h, 0, 0)),
        out_shape=jax.ShapeDtypeStruct((H, M, N), F32),
        compiler_params=_params("arbitrary"),
        name="bmm",
    )(x, w)


def _ab_in_weights(w_in, s5_width):
    a, b, c = s5_width, s5_width + 384, s5_width + 384 + MLA_KV_RANK
    u, cq, ckv, kr = w_in[:, :a], w_in[:, a:b], w_in[:, b:c], w_in[:, c:]
    return jnp.concatenate(
        [cq, u, ckv, _pad_cols(kr, 128), _pad_cols(_swap_halves(kr), 128)], axis=1).astype(BF16)


def _ab_layer(x, mods, g_norm, pos, B, T, tm, cache, wts, tq=128, tk=512):
    sh1, sc1, g1 = mods
    M = B * T
    W = wts['s5_d'].size
    GP = wts['s5_consts'][0].size
    cq, u, ckv, kr, krs = _norm_mod_mm(x, g_norm, sc1, sh1, wts['w_in'], T, tm,
                                       (384, W, MLA_KV_RANK, 128, 128))
    u_tb = jnp.transpose(u.reshape(B, T, W), (1, 0, 2)).reshape(M, W)
    if cache is None:
        s0_re = jnp.zeros((B, GP), F32)
        s0_im = jnp.zeros((B, GP), F32)
    else:
        s0_re, s0_im = cache['s5_re'].reshape(B, GP), cache['s5_im'].reshape(B, GP)
    s5_tb, s_re, s_im = _s5(u_tb, s0_re, s0_im, wts['s5_consts'], wts['s5_d'], wts['s5_w_glu'],
                            wts['s5_b_glu'], nb=B, tc=min(T, 64))
    s5_out = jnp.transpose(s5_tb.reshape(T, B, W), (1, 0, 2)).reshape(M, W)
    c2, s2 = _rope_tables(pos, 128)
    c2, s2 = jnp.tile(c2, (B, 1)), jnp.tile(s2, (B, 1))
    q, kcat, ckvn, krot = _mla_prep(cq, ckv, kr, krs, c2, s2, wts['mla_g_q'], wts['mla_g_kv'],
                                    wts['mla_wq'], wts['mla_wuk'], tm)
    H = MLA_HEADS
    if cache is None:
        o = _mla_attn(q, kcat, wts['mla_wuv'], B, T, tq, min(tk, T))
    else:
        qd = jnp.transpose(q.reshape(H, B, T, MLA_KPAD), (1, 0, 2, 3)).reshape(B, H * T, MLA_KPAD)
        knew = jnp.pad(kcat.reshape(B, T, MLA_KPAD), ((0, 0), (0, PAGE_SIZE - T), (0, 0)))
        o_lat = _mla_decode(qd, knew, cache['ckv'], jnp.transpose(cache['kr'], (0, 1, 3, 2)),
                            cache['page_table'],
                            cache['layer'], T)
        o_lat = jnp.transpose(o_lat.reshape(B, H, T, MLA_KV_RANK), (1, 0, 2, 3)).reshape(H, M, -1)
        o = jnp.transpose(_bmm(o_lat, wts['mla_wuv']), (1, 0, 2)).reshape(M, H * MLA_V)
    mix = jnp.concatenate([s5_out, o], axis=1)
    x = _mm_resid(mix, wts['w_out'], x, g1, T, tm)
    return x, ckvn, krot, s_re, s_im


def _suffix_sums(ln, tri):
    hi = ln.astype(BF16)
    mid = (ln - hi.astype(F32)).astype(BF16)
    return _dot(hi, tri) + _dot(mid, tri)


def _sb_attn_kernel(q_ref, k_ref, v_ref, tri_ref, o_ref, acc_s, r_s, *, tq, scale):
    qi = pl.program_id(2)
    hs = range(q_ref.shape[0])
    tri = tri_ref[...]
    acc_s[...] = jnp.zeros(acc_s.shape, F32)
    r_s[...] = jnp.zeros(r_s.shape, F32)

    def block(kb, diag):
        rows = pl.ds(pl.multiple_of(kb * tq, tq), tq)
        z = [_dot_nt(q_ref[h], k_ref[h, rows, :]) * scale for h in hs]
        sp = [_softplus(a) for a in z]
        if diag:
            mask = (lax.broadcasted_iota(jnp.int32, (tq, tq), 1)
                    < lax.broadcasted_iota(jnp.int32, (tq, tq), 0))
            ln = [jnp.where(mask, -a, 0.0) for a in sp]
        else:
            ln = [-a for a in sp]
        a = [jnp.exp(z[h] - sp[h] + _suffix_sums(ln[h], tri) + r_s[h]) for h in hs]
        if diag:
            a = [jnp.where(mask, ah, 0.0) for ah in a]
        for h in hs:
            acc_s[h] += _dot(a[h].astype(BF16), v_ref[h, rows, :])
            r_s[h] += jnp.sum(ln[h], axis=-1, keepdims=True)

    block(qi, True)

    def body(i, carry):
        block(qi - 1 - i, False)
        return carry

    lax.fori_loop(0, qi, body, 0)
    o_ref[...] = acc_s[...]


def _tri(n):
    return (lax.broadcasted_iota(jnp.int32, (n, n), 0)
            > lax.broadcasted_iota(jnp.int32, (n, n), 1)).astype(BF16)


def _sb_attn(q, k, v, tq):
    B, H, T, d = q.shape
    hb = 2
    seq = pl.BlockSpec((None, hb, T, d), lambda b, h, i: (b, h, 0, 0))
    blk = pl.BlockSpec((None, hb, tq, d), lambda b, h, i: (b, h, i, 0))
    return pl.pallas_call(
        functools.partial(_sb_attn_kernel, tq=tq, scale=d ** -0.5),
        grid=(B, H // hb, T // tq),
        in_specs=[blk, seq, seq, pl.BlockSpec((tq, tq), lambda b, h, i: (0, 0))],
        out_specs=blk,
        out_shape=jax.ShapeDtypeStruct((B, H, T, d), F32),
        scratch_shapes=[pltpu.VMEM((hb, tq, d), F32), pltpu.VMEM((hb, tq, 1), F32)],
        compiler_params=_params("arbitrary", "arbitrary", "arbitrary"),
        name="sb_attn",
    )(q, k, v, _tri(tq))


def _sb_decode_kernel(pt_ref, q_ref, knew_ref, vnew_ref, tw_ref, *refs, npg, scale):
    k_refs = refs[:npg]
    v_refs = refs[npg:2 * npg]
    o_ref, acc_s, r_s, kbuf, vbuf = refs[2 * npg:]
    j = pl.program_id(1)
    H, qrows, d = q_ref.shape
    hs = range(H)
    tw = tw_ref[...]

    def block(keys, vals, n, fresh):
        cols = n * PAGE_SIZE
        z = jnp.concatenate([_dot(q_ref[h], keys[h]) for h in hs], axis=0) * scale
        sp = _softplus(z)
        if fresh:
            t_row = lax.broadcasted_iota(jnp.int32, (H * qrows, cols), 0) % qrows
            mask = lax.broadcasted_iota(jnp.int32, (H * qrows, cols), 1) < t_row
            ln = jnp.where(mask, -sp, 0.0)
        else:
            ln = -sp
        hi = ln.astype(BF16)
        mid = (ln - hi.astype(F32)).astype(BF16)
        run = r_s[...]
        after = []
        for p in range(n):
            sl = slice(p * PAGE_SIZE, (p + 1) * PAGE_SIZE)
            both = _dot(hi[:, sl], tw) + _dot(mid[:, sl], tw)
            after.append(both[:, :LANES] + run)
            run = run + both[:, LANES:]
        a = jnp.exp(z - sp + jnp.concatenate(after, axis=1))
        if fresh:
            a = jnp.where(mask, a, 0.0)
        a = a.astype(BF16)
        for h in hs:
            acc_s[h] += _dot_nt(a[h * qrows:(h + 1) * qrows], vals[h])
        r_s[...] = run

    @pl.when(j == 0)
    def _():
        acc_s[...] = jnp.zeros(acc_s.shape, F32)
        r_s[...] = jnp.zeros(r_s.shape, F32)
        block(knew_ref, vnew_ref, 1, True)

    for r in range(npg):
        lanes_r = slice(r * PAGE_SIZE, (r + 1) * PAGE_SIZE)
        kbuf[:, :, lanes_r] = k_refs[r][...].astype(BF16)
        vbuf[:, :, lanes_r] = v_refs[r][...].astype(BF16)
    block(kbuf, vbuf, npg, False)

    @pl.when(j == pl.num_programs(1) - 1)
    def _():
        o_ref[...] = acc_s[...]


QROWS = 8


def _sb_decode(q, knew, vnew, cache_kt, cache_vt, page_table, layer, npg=8):
    B, H, qrows, d = q.shape
    assert PAGE_SIZE == LANES
    n_pages = page_table.shape[1]
    page = lambda r: pl.BlockSpec(
        (None, None, H, d, PAGE_SIZE),
        lambda b, j, pt: (layer, pt[b, n_pages - 1 - (j * npg + r)], 0, 0, 0))
    per_b = lambda a, c: pl.BlockSpec((None, H, a, c), lambda b, j, pt: (b, 0, 0, 0))
    ti = lax.broadcasted_iota(jnp.int32, (LANES, 2 * LANES), 0)
    si = lax.broadcasted_iota(jnp.int32, (LANES, 2 * LANES), 1)
    tw = ((ti > si) | (si >= LANES)).astype(BF16)
    grid_spec = pltpu.PrefetchScalarGridSpec(
        num_scalar_prefetch=1,
        grid=(B, n_pages // npg),
        in_specs=[per_b(qrows, d), per_b(d, PAGE_SIZE), per_b(d, PAGE_SIZE),
                  pl.BlockSpec((LANES, 2 * LANES), lambda b, j, pt: (0, 0))]
        + [page(r) for r in range(npg)] * 2,
        out_specs=per_b(qrows, d),
        scratch_shapes=[pltpu.VMEM((H, qrows, d), F32), pltpu.VMEM((H * qrows, LANES), F32),
                        pltpu.VMEM((H, d, npg * PAGE_SIZE), BF16),
                        pltpu.VMEM((H, d, npg * PAGE_SIZE), BF16)])
    return pl.pallas_call(
        functools.partial(_sb_decode_kernel, npg=npg, scale=d ** -0.5),
        grid_spec=grid_spec,
        out_shape=jax.ShapeDtypeStruct((B, H, qrows, d), F32),
        compiler_params=_params("arbitrary", "arbitrary"),
        name="sb_decode",
    )(page_table, q, knew, vnew, tw, *([cache_kt] * npg), *([cache_vt] * npg))


def _rwkv_prep_kernel(p_ref, prev_ref, mu_ref, w0_ref, w2_ref, a0_ref, a2_ref, g2_ref,
                      kkw_ref, ka_ref, ones_ref, r_o, lw_o, k_o, v_o, kk_o, b_o, g_o):
    H, _, N = r_o.shape
    Wd = H * N
    p = p_ref[...]
    ps = p + (prev_ref[...] - p) * mu_ref[...]
    r, k, v = ps[:, :Wd], ps[:, Wd:2 * Wd], ps[:, 2 * Wd:3 * Wd]
    o = 3 * Wd
    xw = ps[:, o:o + RW_DECAY_RANK]
    xa = ps[:, o + RW_DECAY_RANK:o + RW_DECAY_RANK + RW_A_RANK]
    xg = ps[:, o + RW_DECAY_RANK + RW_A_RANK:]
    w_log = -_softplus(-(w0_ref[...] + _dot(jnp.tanh(xw).astype(BF16), w2_ref[...]))) - 0.5
    lw = -jnp.exp(w_log)
    a = _sigmoid(a0_ref[...] + _dot(xa.astype(BF16), a2_ref[...]))
    g = _dot(_sigmoid(xg).astype(BF16), g2_ref[...])
    kk = k * kkw_ref[...]
    ss = _dot_exact_rhs(kk * kk, ones_ref[...])
    kk = kk / jnp.maximum(jnp.sqrt(ss), 1e-12)
    k2 = k * (1.0 + (a - 1.0) * ka_ref[...])
    b = kk * a
    for h in range(H):
        sl = slice(h * N, (h + 1) * N)
        r_o[h] = r[:, sl]
        lw_o[h] = lw[:, sl]
        k_o[h] = k2[:, sl]
        v_o[h] = v[:, sl]
        kk_o[h] = kk[:, sl]
        b_o[h] = b[:, sl]
        g_o[h] = g[:, sl]


def _rwkv_prep(p, prev, wts, tm):
    M, cols = p.shape
    H, N = RW_HEADS, RW_HEAD_DIM
    Wd = H * N
    row = lambda n: pl.BlockSpec((1, n), lambda i: (0, 0))
    mat = lambda a, b: pl.BlockSpec((a, b), lambda i: (0, 0))
    out = pl.BlockSpec((H, tm, N), lambda i: (0, i, 0))
    return pl.pallas_call(
        _rwkv_prep_kernel,
        grid=(M // tm,),
        in_specs=[pl.BlockSpec((tm, cols), lambda i: (i, 0)), pl.BlockSpec((tm, cols), lambda i: (i, 0)),
                  row(cols), row(Wd), mat(RW_DECAY_RANK, Wd), row(Wd), mat(RW_A_RANK, Wd),
                  mat(RW_GATE_RANK, Wd), row(Wd), row(Wd), mat(Wd, Wd)],
        out_specs=[out] * 7,
        out_shape=[jax.ShapeDtypeStruct((H, M, N), F32)] * 7,
        compiler_params=_params("arbitrary"),
        name="rwkv_prep",
    )(p, prev, wts['mu'], wts['w0'], wts['w2'], wts['a0'], wts['a2'], wts['g2'],
      wts['k_k'], wts['k_a'], wts['head_ones'])


def _dot3(dot, a, b):
    a0 = a.astype(BF16)
    a1 = (a - a0.astype(F32)).astype(BF16)
    b0 = b.astype(BF16)
    b1 = (b - b0.astype(F32)).astype(BF16)
    return dot(a0, b0) + (dot(a0, b1) + dot(a1, b0))


SOLVE_BLOCK = 16


def _unit_lower_solve(Ls, Rs, C):
    n = range(len(Ls))
    if C <= SOLVE_BLOCK:
        Us = list(Rs)
        for s in range(C - 1):
            Us = [Us[i] - Ls[i][:, s:s + 1] * Us[i][s:s + 1, :] for i in n]
        return Us
    nb = C // SOLVE_BLOCK
    assert nb * SOLVE_BLOCK == C and nb <= 4
    ti = lax.broadcasted_iota(jnp.int32, (C, C), 0)
    si = lax.broadcasted_iota(jnp.int32, (C, C), 1)
    same = (ti // SOLVE_BLOCK) == (si // SOLVE_BLOCK)
    rep_t = (lax.broadcasted_iota(jnp.int32, (C, SOLVE_BLOCK), 0) % SOLVE_BLOCK
             == lax.broadcasted_iota(jnp.int32, (C, SOLVE_BLOCK), 1))
    rep = (lax.broadcasted_iota(jnp.int32, (SOLVE_BLOCK, C), 1) % SOLVE_BLOCK
           == lax.broadcasted_iota(jnp.int32, (SOLVE_BLOCK, C), 0)).astype(BF16)
    Ld = [jnp.where(same, L, 0.0) for L in Ls]
    Ldc = [_dot_exact_rhs(a, rep_t.astype(BF16)) for a in Ld]
    X = [rep_t.astype(F32) for _ in n]
    for s in range(SOLVE_BLOCK - 1):
        rows = [jnp.broadcast_to(x.reshape(nb, SOLVE_BLOCK, SOLVE_BLOCK)[:, s:s + 1, :],
                                 (nb, SOLVE_BLOCK, SOLVE_BLOCK)).reshape(C, SOLVE_BLOCK) for x in X]
        X = [X[i] - Ldc[i][:, s:s + 1] * rows[i] for i in n]
    T1 = [jnp.where(same, _dot_exact_rhs(x, rep), 0.0) for x in X]
    R1 = [_dot3(_dot, T1[i], Rs[i]) for i in n]
    Mm = [_dot3(_dot, T1[i], Ls[i] - Ld[i]) for i in n]
    M2 = [_dot3(_dot, m, m) for m in Mm]
    Y1 = [R1[i] + _dot3(_dot, M2[i], R1[i]) for i in n]
    return [Y1[i] - _dot3(_dot, Mm[i], Y1[i]) for i in n]


def _rwkv_chunk_kernel(r_ref, lw_ref, k_ref, v_ref, kk_ref, b_ref, g_ref, s0_ref,
                       lnw_ref, lnb_ref, rk_ref, y_ref, s_ref, *, C):
    c = pl.program_id(2)

    @pl.when(c == 0)
    def _():
        s_ref[...] = s0_ref[...]

    hs = range(r_ref.shape[0])
    ti = lax.broadcasted_iota(jnp.int32, (C, C), 0)
    si = lax.broadcasted_iota(jnp.int32, (C, C), 1)
    strict, incl = si < ti, si <= ti
    tril = incl.astype(BF16)
    r, lw, k, v = ([ref[h] for h in hs] for ref in (r_ref, lw_ref, k_ref, v_ref))
    kk, b = ([ref[h] for h in hs] for ref in (kk_ref, b_ref))
    S0 = [s_ref[h] for h in hs]
    parts = [_split3(a) for a in lw]
    cum = [_dot(tril, p0) + _dot(tril, p1) + _dot(tril, p2) for p0, p1, p2 in parts]
    g_in = [jnp.exp(a) for a in cum]
    g_inv = [jnp.exp(-a) for a in cum]
    qt = [kk[h] * jnp.exp(cum[h] - lw[h]) for h in hs]
    bt = [b[h] * g_inv[h] for h in hs]
    kt = [k[h] * g_inv[h] for h in hs]
    rt = [r[h] * g_in[h] for h in hs]
    Lb = [jnp.where(strict, _dot3(_dot_nt, qt[h], bt[h]), 0.0) for h in hs]
    Lk = [jnp.where(strict, _dot3(_dot_nt, qt[h], kt[h]), 0.0) for h in hs]
    Ab = [jnp.where(incl, _dot3(_dot_nt, rt[h], bt[h]), 0.0) for h in hs]
    Ak = [jnp.where(incl, _dot3(_dot_nt, rt[h], kt[h]), 0.0) for h in hs]
    rhs = [-(_dot3(_dot_nt, qt[h], S0[h]) + _dot3(_dot, Lk[h], v[h])) for h in hs]
    U = _unit_lower_solve(Lb, rhs, C)
    y = [_dot3(_dot_nt, rt[h], S0[h]) + _dot3(_dot, Ab[h], U[h]) + _dot3(_dot, Ak[h], v[h])
         for h in hs]
    for h in hs:
        s_ref[h] = ((S0[h] + _dot3(_dot_tn, U[h], bt[h]) + _dot3(_dot_tn, v[h], kt[h]))
                    * g_in[h][C - 1:C, :])
    for h in hs:
        mean = jnp.mean(y[h], axis=-1, keepdims=True)
        var = jnp.mean(jnp.square(y[h] - mean), axis=-1, keepdims=True)
        yn = (y[h] - mean) * lax.rsqrt(var + RW_GN_EPS) * lnw_ref[h] + lnb_ref[h]
        bonus = jnp.sum(r[h] * k[h] * rk_ref[h], axis=-1, keepdims=True) * v[h]
        y_ref[h] = (yn + bonus) * g_ref[h]


def _rwkv_chunks(feats, s0, ln_w, ln_b, r_k, B, T, C, Hb=8):
    H, M, N = feats[0].shape
    nc = T // C
    seq = pl.BlockSpec((Hb, C, N), lambda b, hb, c: (hb, b * nc + c, 0))
    st = pl.BlockSpec((None, Hb, N, N), lambda b, hb, c: (b, hb, 0, 0))
    par = pl.BlockSpec((Hb, 1, N), lambda b, hb, c: (hb, 0, 0))
    return pl.pallas_call(
        functools.partial(_rwkv_chunk_kernel, C=C),
        grid=(B, H // Hb, nc),
        in_specs=[seq] * 7 + [st, par, par, par],
        out_specs=[seq, st],
        out_shape=[jax.ShapeDtypeStruct((H, M, N), F32), jax.ShapeDtypeStruct((B, H, N, N), F32)],
        compiler_params=_params("arbitrary", "arbitrary", "arbitrary"),
        name="rwkv_chunks",
    )(*feats, s0, ln_w.reshape(H, 1, N), ln_b.reshape(H, 1, N), r_k.reshape(H, 1, N))


def _cd_weights(w):
    H, N = RW_HEADS, RW_HEAD_DIM
    Wd = H * N
    row = lambda a: a.reshape(1, -1)
    head = jnp.arange(Wd) // N
    return {
        'w_in': w['cd_w_in'].astype(BF16), 'w_out': w['cd_w_out'].astype(BF16),
        'mu': row(w['rwkv_mu']), 'w0': row(w['rwkv_w0']), 'w2': w['rwkv_w2'].astype(BF16),
        'a0': row(w['rwkv_a0']), 'a2': w['rwkv_a2'].astype(BF16), 'g2': w['rwkv_g2'].astype(BF16),
        'k_k': row(w['rwkv_k_k']), 'k_a': row(w['rwkv_k_a']),
        'head_ones': (head[:, None] == head[None, :]).astype(BF16),
        'ln_w': w['rwkv_ln_w'], 'ln_b': w['rwkv_ln_b'], 'r_k': w['rwkv_r_k'],
    }


def _ab_weights(w):
    wq, wuk = _mla_weights(w['mla_w_uq'], w['mla_w_uk'])
    return {
        'w_in': _ab_in_weights(w['ab_w_in'], w['s5_d'].size), 'w_out': w['ab_w_out'].astype(BF16),
        's5_consts': _s5_consts(w['s5_lambda_re'], w['s5_lambda_im'], w['s5_log_dt'],
                                w['s5_b_re'], w['s5_b_im'], w['s5_c_re'], w['s5_c_im']),
        's5_d': w['s5_d'], 's5_w_glu': w['s5_w_glu'], 's5_b_glu': w['s5_b_glu'],
        'mla_g_q': w['mla_g_q'], 'mla_g_kv': w['mla_g_kv'], 'mla_wq': wq, 'mla_wuk': wuk,
        'mla_wuv': jnp.transpose(w['mla_w_uv'], (1, 0, 2)).astype(BF16),
    }


def _cd_layer(x, mods, g_norm, B, T, tm, cache, wts, tq=256):
    sh1, sc1, g1 = mods
    M = B * T
    H, d = SB_HEADS, SB_HEAD_DIM
    Wsb = H * d
    cols = wts['mu'].shape[1]
    q, k, v, rw = _norm_mod_mm(x, g_norm, sc1, sh1, wts['w_in'], T, tm, (Wsb, Wsb, Wsb, cols))
    if cache is None:
        heads = lambda a: jnp.transpose(a.reshape(B, T, H, d), (0, 2, 1, 3)).astype(BF16)
        sb = _sb_attn(heads(q), heads(k), heads(v), tq)
        sb = jnp.transpose(sb, (0, 2, 1, 3)).reshape(M, Wsb)
    else:
        keys_minor = lambda a: jnp.transpose(a, (0, 1, 3, 4, 2))
        heads = lambda a: jnp.transpose(a.reshape(B, T, H, d), (0, 2, 1, 3))
        qh = jnp.pad(heads(q), ((0, 0), (0, 0), (0, QROWS - T), (0, 0))).astype(BF16)
        fresh = lambda a: jnp.pad(jnp.transpose(heads(a), (0, 1, 3, 2)),
                                  ((0, 0), (0, 0), (0, 0), (0, PAGE_SIZE - T))).astype(BF16)
        sb = _sb_decode(qh, fresh(k), fresh(v), keys_minor(cache['sb_k']),
                        keys_minor(cache['sb_v']), cache['page_table'], cache['layer'])
        sb = jnp.transpose(sb[:, :, :T], (0, 2, 1, 3)).reshape(M, Wsb)
    rw3 = rw.reshape(B, T, cols)
    shift0 = jnp.zeros((B, cols), F32) if cache is None else cache['shift']
    prev = jnp.concatenate([shift0[:, None, :], rw3[:, :-1]], axis=1).reshape(M, cols)
    feats = _rwkv_prep(rw, prev, wts, tm)
    N = RW_HEAD_DIM
    C = min(T, 64)
    if T < 8:
        C = 8
        padt = lambda a: jnp.pad(a.reshape(RW_HEADS, B, T, N),
                                 ((0, 0), (0, 0), (0, C - T), (0, 0))).reshape(RW_HEADS, B * C, N)
        feats = [padt(f) for f in feats]
    s0 = (jnp.zeros((B, RW_HEADS, N, N), F32) if cache is None else cache['wkv'])
    Tp = max(T, C)
    y, wkv = _rwkv_chunks(feats, s0, wts['ln_w'], wts['ln_b'], wts['r_k'], B, Tp, C)
    y = y.reshape(RW_HEADS, B, Tp, N)[:, :, :T]
    rw_out = jnp.transpose(y, (1, 2, 0, 3)).reshape(M, RW_HEADS * N)
    mix = jnp.concatenate([sb, rw_out], axis=1)
    x = _mm_resid(mix, wts['w_out'], x, g1, T, tm)
    return x, k, v, wkv, rw3[:, -1]


LANES = 128


def _top_values(s, n):
    out = []
    rank = jnp.full(s.shape, float(n), F32)
    for i in range(n):
        m = jnp.max(s, axis=0, keepdims=True)
        out.append(m)
        hit = s == m
        rank = jnp.where(hit, float(i), rank)
        s = jnp.where(hit, NEG_BIG, s)
    return out, rank


def _peer_route_kernel(x_ref, g_ref, sc_ref, sh_ref, wq_ref, k1_ref, k2_ref,
                       h_ref, e1_ref, j_ref, r2_ref, e2_ref, s1_s, s2_s):
    H, NK, tm = s1_s.shape
    half = k1_ref.shape[2]
    K = PEER_TOPK
    h = (_rms(x_ref[...]) * g_ref[...] * (1.0 + sc_ref[...]) + sh_ref[...]).astype(BF16)
    h_ref[...] = h
    q = _dot(h, wq_ref[...]).astype(BF16)
    for hd in range(H):
        base = hd * 2 * half
        s1_s[hd] = _dot_nt(k1_ref[hd], q[:, base:base + half])
        s2_s[hd] = _dot_nt(k2_ref[hd], q[:, base + half:base + 2 * half])
    row16 = lax.broadcasted_iota(jnp.int32, (K, LANES), 0)
    row8 = lax.broadcasted_iota(jnp.int32, (8, LANES), 0)

    def lane_group(c, carry):
        lanes = pl.ds(pl.multiple_of(c * LANES, LANES), LANES)
        for hd in range(H):
            s1 = s1_s[hd, :, lanes]
            s2 = s2_s[hd, :, lanes]
            v1, _ = _top_values(s1, K)
            v2, rank = _top_values(s2, K)
            v2t = jnp.zeros((K, LANES), F32)
            for i in range(K):
                v2t = jnp.where(row16 == i, v2[i], v2t)
            cands = [v1[0] + v2t]
            for i in range(1, K):
                cands.append(jnp.where(row8 < K // (i + 1), v1[i] + v2t[:8], NEG_BIG))
            best = []
            for _ in range(K):
                m = jnp.max(cands[0], axis=0, keepdims=True)
                for cnd in cands[1:]:
                    m = jnp.maximum(m, jnp.max(cnd, axis=0, keepdims=True))
                best.append(m)
                cands = [jnp.where(cnd == m, NEG_BIG, cnd) for cnd in cands]
            tau = best[K - 1]
            z = jnp.ones((1, LANES), F32)
            for n in range(1, K):
                z = z + jnp.exp(best[n] - best[0])
            cnt = jnp.zeros((NK, LANES), F32)
            for i in range(K):
                cnt = cnt + jnp.where(s1 + v2[i] >= tau, 1.0, 0.0)
            e1_ref[hd, :, lanes] = jnp.exp(s1 - v1[0]) / z
            j_ref[hd, :, lanes] = cnt
            r2_ref[hd, :, lanes] = rank.astype(BF16)
            e2_ref[hd, :, lanes] = jnp.exp(s2 - v2[0]).astype(BF16)
        return carry

    lax.fori_loop(0, tm // LANES, lane_group, 0)


def _peer_route(x, g, sc, sh, wq, k1, k2, T, tm):
    M, D = x.shape
    H, NK, half = k1.shape
    tab = pl.BlockSpec((H, NK, tm), lambda i: (0, 0, i))
    return pl.pallas_call(
        _peer_route_kernel,
        grid=(M // tm,),
        in_specs=[pl.BlockSpec((tm, D), lambda i: (i, 0)),
                  pl.BlockSpec((1, D), lambda i: (0, 0)),
                  _mod_spec(T, tm, D), _mod_spec(T, tm, D),
                  pl.BlockSpec((D, wq.shape[1]), lambda i: (0, 0)),
                  pl.BlockSpec((H, NK, half), lambda i: (0, 0, 0)),
                  pl.BlockSpec((H, NK, half), lambda i: (0, 0, 0))],
        out_specs=[pl.BlockSpec((tm, D), lambda i: (i, 0)), tab, tab, tab, tab],
        out_shape=[jax.ShapeDtypeStruct((M, D), BF16)]
        + [jax.ShapeDtypeStruct((H, NK, M), dt) for dt in (F32, F32, BF16, BF16)],
        scratch_shapes=[pltpu.VMEM((H, NK, tm), F32), pltpu.VMEM((H, NK, tm), F32)],
        compiler_params=_params("arbitrary"),
        name="peer_route",
    )(x, g.reshape(1, D), _mod_rows(sc, T, tm), _mod_rows(sh, T, tm), wq, k1, k2)


def _peer_expert_kernel(h_ref, u_ref, vt_ref, e1_ref, j_ref, r2_ref, e2_ref, x_ref, gate_ref,
                        o_ref, acc_s, act_s, g_s, *, A):
    j = pl.program_id(1)
    H, NK, tm = r2_ref.shape

    @pl.when(j == 0)
    def _():
        acc_s[...] = jnp.zeros(acc_s.shape, F32)

    act_s[...] = _dot_nt(u_ref[...], h_ref[...])
    GA = 4
    zero = jnp.zeros((NK, LANES), BF16)
    for c in range(tm // LANES):
        lanes = slice(c * LANES, (c + 1) * LANES)
        for a0 in range(0, A, GA):
            ws = [zero] * GA
            for hd in range(H):
                r2 = r2_ref[hd, :, lanes]
                e2 = e2_ref[hd, :, lanes]
                for i in range(GA):
                    al = a0 + i
                    e1 = e1_ref[hd, al:al + 1, lanes].astype(BF16)
                    cnt = j_ref[hd, al:al + 1, lanes].astype(BF16)
                    ws[i] = ws[i] + jnp.where(r2 < cnt, e2, zero) * e1
            for i in range(GA):
                rows = slice((a0 + i) * NK, (a0 + i + 1) * NK)
                g_s[rows, lanes] = ws[i] * _gelu(act_s[rows, lanes]).astype(BF16)
    acc_s[...] += _dot(vt_ref[...], g_s[...])

    @pl.when(j == pl.num_programs(1) - 1)
    def _():
        o_ref[...] = x_ref[...] + gate_ref[...] * acc_s[...].T


def _peer_experts(h, u, vt, tabs, x, gate, T, tm, te=1024):
    M, D = x.shape
    E = u.shape[0]
    H, NK, _ = tabs[0].shape
    A = te // NK
    tab = pl.BlockSpec((H, NK, tm), lambda i, j: (0, 0, i))
    tab_a = pl.BlockSpec((H, A, tm), lambda i, j: (0, j, i))
    return pl.pallas_call(
        functools.partial(_peer_expert_kernel, A=A),
        grid=(M // tm, E // te),
        in_specs=[pl.BlockSpec((tm, D), lambda i, j: (i, 0)),
                  pl.BlockSpec((te, D), lambda i, j: (j, 0)),
                  pl.BlockSpec((D, te), lambda i, j: (0, j)),
                  tab_a, tab_a, tab, tab,
                  pl.BlockSpec((tm, D), lambda i, j: (i, 0)),
                  _mod_spec2(T, tm, D)],
        out_specs=pl.BlockSpec((tm, D), lambda i, j: (i, 0)),
        out_shape=jax.ShapeDtypeStruct((M, D), F32),
        scratch_shapes=[pltpu.VMEM((D, tm), F32), pltpu.VMEM((te, tm), F32),
                        pltpu.VMEM((te, tm), BF16)],
        compiler_params=_params("arbitrary", "arbitrary"),
        name="peer_experts",
    )(h, u, vt, *tabs, x, _mod_rows(gate, T, tm))


def _mod_spec2(T, tm, D):
    if tm <= T:
        return pl.BlockSpec((None, 1, D), lambda i, j: (i * tm // T, 0, 0))
    return pl.BlockSpec((None, tm, D), lambda i, j: (0, i, 0))


ROW_TILE = 512


def _trunk(x, c, w, ab_wts, cd_wts, peer_wts, cache):
    B, T, D = x.shape
    M = B * T
    tm = min(ROW_TILE, M)
    depth = w['ada_w'].shape[0]
    past_len = 0 if cache is None else cache['page_table'].shape[1] * PAGE_SIZE
    pos = past_len + jnp.arange(T, dtype=jnp.int32)
    mod = _ada(c, w['ada_w'], w['ada_b'])
    xf = x.reshape(M, D)
    ckv_rows, kr_rows, s5r, s5i, k_rows, v_rows, wkvs, shifts = [], [], [], [], [], [], [], []
    for l in range(depth):
        i = l // 2
        sh1, sc1, g1, sh2, sc2, g2 = jnp.split(mod[l], 6, axis=-1)
        if l % 2 == 0:
            lc = None if cache is None else dict(
                ckv=cache['mla_ckv'], kr=cache['mla_krope'], page_table=cache['page_table'], layer=i,
                s5_re=cache['s5_re'][i], s5_im=cache['s5_im'][i])
            xf, ckv, kr, sr, si = _ab_layer(xf, (sh1, sc1, g1), w['norm1_g'][l], pos, B, T, tm, lc,
                                            ab_wts[i])
            ckv_rows.append(ckv.reshape(B, T, -1))
            kr_rows.append(kr.reshape(B, T, -1))
            state_shape = (B,) + w['s5_lambda_re'].shape[1:]
            s5r.append(sr.reshape(state_shape))
            s5i.append(si.reshape(state_shape))
        else:
            lc = None if cache is None else dict(
                sb_k=cache['sb_k'], sb_v=cache['sb_v'], page_table=cache['page_table'], layer=i,
                wkv=cache['wkv'][i], shift=cache['shift'][i])
            xf, k, v, wkv, shift = _cd_layer(xf, (sh1, sc1, g1), w['norm1_g'][l], B, T, tm, lc,
                                             cd_wts[i])
            k_rows.append(k.reshape(B, T, SB_HEADS, SB_HEAD_DIM))
            v_rows.append(v.reshape(B, T, SB_HEADS, SB_HEAD_DIM))
            wkvs.append(wkv)
            shifts.append(shift)
        pw = peer_wts[l]
        h2, *tabs = _peer_route(xf, w['norm2_g'][l], sc2, sh2, pw['wq'], pw['k1'], pw['k2'], T, tm)
        xf = _peer_experts(h2, pw['u'], pw['vt'], tabs, xf, g2, T, tm)
    y = _rmsnorm(xf, w['final_g'], tm).reshape(B, T, D)
    return (y, jnp.stack(ckv_rows), jnp.stack(kr_rows), jnp.stack(k_rows), jnp.stack(v_rows),
            jnp.stack(s5r), jnp.stack(s5i), jnp.stack(wkvs), jnp.stack(shifts))


def kernel(x_prompt, x_sample, c_prompt, c_sample, cache_mla_ckv, cache_mla_krope, cache_sb_k,
           cache_sb_v, page_table, state_s5_re, state_s5_im, state_rwkv_wkv, state_rwkv_shift,
           ada_w, ada_b, norm1_g, norm2_g, final_g, ab_w_in, ab_w_out, s5_lambda_re, s5_lambda_im,
           s5_log_dt, s5_b_re, s5_b_im, s5_c_re, s5_c_im, s5_d, s5_w_glu, s5_b_glu, mla_g_q,
           mla_w_uq, mla_g_kv, mla_w_uk, mla_w_uv, cd_w_in, cd_w_out, rwkv_mu, rwkv_w0, rwkv_w2,
           rwkv_a0, rwkv_a2, rwkv_g2, rwkv_k_k, rwkv_k_a, rwkv_r_k, rwkv_ln_w, rwkv_ln_b,
           peer_w_q, peer_k1, peer_k2, peer_u, peer_v):
    w = dict(ada_w=ada_w, ada_b=ada_b, norm1_g=norm1_g, norm2_g=norm2_g, final_g=final_g,
             s5_lambda_re=s5_lambda_re)
    ab = dict(ab_w_in=ab_w_in, ab_w_out=ab_w_out, s5_lambda_re=s5_lambda_re,
              s5_lambda_im=s5_lambda_im, s5_log_dt=s5_log_dt, s5_b_re=s5_b_re, s5_b_im=s5_b_im,
              s5_c_re=s5_c_re, s5_c_im=s5_c_im, s5_d=s5_d, s5_w_glu=s5_w_glu, s5_b_glu=s5_b_glu,
              mla_g_q=mla_g_q, mla_w_uq=mla_w_uq, mla_g_kv=mla_g_kv, mla_w_uk=mla_w_uk,
              mla_w_uv=mla_w_uv)
    cd = dict(cd_w_in=cd_w_in, cd_w_out=cd_w_out, rwkv_mu=rwkv_mu, rwkv_w0=rwkv_w0,
              rwkv_w2=rwkv_w2, rwkv_a0=rwkv_a0, rwkv_a2=rwkv_a2, rwkv_g2=rwkv_g2,
              rwkv_k_k=rwkv_k_k, rwkv_k_a=rwkv_k_a, rwkv_r_k=rwkv_r_k, rwkv_ln_w=rwkv_ln_w,
              rwkv_ln_b=rwkv_ln_b)
    depth = ada_w.shape[0]
    ab_wts = [_ab_weights({k: v[i] for k, v in ab.items()}) for i in range((depth + 1) // 2)]
    cd_wts = [_cd_weights({k: v[i] for k, v in cd.items()}) for i in range(depth // 2)]
    peer_wts = [dict(wq=peer_w_q[l].astype(BF16), k1=peer_k1[l].astype(BF16),
                     k2=peer_k2[l].astype(BF16), u=peer_u[l].astype(BF16),
                     vt=peer_v[l].T.astype(BF16)) for l in range(depth)]
    cache = dict(mla_ckv=cache_mla_ckv, mla_krope=cache_mla_krope, sb_k=cache_sb_k, sb_v=cache_sb_v,
                 page_table=page_table, s5_re=state_s5_re, s5_im=state_s5_im, wkv=state_rwkv_wkv,
                 shift=state_rwkv_shift)
    p = _trunk(x_prompt, c_prompt, w, ab_wts, cd_wts, peer_wts, None)
    s = _trunk(x_sample, c_sample, w, ab_wts, cd_wts, peer_wts, cache)
    return (p[0], s[0]) + p[1:] + s[1:]
```

```python
import functools
import math

import jax
import jax.numpy as jnp
from jax import lax
from jax.experimental import pallas as pl
from jax.experimental.pallas import tpu as pltpu

F32 = jnp.float32
BF16 = jnp.bfloat16

NORM_EPS = 1e-6
ROPE_BASE = 10000.0
PAGE_SIZE = 128
S5_GROUP = 16
S5_STATE = 64
MLA_HEADS = 8
MLA_NOPE = 64
MLA_ROPE = 32
MLA_V = 64
MLA_KV_RANK = 256
MLA_KPAD = 384
SB_HEADS = 8
SB_HEAD_DIM = 64
RW_HEADS = 8
RW_HEAD_DIM = 64
RW_DECAY_RANK = 64
RW_A_RANK = 64
RW_GATE_RANK = 128
RW_GN_EPS = 64e-5
PEER_HEADS = 8
PEER_N_KEYS = 128
PEER_TOPK = 16
NEG_BIG = -3.0e38

VMEM_LIMIT = 48 * 1024 * 1024


def _params(*sem):
    return pltpu.CompilerParams(dimension_semantics=sem, vmem_limit_bytes=VMEM_LIMIT)


def _dot(a, b):
    return jnp.dot(a, b, preferred_element_type=F32)


def _dot_nt(a, b):
    return lax.dot_general(a, b, (((1,), (1,)), ((), ())), preferred_element_type=F32)


def _dot_tn(a, b):
    return lax.dot_general(a, b, (((0,), (0,)), ((), ())), preferred_element_type=F32)


def _split3(a):
    hi = a.astype(BF16)
    r1 = a - hi.astype(F32)
    mid = r1.astype(BF16)
    lo = (r1 - mid.astype(F32)).astype(BF16)
    return hi, mid, lo


def _dotx(dot, a, b):
    a0, a1, a2 = _split3(a)
    b0, b1, b2 = _split3(b)
    return (dot(a0, b0) + (dot(a0, b1) + dot(a1, b0))
            + (dot(a1, b1) + dot(a0, b2) + dot(a2, b0)))


def _dot_exact_rhs(a, b_bf16):
    a0, a1, a2 = _split3(a)
    return _dot(a0, b_bf16) + _dot(a1, b_bf16) + _dot(a2, b_bf16)


def _gelu(x):
    c = math.sqrt(2.0 / math.pi)
    return 0.5 * x * (1.0 + jnp.tanh(c * (x + 0.044715 * (x * x * x))))


def _sigmoid(x):
    return 1.0 / (1.0 + jnp.exp(-x))


def _softplus(x):
    return jnp.maximum(x, 0.0) + jnp.log1p(jnp.exp(-jnp.abs(x)))


def _rms(x):
    return x * lax.rsqrt(jnp.mean(x * x, axis=-1, keepdims=True) + NORM_EPS)


def _ada_kernel(c_ref, w_ref, b_ref, o_ref):
    c = c_ref[...]
    s = (c * _sigmoid(c)).astype(BF16)
    o_ref[...] = _dot(s, w_ref[...].astype(BF16)) + b_ref[...]


def _ada(c, ada_w, ada_b):
    L, D, N = ada_w.shape
    Bc = c.shape[0]
    tn = 1024
    return pl.pallas_call(
        _ada_kernel,
        grid=(L, N // tn),
        in_specs=[pl.BlockSpec((Bc, D), lambda l, j: (0, 0)),
                  pl.BlockSpec((None, D, tn), lambda l, j: (l, 0, j)),
                  pl.BlockSpec((None, 1, tn), lambda l, j: (l, 0, j))],
        out_specs=pl.BlockSpec((None, Bc, tn), lambda l, j: (l, 0, j)),
        out_shape=jax.ShapeDtypeStruct((L, Bc, N), F32),
        compiler_params=_params("arbitrary", "arbitrary"),
        name="ada",
    )(c, ada_w, ada_b.reshape(L, 1, N))


def _mod_spec(T, tm, D):
    if tm <= T:
        return pl.BlockSpec((None, 1, D), lambda i: (i * tm // T, 0, 0))
    return pl.BlockSpec((None, tm, D), lambda i: (0, i, 0))


def _mod_rows(m, T, tm):
    if tm <= T:
        return m[:, None, :]
    return jnp.repeat(m, T, axis=0)[None]


def _norm_mod_mm_kernel(x_ref, g_ref, sc_ref, sh_ref, w_ref, *o_refs):
    h = _rms(x_ref[...]) * g_ref[...] * (1.0 + sc_ref[...]) + sh_ref[...]
    res = _dot(h.astype(BF16), w_ref[...])
    off = 0
    for o_ref in o_refs:
        n = o_ref.shape[1]
        o_ref[...] = res[:, off:off + n]
        off += n


def _norm_mod_mm(x, g, sc, sh, w, T, tm, widths):
    M, D = x.shape
    N = w.shape[1]
    assert sum(widths) == N
    return pl.pallas_call(
        _norm_mod_mm_kernel,
        grid=(M // tm,),
        in_specs=[pl.BlockSpec((tm, D), lambda i: (i, 0)),
                  pl.BlockSpec((1, D), lambda i: (0, 0)),
                  _mod_spec(T, tm, D), _mod_spec(T, tm, D),
                  pl.BlockSpec((D, N), lambda i: (0, 0))],
        out_specs=[pl.BlockSpec((tm, n), lambda i: (i, 0)) for n in widths],
        out_shape=[jax.ShapeDtypeStruct((M, n), F32) for n in widths],
        compiler_params=_params("arbitrary"),
        name="norm_mod_mm",
    )(x, g.reshape(1, D), _mod_rows(sc, T, tm), _mod_rows(sh, T, tm), w)


def _mm_resid_kernel(a_ref, w_ref, x_ref, gate_ref, o_ref):
    o_ref[...] = x_ref[...] + gate_ref[...] * _dot(a_ref[...].astype(BF16), w_ref[...])


def _mm_resid(a, w, x, gate, T, tm):
    M, K = a.shape
    D = w.shape[1]
    return pl.pallas_call(
        _mm_resid_kernel,
        grid=(M // tm,),
        in_specs=[pl.BlockSpec((tm, K), lambda i: (i, 0)),
                  pl.BlockSpec((K, D), lambda i: (0, 0)),
                  pl.BlockSpec((tm, D), lambda i: (i, 0)),
                  _mod_spec(T, tm, D)],
        out_specs=pl.BlockSpec((tm, D), lambda i: (i, 0)),
        out_shape=jax.ShapeDtypeStruct((M, D), F32),
        compiler_params=_params("arbitrary"),
        name="mm_resid",
    )(a, w, x, _mod_rows(gate, T, tm))


def _rmsnorm_kernel(x_ref, g_ref, o_ref):
    o_ref[...] = _rms(x_ref[...]) * g_ref[...]


def _rmsnorm(x, g, tm):
    M, D = x.shape
    return pl.pallas_call(
        _rmsnorm_kernel,
        grid=(M // tm,),
        in_specs=[pl.BlockSpec((tm, D), lambda i: (i, 0)),
                  pl.BlockSpec((1, D), lambda i: (0, 0))],
        out_specs=pl.BlockSpec((tm, D), lambda i: (i, 0)),
        out_shape=jax.ShapeDtypeStruct((M, D), F32),
        compiler_params=_params("arbitrary"),
        name="final_norm",
    )(x, g.reshape(1, D))


def _s5_kernel(u_ref, s0r_ref, s0i_ref, abr_ref, abi_ref, cor_ref, coi_ref,
               wbr_ref, wbi_ref, wcr_ref, wci_ref, d_ref, wg_ref, bg_ref,
               o_ref, sr_ref, si_ref, xr_buf, xi_buf, *, nb, tc):
    c = pl.program_id(0)

    @pl.when(c == 0)
    def _():
        sr_ref[...] = s0r_ref[...]
        si_ref[...] = s0i_ref[...]

    u = u_ref[...]
    ub = u.astype(BF16)
    bu_r = _dot(ub, wbr_ref[...])
    bu_i = _dot(ub, wbi_ref[...])
    cor, coi = cor_ref[...], coi_ref[...]
    xr_buf[...] = cor * bu_r - coi * bu_i
    xi_buf[...] = cor * bu_i + coi * bu_r
    abr, abi = abr_ref[...], abi_ref[...]

    def step(t, carry):
        xr, xi = carry
        rows = pl.ds(pl.multiple_of(t * nb, nb), nb)
        nr = abr * xr - abi * xi + xr_buf[rows, :]
        ni = abr * xi + abi * xr + xi_buf[rows, :]
        xr_buf[rows, :] = nr
        xi_buf[rows, :] = ni
        return nr, ni

    xr, xi = lax.fori_loop(0, tc, step, (sr_ref[...], si_ref[...]))
    sr_ref[...] = xr
    si_ref[...] = xi

    y = (_dot(xr_buf[...].astype(BF16), wcr_ref[...])
         - _dot(xi_buf[...].astype(BF16), wci_ref[...]) + d_ref[...] * u)
    z = _gelu(y)
    o_ref[...] = z * _sigmoid(_dot(z.astype(BF16), wg_ref[...]) + bg_ref[...])


def _s5_consts(lam_re, lam_im, log_dt, b_re, b_im, c_re, c_im):
    G, P = lam_re.shape
    Hc = b_re.shape[-1]
    dt = jnp.exp(log_dt)[:, None]
    mag = jnp.exp(lam_re * dt)
    ab_re, ab_im = mag * jnp.cos(lam_im * dt), mag * jnp.sin(lam_im * dt)
    den = lam_re * lam_re + lam_im * lam_im
    nr = ab_re - 1.0
    co_re = (nr * lam_re + ab_im * lam_im) / den
    co_im = (ab_im * lam_re - nr * lam_im) / den
    eye = jnp.eye(G, dtype=F32)
    wb = lambda b: jnp.einsum('gph,gk->ghkp', b, eye).reshape(G * Hc, G * P).astype(BF16)
    wc = lambda cc: jnp.einsum('ghp,gk->gpkh', cc, eye).reshape(G * P, G * Hc).astype(BF16)
    flat = lambda a: a.reshape(1, G * P)
    return (flat(ab_re), flat(ab_im), flat(co_re), flat(co_im),
            wb(b_re), wb(b_im), wc(c_re), wc(c_im))


def _s5(u_tb, s0_re, s0_im, consts, d, w_glu, b_glu, nb, tc):
    M, W = u_tb.shape
    T = M // nb
    GP = s0_re.shape[1]
    ab_re, ab_im, co_re, co_im, wbr, wbi, wcr, wci = consts
    rows = tc * nb
    full = lambda shape: pl.BlockSpec(shape, lambda c: (0,) * len(shape))
    return pl.pallas_call(
        functools.partial(_s5_kernel, nb=nb, tc=tc),
        grid=(T // tc,),
        in_specs=[pl.BlockSpec((rows, W), lambda c: (c, 0)),
                  full((nb, GP)), full((nb, GP)),
                  full((1, GP)), full((1, GP)), full((1, GP)), full((1, GP)),
                  full((W, GP)), full((W, GP)), full((GP, W)), full((GP, W)),
                  full((1, W)), full((W, W)), full((1, W))],
        out_specs=[pl.BlockSpec((rows, W), lambda c: (c, 0)), full((nb, GP)), full((nb, GP))],
        out_shape=[jax.ShapeDtypeStruct((M, W), F32),
                   jax.ShapeDtypeStruct((nb, GP), F32),
                   jax.ShapeDtypeStruct((nb, GP), F32)],
        scratch_shapes=[pltpu.VMEM((rows, GP), F32), pltpu.VMEM((rows, GP), F32)],
        compiler_params=_params("arbitrary"),
        name="s5",
    )(u_tb, s0_re, s0_im, ab_re, ab_im, co_re, co_im, wbr, wbi, wcr, wci,
      d.reshape(1, W), w_glu.astype(BF16), b_glu.reshape(1, W))


def _rope_tables(pos, width):
    half = MLA_ROPE // 2
    inv = ROPE_BASE ** (-jnp.arange(half, dtype=F32) / half)
    ang = pos.astype(F32)[:, None] * inv[None, :]
    cos, sin = jnp.cos(ang), jnp.sin(ang)
    pad = jnp.zeros((pos.shape[0], width - MLA_ROPE), F32)
    return (jnp.concatenate([cos, cos, pad], axis=1), jnp.concatenate([-sin, sin, pad], axis=1))


def _swap_halves(w):
    half = w.shape[-1] // 2
    return jnp.concatenate([w[..., half:], w[..., :half]], axis=-1)


def _pad_cols(w, width):
    return jnp.pad(w, [(0, 0)] * (w.ndim - 1) + [(0, width - w.shape[-1])])


def _mla_prep_kernel(cq_ref, ckv_ref, kr_ref, krs_ref, c2_ref, s2_ref, gq_ref, gkv_ref,
                     wq_ref, wuk_ref, q_ref, kcat_ref, ckvn_ref, krot_ref):
    H = q_ref.shape[0]
    c2, s2 = c2_ref[...], s2_ref[...]
    cqn = (_rms(cq_ref[...]) * gq_ref[...]).astype(BF16)
    qall = _dot(cqn, wq_ref[...])
    for h in range(H):
        base = h * 384
        nope = qall[:, base:base + 128].astype(BF16)
        rot = qall[:, base + 128:base + 256] * c2 + qall[:, base + 256:base + 384] * s2
        q_ref[h, :, 0:MLA_KV_RANK] = _dot(nope, wuk_ref[h]).astype(BF16)
        q_ref[h, :, MLA_KV_RANK:MLA_KPAD] = rot.astype(BF16)
    ckvn = _rms(ckv_ref[...]) * gkv_ref[...]
    krot = kr_ref[...] * c2 + krs_ref[...] * s2
    ckvn_ref[...] = ckvn
    krot_ref[...] = krot[:, :MLA_ROPE]
    kcat_ref[:, 0:MLA_KV_RANK] = ckvn.astype(BF16)
    kcat_ref[:, MLA_KV_RANK:MLA_KPAD] = krot.astype(BF16)


def _mla_prep(cq, ckv, kr, krs, c2, s2, g_q, g_kv, wq, wuk, tm):
    M = cq.shape[0]
    H = MLA_HEADS
    col = lambda w: pl.BlockSpec((tm, w), lambda i: (i, 0))
    full = lambda shape: pl.BlockSpec(shape, lambda i: (0,) * len(shape))
    return pl.pallas_call(
        _mla_prep_kernel,
        grid=(M // tm,),
        in_specs=[col(384), col(256), col(128), col(128), col(128), col(128),
                  full((1, 384)), full((1, 256)), full((384, H * 384)), full((H, 128, 256))],
        out_specs=[pl.BlockSpec((H, tm, MLA_KPAD), lambda i: (0, i, 0)),
                   col(MLA_KPAD), col(MLA_KV_RANK), col(MLA_ROPE)],
        out_shape=[jax.ShapeDtypeStruct((H, M, MLA_KPAD), BF16),
                   jax.ShapeDtypeStruct((M, MLA_KPAD), BF16),
                   jax.ShapeDtypeStruct((M, MLA_KV_RANK), F32),
                   jax.ShapeDtypeStruct((M, MLA_ROPE), F32)],
        compiler_params=_params("arbitrary"),
        name="mla_prep",
    )(cq, ckv, kr, krs, c2, s2, g_q.reshape(1, -1), g_kv.reshape(1, -1), wq, wuk)


def _mla_weights(w_uq, w_uk):
    C, H, _ = w_uq.shape
    nope = _pad_cols(w_uq[..., :MLA_NOPE], 128)
    rope = w_uq[..., MLA_NOPE:]
    wq = jnp.concatenate([nope, _pad_cols(rope, 128), _pad_cols(_swap_halves(rope), 128)], axis=-1)
    wuk = jnp.pad(jnp.transpose(w_uk, (1, 2, 0)), ((0, 0), (0, 128 - MLA_NOPE), (0, 0)))
    return wq.reshape(C, H * 384).astype(BF16), wuk.astype(BF16)


def _mla_attn_kernel(q_ref, k_ref, wuv_ref, o_ref, m_s, l_s, acc_s, *, tq, tk, scale):
    qi = pl.program_id(1)
    H = q_ref.shape[0]
    rows = H * tq
    q = q_ref[...].reshape(rows, MLA_KPAD)
    m_s[...] = jnp.full((rows, 1), NEG_BIG, F32)
    l_s[...] = jnp.zeros((rows, 1), F32)
    acc_s[...] = jnp.zeros((rows, MLA_KV_RANK), F32)
    row_t = qi * tq + lax.broadcasted_iota(jnp.int32, (H, tq, 1), 1).reshape(rows, 1)
    groups = [slice(g * rows // 2, (g + 1) * rows // 2) for g in range(2)]

    def block(kb, masked):
        k = k_ref[pl.ds(pl.multiple_of(kb * tk, tk), tk), :]
        vals = k[:, :MLA_KV_RANK]
        s = [_dot_nt(q[g], k) * scale for g in groups]
        if masked:
            kpos = kb * tk + lax.broadcasted_iota(jnp.int32, (1, tk), 1)
            s = [jnp.where(kpos <= row_t[g], sg, NEG_BIG) for g, sg in zip(groups, s)]
        m_prev = [m_s[g] for g in groups]
        m_new = [jnp.maximum(mp, jnp.max(sg, axis=-1, keepdims=True)) for mp, sg in zip(m_prev, s)]
        alpha = [jnp.exp(mp - mn) for mp, mn in zip(m_prev, m_new)]
        p = [jnp.exp(sg - mn) for sg, mn in zip(s, m_new)]
        pv = [_dot(pg.astype(BF16), vals) for pg in p]
        for g, al, pg, pvg, mn in zip(groups, alpha, p, pv, m_new):
            l_s[g] = al * l_s[g] + jnp.sum(pg, axis=-1, keepdims=True)
            acc_s[g] = al * acc_s[g] + pvg
            m_s[g] = mn

    n_full = (qi * tq) // tk

    def body(kb, carry):
        block(kb, False)
        return carry

    lax.fori_loop(0, n_full, body, 0)
    block(n_full, True)
    o = (acc_s[...] / l_s[...]).astype(BF16)
    for h in range(H):
        o_ref[:, h * MLA_V:(h + 1) * MLA_V] = _dot(o[h * tq:(h + 1) * tq], wuv_ref[h])


def _mla_attn(q, kcat, wuv, B, T, tq, tk):
    H = q.shape[0]
    nq = T // tq
    scale = (MLA_NOPE + MLA_ROPE) ** -0.5
    return pl.pallas_call(
        functools.partial(_mla_attn_kernel, tq=tq, tk=tk, scale=scale),
        grid=(B, nq),
        in_specs=[pl.BlockSpec((H, tq, MLA_KPAD), lambda b, i: (0, b * nq + i, 0)),
                  pl.BlockSpec((T, MLA_KPAD), lambda b, i: (b, 0)),
                  pl.BlockSpec((H, MLA_KV_RANK, MLA_V), lambda b, i: (0, 0, 0))],
        out_specs=pl.BlockSpec((tq, H * MLA_V), lambda b, i: (b * nq + i, 0)),
        out_shape=jax.ShapeDtypeStruct((B * T, H * MLA_V), F32),
        scratch_shapes=[pltpu.VMEM((H * tq, 1), F32), pltpu.VMEM((H * tq, 1), F32),
                        pltpu.VMEM((H * tq, MLA_KV_RANK), F32)],
        compiler_params=_params("arbitrary", "arbitrary"),
        name="mla_attn",
    )(q, kcat, wuv)


def _mla_decode_kernel(pt_ref, q_ref, knew_ref, *refs, npg, nt, scale):
    ckv_refs = refs[:npg]
    kr_refs = refs[npg:2 * npg]
    o_ref, m_s, l_s, acc_s, kbuf, rbuf = refs[2 * npg:]
    j = pl.program_id(1)
    q = q_ref[...]
    rows = q.shape[0]

    def update(s, vals):
        m_prev = m_s[...]
        m_new = jnp.maximum(m_prev, jnp.max(s, axis=-1, keepdims=True))
        alpha = jnp.exp(m_prev - m_new)
        p = jnp.exp(s - m_new)
        l_s[...] = alpha * l_s[...] + jnp.sum(p, axis=-1, keepdims=True)
        acc_s[...] = alpha * acc_s[...] + _dot(p.astype(BF16), vals)
        m_s[...] = m_new

    @pl.when(j == 0)
    def _():
        m_s[...] = jnp.full((rows, 1), NEG_BIG, F32)
        l_s[...] = jnp.zeros((rows, 1), F32)
        acc_s[...] = jnp.zeros((rows, MLA_KV_RANK), F32)
        knew = knew_ref[...]
        s = _dot_nt(q, knew) * scale
        t_row = lax.broadcasted_iota(jnp.int32, (rows, 1), 0) % nt
        col = lax.broadcasted_iota(jnp.int32, (1, PAGE_SIZE), 1)
        update(jnp.where(col <= t_row, s, NEG_BIG), knew[:, :MLA_KV_RANK])

    for r in range(npg):
        rows_r = slice(r * PAGE_SIZE, (r + 1) * PAGE_SIZE)
        kbuf[rows_r, :] = ckv_refs[r][...].astype(BF16)
        rbuf[:, rows_r] = kr_refs[r][...].astype(BF16)
    lat = kbuf[...]
    s = _dot_nt(q[:, :MLA_KV_RANK], lat) + _dot(q[:, MLA_KV_RANK:MLA_KV_RANK + MLA_ROPE], rbuf[...])
    update(s * scale, lat)

    @pl.when(j == pl.num_programs(1) - 1)
    def _():
        o_ref[...] = acc_s[...] / l_s[...]


def _mla_decode(q, knew, cache_ckv, cache_kr, page_table, layer, nt, npg=16):
    B, rows, _ = q.shape
    n_pages = page_table.shape[1]
    page = lambda r, a, c: pl.BlockSpec(
        (None, None, a, c), lambda b, j, pt: (layer, pt[b, j * npg + r], 0, 0))
    grid_spec = pltpu.PrefetchScalarGridSpec(
        num_scalar_prefetch=1,
        grid=(B, n_pages // npg),
        in_specs=[pl.BlockSpec((None, rows, MLA_KPAD), lambda b, j, pt: (b, 0, 0)),
                  pl.BlockSpec((None, PAGE_SIZE, MLA_KPAD), lambda b, j, pt: (b, 0, 0))]
        + [page(r, PAGE_SIZE, MLA_KV_RANK) for r in range(npg)]
        + [page(r, MLA_ROPE, PAGE_SIZE) for r in range(npg)],
        out_specs=pl.BlockSpec((None, rows, MLA_KV_RANK), lambda b, j, pt: (b, 0, 0)),
        scratch_shapes=[pltpu.VMEM((rows, 1), F32), pltpu.VMEM((rows, 1), F32),
                        pltpu.VMEM((rows, MLA_KV_RANK), F32),
                        pltpu.VMEM((npg * PAGE_SIZE, MLA_KV_RANK), BF16),
                        pltpu.VMEM((MLA_ROPE, npg * PAGE_SIZE), BF16)])
    scale = (MLA_NOPE + MLA_ROPE) ** -0.5
    return pl.pallas_call(
        functools.partial(_mla_decode_kernel, npg=npg, nt=nt, scale=scale),
        grid_spec=grid_spec,
        out_shape=jax.ShapeDtypeStruct((B, rows, MLA_KV_RANK), F32),
        compiler_params=_params("arbitrary", "arbitrary"),
        name="mla_decode",
    )(page_table, q, knew, *([cache_ckv] * npg), *([cache_kr] * npg))


def _bmm_kernel(x_ref, w_ref, o_ref):
    o_ref[...] = _dot(x_ref[...].astype(BF16), w_ref[...])


def _bmm(x, w):
    H, M, K = x.shape
    N = w.shape[2]
    return pl.pallas_call(
        _bmm_kernel,
        grid=(H,),
        in_specs=[pl.BlockSpec((None, M, K), lambda h: (h, 0, 0)),
                  pl.BlockSpec((None, K, N), lambda h: (h, 0, 0))],
        out_specs=pl.BlockSpec((None, M, N), lambda h: (h, 0, 0)),
        out_shape=jax.ShapeDtypeStruct((H, M, N), F32),
        compiler_params=_params("arbitrary"),
        name="bmm",
    )(x, w)


def _ab_in_weights(w_in, s5_width):
    a, b, c = s5_width, s5_width + 384, s5_width + 384 + MLA_KV_RANK
    u, cq, ckv, kr = w_in[:, :a], w_in[:, a:b], w_in[:, b:c], w_in[:, c:]
    return jnp.concatenate(
        [cq, u, ckv, _pad_cols(kr, 128), _pad_cols(_swap_halves(kr), 128)], axis=1).astype(BF16)


def _ab_layer(x, mods, g_norm, pos, B, T, tm, cache, wts, tq=128, tk=512):
    sh1, sc1, g1 = mods
    M = B * T
    W = wts['s5_d'].size
    GP = wts['s5_consts'][0].size
    cq, u, ckv, kr, krs = _norm_mod_mm(x, g_norm, sc1, sh1, wts['w_in'], T, tm,
                                       (384, W, MLA_KV_RANK, 128, 128))
    u_tb = jnp.transpose(u.reshape(B, T, W), (1, 0, 2)).reshape(M, W)
    if cache is None:
        s0_re = jnp.zeros((B, GP), F32)
        s0_im = jnp.zeros((B, GP), F32)
    else:
        s0_re, s0_im = cache['s5_re'].reshape(B, GP), cache['s5_im'].reshape(B, GP)
    s5_tb, s_re, s_im = _s5(u_tb, s0_re, s0_im, wts['s5_consts'], wts['s5_d'], wts['s5_w_glu'],
                            wts['s5_b_glu'], nb=B, tc=min(T, 64))
    s5_out = jnp.transpose(s5_tb.reshape(T, B, W), (1, 0, 2)).reshape(M, W)
    c2, s2 = _rope_tables(pos, 128)
    c2, s2 = jnp.tile(c2, (B, 1)), jnp.tile(s2, (B, 1))
    q, kcat, ckvn, krot = _mla_prep(cq, ckv, kr, krs, c2, s2, wts['mla_g_q'], wts['mla_g_kv'],
                                    wts['mla_wq'], wts['mla_wuk'], tm)
    H = MLA_HEADS
    if cache is None:
        o = _mla_attn(q, kcat, wts['mla_wuv'], B, T, tq, min(tk, T))
    else:
        qd = jnp.transpose(q.reshape(H, B, T, MLA_KPAD), (1, 0, 2, 3)).reshape(B, H * T, MLA_KPAD)
        knew = jnp.pad(kcat.reshape(B, T, MLA_KPAD), ((0, 0), (0, PAGE_SIZE - T), (0, 0)))
        o_lat = _mla_decode(qd, knew, cache['ckv'], jnp.transpose(cache['kr'], (0, 1, 3, 2)),
                            cache['page_table'],
                            cache['layer'], T)
        o_lat = jnp.transpose(o_lat.reshape(B, H, T, MLA_KV_RANK), (1, 0, 2, 3)).reshape(H, M, -1)
        o = jnp.transpose(_bmm(o_lat, wts['mla_wuv']), (1, 0, 2)).reshape(M, H * MLA_V)
    mix = jnp.concatenate([s5_out, o], axis=1)
    x = _mm_resid(mix, wts['w_out'], x, g1, T, tm)
    return x, ckvn, krot, s_re, s_im


def _suffix_sums(ln, tri):
    hi = ln.astype(BF16)
    mid = (ln - hi.astype(F32)).astype(BF16)
    return _dot(hi, tri) + _dot(mid, tri)


def _sb_attn_kernel(q_ref, k_ref, v_ref, tri_ref, o_ref, acc_s, r_s, *, tq, scale):
    qi = pl.program_id(2)
    hs = range(q_ref.shape[0])
    tri = tri_ref[...]
    acc_s[...] = jnp.zeros(acc_s.shape, F32)
    r_s[...] = jnp.zeros(r_s.shape, F32)

    def block(kb, diag):
        rows = pl.ds(pl.multiple_of(kb * tq, tq), tq)
        z = [_dot_nt(q_ref[h], k_ref[h, rows, :]) * scale for h in hs]
        sp = [_softplus(a) for a in z]
        if diag:
            mask = (lax.broadcasted_iota(jnp.int32, (tq, tq), 1)
                    < lax.broadcasted_iota(jnp.int32, (tq, tq), 0))
            ln = [jnp.where(mask, -a, 0.0) for a in sp]
        else:
            ln = [-a for a in sp]
        a = [jnp.exp(z[h] - sp[h] + _suffix_sums(ln[h], tri) + r_s[h]) for h in hs]
        if diag:
            a = [jnp.where(mask, ah, 0.0) for ah in a]
        for h in hs:
            acc_s[h] += _dot(a[h].astype(BF16), v_ref[h, rows, :])
            r_s[h] += jnp.sum(ln[h], axis=-1, keepdims=True)

    block(qi, True)

    def body(i, carry):
        block(qi - 1 - i, False)
        return carry

    lax.fori_loop(0, qi, body, 0)
    o_ref[...] = acc_s[...]


def _tri(n):
    return (lax.broadcasted_iota(jnp.int32, (n, n), 0)
            > lax.broadcasted_iota(jnp.int32, (n, n), 1)).astype(BF16)


def _sb_attn(q, k, v, tq):
    B, H, T, d = q.shape
    hb = 2
    seq = pl.BlockSpec((None, hb, T, d), lambda b, h, i: (b, h, 0, 0))
    blk = pl.BlockSpec((None, hb, tq, d), lambda b, h, i: (b, h, i, 0))
    return pl.pallas_call(
        functools.partial(_sb_attn_kernel, tq=tq, scale=d ** -0.5),
        grid=(B, H // hb, T // tq),
        in_specs=[blk, seq, seq, pl.BlockSpec((tq, tq), lambda b, h, i: (0, 0))],
        out_specs=blk,
        out_shape=jax.ShapeDtypeStruct((B, H, T, d), F32),
        scratch_shapes=[pltpu.VMEM((hb, tq, d), F32), pltpu.VMEM((hb, tq, 1), F32)],
        compiler_params=_params("arbitrary", "arbitrary", "arbitrary"),
        name="sb_attn",
    )(q, k, v, _tri(tq))


def _sb_decode_kernel(pt_ref, q_ref, knew_ref, vnew_ref, tw_ref, *refs, npg, scale):
    k_refs = refs[:npg]
    v_refs = refs[npg:2 * npg]
    o_ref, acc_s, r_s, kbuf, vbuf = refs[2 * npg:]
    j = pl.program_id(1)
    H, qrows, d = q_ref.shape
    hs = range(H)
    tw = tw_ref[...]

    def block(keys, vals, n, fresh):
        cols = n * PAGE_SIZE
        z = jnp.concatenate([_dot(q_ref[h], keys[h]) for h in hs], axis=0) * scale
        sp = _softplus(z)
        if fresh:
            t_row = lax.broadcasted_iota(jnp.int32, (H * qrows, cols), 0) % qrows
            mask = lax.broadcasted_iota(jnp.int32, (H * qrows, cols), 1) < t_row
            ln = jnp.where(mask, -sp, 0.0)
        else:
            ln = -sp
        hi = ln.astype(BF16)
        mid = (ln - hi.astype(F32)).astype(BF16)
        run = r_s[...]
        after = []
        for p in range(n):
            sl = slice(p * PAGE_SIZE, (p + 1) * PAGE_SIZE)
            both = _dot(hi[:, sl], tw) + _dot(mid[:, sl], tw)
            after.append(both[:, :LANES] + run)
            run = run + both[:, LANES:]
        a = jnp.exp(z - sp + jnp.concatenate(after, axis=1))
        if fresh:
            a = jnp.where(mask, a, 0.0)
        a = a.astype(BF16)
        for h in hs:
            acc_s[h] += _dot_nt(a[h * qrows:(h + 1) * qrows], vals[h])
        r_s[...] = run

    @pl.when(j == 0)
    def _():
        acc_s[...] = jnp.zeros(acc_s.shape, F32)
        r_s[...] = jnp.zeros(r_s.shape, F32)
        block(knew_ref, vnew_ref, 1, True)

    for r in range(npg):
        lanes_r = slice(r * PAGE_SIZE, (r + 1) * PAGE_SIZE)
        kbuf[:, :, lanes_r] = k_refs[r][...].astype(BF16)
        vbuf[:, :, lanes_r] = v_refs[r][...].astype(BF16)
    block(kbuf, vbuf, npg, False)

    @pl.when(j == pl.num_programs(1) - 1)
    def _():
        o_ref[...] = acc_s[...]


QROWS = 8


def _sb_decode(q, knew, vnew, cache_kt, cache_vt, page_table, layer, npg=8):
    B, H, qrows, d = q.shape
    assert PAGE_SIZE == LANES
    n_pages = page_table.shape[1]
    page = lambda r: pl.BlockSpec(
        (None, None, H, d, PAGE_SIZE),
        lambda b, j, pt: (layer, pt[b, n_pages - 1 - (j * npg + r)], 0, 0, 0))
    per_b = lambda a, c: pl.BlockSpec((None, H, a, c), lambda b, j, pt: (b, 0, 0, 0))
    ti = lax.broadcasted_iota(jnp.int32, (LANES, 2 * LANES), 0)
    si = lax.broadcasted_iota(jnp.int32, (LANES, 2 * LANES), 1)
    tw = ((ti > si) | (si >= LANES)).astype(BF16)
    grid_spec = pltpu.PrefetchScalarGridSpec(
        num_scalar_prefetch=1,
        grid=(B, n_pages // npg),
        in_specs=[per_b(qrows, d), per_b(d, PAGE_SIZE), per_b(d, PAGE_SIZE),
                  pl.BlockSpec((LANES, 2 * LANES), lambda b, j, pt: (0, 0))]
        + [page(r) for r in range(npg)] * 2,
        out_specs=per_b(qrows, d),
        scratch_shapes=[pltpu.VMEM((H, qrows, d), F32), pltpu.VMEM((H * qrows, LANES), F32),
                        pltpu.VMEM((H, d, npg * PAGE_SIZE), BF16),
                        pltpu.VMEM((H, d, npg * PAGE_SIZE), BF16)])
    return pl.pallas_call(
        functools.partial(_sb_decode_kernel, npg=npg, scale=d ** -0.5),
        grid_spec=grid_spec,
        out_shape=jax.ShapeDtypeStruct((B, H, qrows, d), F32),
        compiler_params=_params("arbitrary", "arbitrary"),
        name="sb_decode",
    )(page_table, q, knew, vnew, tw, *([cache_kt] * npg), *([cache_vt] * npg))


def _rwkv_prep_kernel(p_ref, prev_ref, mu_ref, w0_ref, w2_ref, a0_ref, a2_ref, g2_ref,
                      kkw_ref, ka_ref, ones_ref, r_o, lw_o, k_o, v_o, kk_o, b_o, g_o):
    H, _, N = r_o.shape
    Wd = H * N
    p = p_ref[...]
    ps = p + (prev_ref[...] - p) * mu_ref[...]
    r, k, v = ps[:, :Wd], ps[:, Wd:2 * Wd], ps[:, 2 * Wd:3 * Wd]
    o = 3 * Wd
    xw = ps[:, o:o + RW_DECAY_RANK]
    xa = ps[:, o + RW_DECAY_RANK:o + RW_DECAY_RANK + RW_A_RANK]
    xg = ps[:, o + RW_DECAY_RANK + RW_A_RANK:]
    w_log = -_softplus(-(w0_ref[...] + _dot(jnp.tanh(xw).astype(BF16), w2_ref[...]))) - 0.5
    lw = -jnp.exp(w_log)
    a = _sigmoid(a0_ref[...] + _dot(xa.astype(BF16), a2_ref[...]))
    g = _dot(_sigmoid(xg).astype(BF16), g2_ref[...])
    kk = k * kkw_ref[...]
    ss = _dot_exact_rhs(kk * kk, ones_ref[...])
    kk = kk / jnp.maximum(jnp.sqrt(ss), 1e-12)
    k2 = k * (1.0 + (a - 1.0) * ka_ref[...])
    b = kk * a
    for h in range(H):
        sl = slice(h * N, (h + 1) * N)
        r_o[h] = r[:, sl]
        lw_o[h] = lw[:, sl]
        k_o[h] = k2[:, sl]
        v_o[h] = v[:, sl]
        kk_o[h] = kk[:, sl]
        b_o[h] = b[:, sl]
        g_o[h] = g[:, sl]


def _rwkv_prep(p, prev, wts, tm):
    M, cols = p.shape
    H, N = RW_HEADS, RW_HEAD_DIM
    Wd = H * N
    row = lambda n: pl.BlockSpec((1, n), lambda i: (0, 0))
    mat = lambda a, b: pl.BlockSpec((a, b), lambda i: (0, 0))
    out = pl.BlockSpec((H, tm, N), lambda i: (0, i, 0))
    return pl.pallas_call(
        _rwkv_prep_kernel,
        grid=(M // tm,),
        in_specs=[pl.BlockSpec((tm, cols), lambda i: (i, 0)), pl.BlockSpec((tm, cols), lambda i: (i, 0)),
                  row(cols), row(Wd), mat(RW_DECAY_RANK, Wd), row(Wd), mat(RW_A_RANK, Wd),
                  mat(RW_GATE_RANK, Wd), row(Wd), row(Wd), mat(Wd, Wd)],
        out_specs=[out] * 7,
        out_shape=[jax.ShapeDtypeStruct((H, M, N), F32)] * 7,
        compiler_params=_params("arbitrary"),
        name="rwkv_prep",
    )(p, prev, wts['mu'], wts['w0'], wts['w2'], wts['a0'], wts['a2'], wts['g2'],
      wts['k_k'], wts['k_a'], wts['head_ones'])


def _dot3(dot, a, b):
    a0 = a.astype(BF16)
    a1 = (a - a0.astype(F32)).astype(BF16)
    b0 = b.astype(BF16)
    b1 = (b - b0.astype(F32)).astype(BF16)
    return dot(a0, b0) + (dot(a0, b1) + dot(a1, b0))


SOLVE_BLOCK = 16


def _unit_lower_solve(Ls, Rs, C):
    n = range(len(Ls))
    if C <= SOLVE_BLOCK:
        Us = list(Rs)
        for s in range(C - 1):
            Us = [Us[i] - Ls[i][:, s:s + 1] * Us[i][s:s + 1, :] for i in n]
        return Us
    nb = C // SOLVE_BLOCK
    assert nb * SOLVE_BLOCK == C and nb <= 4
    ti = lax.broadcasted_iota(jnp.int32, (C, C), 0)
    si = lax.broadcasted_iota(jnp.int32, (C, C), 1)
    same = (ti // SOLVE_BLOCK) == (si // SOLVE_BLOCK)
    rep_t = (lax.broadcasted_iota(jnp.int32, (C, SOLVE_BLOCK), 0) % SOLVE_BLOCK
             == lax.broadcasted_iota(jnp.int32, (C, SOLVE_BLOCK), 1))
    rep = (lax.broadcasted_iota(jnp.int32, (SOLVE_BLOCK, C), 1) % SOLVE_BLOCK
           == lax.broadcasted_iota(jnp.int32, (SOLVE_BLOCK, C), 0)).astype(BF16)
    Ld = [jnp.where(same, L, 0.0) for L in Ls]
    Ldc = [_dot_exact_rhs(a, rep_t.astype(BF16)) for a in Ld]
    X = [rep_t.astype(F32) for _ in n]
    for s in range(SOLVE_BLOCK - 1):
        rows = [jnp.broadcast_to(x.reshape(nb, SOLVE_BLOCK, SOLVE_BLOCK)[:, s:s + 1, :],
                                 (nb, SOLVE_BLOCK, SOLVE_BLOCK)).reshape(C, SOLVE_BLOCK) for x in X]
        X = [X[i] - Ldc[i][:, s:s + 1] * rows[i] for i in n]
    T1 = [jnp.where(same, _dot_exact_rhs(x, rep), 0.0) for x in X]
    R1 = [_dot3(_dot, T1[i], Rs[i]) for i in n]
    Mm = [_dot3(_dot, T1[i], Ls[i] - Ld[i]) for i in n]
    M2 = [_dot3(_dot, m, m) for m in Mm]
    Y1 = [R1[i] + _dot3(_dot, M2[i], R1[i]) for i in n]
    return [Y1[i] - _dot3(_dot, Mm[i], Y1[i]) for i in n]


def _rwkv_chunk_kernel(r_ref, lw_ref, k_ref, v_ref, kk_ref, b_ref, g_ref, s0_ref,
                       lnw_ref, lnb_ref, rk_ref, y_ref, s_ref, *, C):
    c = pl.program_id(2)

    @pl.when(c == 0)
    def _():
        s_ref[...] = s0_ref[...]

    hs = range(r_ref.shape[0])
    ti = lax.broadcasted_iota(jnp.int32, (C, C), 0)
    si = lax.broadcasted_iota(jnp.int32, (C, C), 1)
    strict, incl = si < ti, si <= ti
    tril = incl.astype(BF16)
    r, lw, k, v = ([ref[h] for h in hs] for ref in (r_ref, lw_ref, k_ref, v_ref))
    kk, b = ([ref[h] for h in hs] for ref in (kk_ref, b_ref))
    S0 = [s_ref[h] for h in hs]
    parts = [_split3(a) for a in lw]
    cum = [_dot(tril, p0) + _dot(tril, p1) + _dot(tril, p2) for p0, p1, p2 in parts]
    g_in = [jnp.exp(a) for a in cum]
    g_inv = [jnp.exp(-a) for a in cum]
    qt = [kk[h] * jnp.exp(cum[h] - lw[h]) for h in hs]
    bt = [b[h] * g_inv[h] for h in hs]
    kt = [k[h] * g_inv[h] for h in hs]
    rt = [r[h] * g_in[h] for h in hs]
    qr = [jnp.concatenate([qt[h], rt[h]], axis=0) for h in hs]
    bk = [jnp.concatenate([bt[h], kt[h]], axis=0) for h in hs]
    gram = [_dot3(_dot_nt, qr[h], bk[h]) for h in hs]
    Lb = [jnp.where(strict, g[:C, :C], 0.0) for g in gram]
    Lk = [jnp.where(strict, g[:C, C:], 0.0) for g in gram]
    Ab = [jnp.where(incl, g[C:, :C], 0.0) for g in gram]
    Ak = [jnp.where(incl, g[C:, C:], 0.0) for g in gram]
    on_state = [_dot3(_dot_nt, qr[h], S0[h]) for h in hs]
    on_v = [_dot3(_dot, jnp.concatenate([Lk[h], Ak[h]], axis=0), v[h]) for h in hs]
    U = _unit_lower_solve(Lb, [-(on_state[h][:C] + on_v[h][:C]) for h in hs], C)
    y = [on_state[h][C:] + _dot3(_dot, Ab[h], U[h]) + on_v[h][C:] for h in hs]
    for h in hs:
        grown = _dot3(_dot_tn, jnp.concatenate([U[h], v[h]], axis=0), bk[h])
        s_ref[h] = (S0[h] + grown) * g_in[h][C - 1:C, :]
    for h in hs:
        mean = jnp.mean(y[h], axis=-1, keepdims=True)
        var = jnp.mean(jnp.square(y[h] - mean), axis=-1, keepdims=True)
        yn = (y[h] - mean) * lax.rsqrt(var + RW_GN_EPS) * lnw_ref[h] + lnb_ref[h]
        bonus = jnp.sum(r[h] * k[h] * rk_ref[h], axis=-1, keepdims=True) * v[h]
        y_ref[h] = (yn + bonus) * g_ref[h]


def _rwkv_chunks(feats, s0, ln_w, ln_b, r_k, B, T, C, Hb=8):
    H, M, N = feats[0].shape
    nc = T // C
    seq = pl.BlockSpec((Hb, C, N), lambda b, hb, c: (hb, b * nc + c, 0))
    st = pl.BlockSpec((None, Hb, N, N), lambda b, hb, c: (b, hb, 0, 0))
    par = pl.BlockSpec((Hb, 1, N), lambda b, hb, c: (hb, 0, 0))
    return pl.pallas_call(
        functools.partial(_rwkv_chunk_kernel, C=C),
        grid=(B, H // Hb, nc),
        in_specs=[seq] * 7 + [st, par, par, par],
        out_specs=[seq, st],
        out_shape=[jax.ShapeDtypeStruct((H, M, N), F32), jax.ShapeDtypeStruct((B, H, N, N), F32)],
        compiler_params=_params("arbitrary", "arbitrary", "arbitrary"),
        name="rwkv_chunks",
    )(*feats, s0, ln_w.reshape(H, 1, N), ln_b.reshape(H, 1, N), r_k.reshape(H, 1, N))


def _cd_weights(w):
    H, N = RW_HEADS, RW_HEAD_DIM
    Wd = H * N
    row = lambda a: a.reshape(1, -1)
    head = jnp.arange(Wd) // N
    return {
        'w_in': w['cd_w_in'].astype(BF16), 'w_out': w['cd_w_out'].astype(BF16),
        'mu': row(w['rwkv_mu']), 'w0': row(w['rwkv_w0']), 'w2': w['rwkv_w2'].astype(BF16),
        'a0': row(w['rwkv_a0']), 'a2': w['rwkv_a2'].astype(BF16), 'g2': w['rwkv_g2'].astype(BF16),
        'k_k': row(w['rwkv_k_k']), 'k_a': row(w['rwkv_k_a']),
        'head_ones': (head[:, None] == head[None, :]).astype(BF16),
        'ln_w': w['rwkv_ln_w'], 'ln_b': w['rwkv_ln_b'], 'r_k': w['rwkv_r_k'],
    }


def _ab_weights(w):
    wq, wuk = _mla_weights(w['mla_w_uq'], w['mla_w_uk'])
    return {
        'w_in': _ab_in_weights(w['ab_w_in'], w['s5_d'].size), 'w_out': w['ab_w_out'].astype(BF16),
        's5_consts': _s5_consts(w['s5_lambda_re'], w['s5_lambda_im'], w['s5_log_dt'],
                                w['s5_b_re'], w['s5_b_im'], w['s5_c_re'], w['s5_c_im']),
        's5_d': w['s5_d'], 's5_w_glu': w['s5_w_glu'], 's5_b_glu': w['s5_b_glu'],
        'mla_g_q': w['mla_g_q'], 'mla_g_kv': w['mla_g_kv'], 'mla_wq': wq, 'mla_wuk': wuk,
        'mla_wuv': jnp.transpose(w['mla_w_uv'], (1, 0, 2)).astype(BF16),
    }


def _cd_layer(x, mods, g_norm, B, T, tm, cache, wts, tq=256):
    sh1, sc1, g1 = mods
    M = B * T
    H, d = SB_HEADS, SB_HEAD_DIM
    Wsb = H * d
    cols = wts['mu'].shape[1]
    q, k, v, rw = _norm_mod_mm(x, g_norm, sc1, sh1, wts['w_in'], T, tm, (Wsb, Wsb, Wsb, cols))
    if cache is None:
        heads = lambda a: jnp.transpose(a.reshape(B, T, H, d), (0, 2, 1, 3)).astype(BF16)
        sb = _sb_attn(heads(q), heads(k), heads(v), tq)
        sb = jnp.transpose(sb, (0, 2, 1, 3)).reshape(M, Wsb)
    else:
        keys_minor = lambda a: jnp.transpose(a, (0, 1, 3, 4, 2))
        heads = lambda a: jnp.transpose(a.reshape(B, T, H, d), (0, 2, 1, 3))
        qh = jnp.pad(heads(q), ((0, 0), (0, 0), (0, QROWS - T), (0, 0))).astype(BF16)
        fresh = lambda a: jnp.pad(jnp.transpose(heads(a), (0, 1, 3, 2)),
                                  ((0, 0), (0, 0), (0, 0), (0, PAGE_SIZE - T))).astype(BF16)
        sb = _sb_decode(qh, fresh(k), fresh(v), keys_minor(cache['sb_k']),
                        keys_minor(cache['sb_v']), cache['page_table'], cache['layer'])
        sb = jnp.transpose(sb[:, :, :T], (0, 2, 1, 3)).reshape(M, Wsb)
    rw3 = rw.reshape(B, T, cols)
    shift0 = jnp.zeros((B, cols), F32) if cache is None else cache['shift']
    prev = jnp.concatenate([shift0[:, None, :], rw3[:, :-1]], axis=1).reshape(M, cols)
    feats = _rwkv_prep(rw, prev, wts, tm)
    N = RW_HEAD_DIM
    C = min(T, 64)
    if T < 8:
        C = 8
        padt = lambda a: jnp.pad(a.reshape(RW_HEADS, B, T, N),
                                 ((0, 0), (0, 0), (0, C - T), (0, 0))).reshape(RW_HEADS, B * C, N)
        feats = [padt(f) for f in feats]
    s0 = (jnp.zeros((B, RW_HEADS, N, N), F32) if cache is None else cache['wkv'])
    Tp = max(T, C)
    y, wkv = _rwkv_chunks(feats, s0, wts['ln_w'], wts['ln_b'], wts['r_k'], B, Tp, C)
    y = y.reshape(RW_HEADS, B, Tp, N)[:, :, :T]
    rw_out = jnp.transpose(y, (1, 2, 0, 3)).reshape(M, RW_HEADS * N)
    mix = jnp.concatenate([sb, rw_out], axis=1)
    x = _mm_resid(mix, wts['w_out'], x, g1, T, tm)
    return x, k, v, wkv, rw3[:, -1]


LANES = 128


def _top_values(s, n):
    out = []
    rank = jnp.full(s.shape, float(n), F32)
    for i in range(n):
        m = jnp.max(s, axis=0, keepdims=True)
        out.append(m)
        hit = s == m
        rank = jnp.where(hit, float(i), rank)
        s = jnp.where(hit, NEG_BIG, s)
    return out, rank


def _count_ge(s, v, tau):
    assert len(v) == 16
    ge = lambda row: (s + row) >= tau
    t8 = ge(v[7])
    t4 = ge(jnp.where(t8, v[11], v[3]))
    t2 = ge(jnp.where(t8, jnp.where(t4, v[13], v[9]), jnp.where(t4, v[5], v[1])))
    hi = jnp.where(t4, jnp.where(t2, v[14], v[12]), jnp.where(t2, v[10], v[8]))
    lo = jnp.where(t4, jnp.where(t2, v[6], v[4]), jnp.where(t2, v[2], v[0]))
    t1 = ge(jnp.where(t8, hi, lo))
    one = lambda t, n: jnp.where(t, float(n), 0.0)
    return one(t8, 8) + one(t4, 4) + one(t2, 2) + one(t1, 1) + one(ge(v[15]), 1)


def _peer_route_kernel(x_ref, g_ref, sc_ref, sh_ref, wq_ref, k1_ref, k2_ref,
                       h_ref, e1_ref, j_ref, r2_ref, e2_ref, s1_s, s2_s):
    H, NK, tm = s1_s.shape
    half = k1_ref.shape[2]
    K = PEER_TOPK
    hf = _rms(x_ref[...]) * g_ref[...] * (1.0 + sc_ref[...]) + sh_ref[...]
    h = hf.astype(BF16)
    h_ref[...] = hf.T.astype(BF16)
    q = _dot(h, wq_ref[...]).astype(BF16)
    for hd in range(H):
        base = hd * 2 * half
        s1_s[hd] = _dot_nt(k1_ref[hd], q[:, base:base + half])
        s2_s[hd] = _dot_nt(k2_ref[hd], q[:, base + half:base + 2 * half])
    row16 = lax.broadcasted_iota(jnp.int32, (K, LANES), 0)
    row8 = lax.broadcasted_iota(jnp.int32, (8, LANES), 0)

    def lane_group(c, carry):
        lanes = pl.ds(pl.multiple_of(c * LANES, LANES), LANES)
        for hd in range(H):
            s1 = s1_s[hd, :, lanes]
            s2 = s2_s[hd, :, lanes]
            v1, _ = _top_values(s1, K)
            v2, rank = _top_values(s2, K)
            v2t = jnp.zeros((K, LANES), F32)
            for i in range(K):
                v2t = jnp.where(row16 == i, v2[i], v2t)
            cands = [v1[0] + v2t]
            for i in range(1, K):
                cands.append(jnp.where(row8 < K // (i + 1), v1[i] + v2t[:8], NEG_BIG))
            best = []
            for _ in range(K):
                m8 = jnp.maximum(cands[0][:8], cands[0][8:])
                for cnd in cands[1:]:
                    m8 = jnp.maximum(m8, cnd)
                m = jnp.max(m8, axis=0, keepdims=True)
                best.append(m)
                cands = [jnp.where(cnd == m, NEG_BIG, cnd) for cnd in cands]
            tau = best[K - 1]
            z = jnp.ones((1, LANES), F32)
            for n in range(1, K):
                z = z + jnp.exp(best[n] - best[0])
            e1_ref[hd, :, lanes] = jnp.exp(s1 - v1[0]) / z
            j_ref[hd, :, lanes] = _count_ge(s1, v2, tau)
            r2_ref[hd, :, lanes] = rank.astype(BF16)
            e2_ref[hd, :, lanes] = jnp.exp(s2 - v2[0]).astype(BF16)
        return carry

    lax.fori_loop(0, tm // LANES, lane_group, 0)


def _peer_route(x, g, sc, sh, wq, k1, k2, T, tm):
    M, D = x.shape
    H, NK, half = k1.shape
    nt = M // tm
    tab = pl.BlockSpec((None, H, NK, tm), lambda i: (i, 0, 0, 0))
    return pl.pallas_call(
        _peer_route_kernel,
        grid=(M // tm,),
        in_specs=[pl.BlockSpec((tm, D), lambda i: (i, 0)),
                  pl.BlockSpec((1, D), lambda i: (0, 0)),
                  _mod_spec(T, tm, D), _mod_spec(T, tm, D),
                  pl.BlockSpec((D, wq.shape[1]), lambda i: (0, 0)),
                  pl.BlockSpec((H, NK, half), lambda i: (0, 0, 0)),
                  pl.BlockSpec((H, NK, half), lambda i: (0, 0, 0))],
        out_specs=[pl.BlockSpec((None, D, tm), lambda i: (i, 0, 0)), tab, tab, tab, tab],
        out_shape=[jax.ShapeDtypeStruct((nt, D, tm), BF16)]
        + [jax.ShapeDtypeStruct((nt, H, NK, tm), dt) for dt in (F32, F32, BF16, BF16)],
        scratch_shapes=[pltpu.VMEM((H, NK, tm), F32), pltpu.VMEM((H, NK, tm), F32)],
        compiler_params=_params("arbitrary"),
        name="peer_route",
    )(x, g.reshape(1, D), _mod_rows(sc, T, tm), _mod_rows(sh, T, tm), wq, k1, k2)


def _peer_expert_kernel(h_ref, u_ref, vt_ref, e1_ref, j_ref, r2_ref, e2_ref, x_ref, gate_ref,
                        o_ref, acc_s, act_s, g_s, *, A):
    j = pl.program_id(1)
    H, NK, tm = r2_ref.shape

    @pl.when(j == 0)
    def _():
        acc_s[...] = jnp.zeros(acc_s.shape, F32)

    act_s[...] = _dot(u_ref[...], h_ref[...])
    GA = 4
    zero = jnp.zeros((NK, LANES), BF16)
    for c in range(tm // LANES):
        lanes = slice(c * LANES, (c + 1) * LANES)
        for a0 in range(0, A, GA):
            ws = [zero] * GA
            for hd in range(H):
                r2 = r2_ref[hd, :, lanes]
                e2 = e2_ref[hd, :, lanes]
                for i in range(GA):
                    al = a0 + i
                    e1 = e1_ref[hd, al:al + 1, lanes].astype(BF16)
                    cnt = j_ref[hd, al:al + 1, lanes].astype(BF16)
                    ws[i] = ws[i] + jnp.where(r2 < cnt, e2, zero) * e1
            for i in range(GA):
                rows = slice((a0 + i) * NK, (a0 + i + 1) * NK)
                g_s[rows, lanes] = ws[i] * _gelu(act_s[rows, lanes]).astype(BF16)
    acc_s[...] += _dot(vt_ref[...], g_s[...])

    @pl.when(j == pl.num_programs(1) - 1)
    def _():
        o_ref[...] = x_ref[...] + gate_ref[...] * acc_s[...].T


EXPERT_TILE = 1024


def _peer_experts(h, u, vt, tabs, x, gate, T, tm, te=EXPERT_TILE):
    M, D = x.shape
    E = u.shape[0]
    _, H, NK, _ = tabs[0].shape
    A = te // NK
    assert vt.shape == (E // te, D, te)
    tab = pl.BlockSpec((None, H, NK, tm), lambda i, j: (i, 0, 0, 0))
    tab_a = pl.BlockSpec((None, H, A, tm), lambda i, j: (i, 0, j, 0))
    return pl.pallas_call(
        functools.partial(_peer_expert_kernel, A=A),
        grid=(M // tm, E // te),
        in_specs=[pl.BlockSpec((None, D, tm), lambda i, j: (i, 0, 0)),
                  pl.BlockSpec((te, D), lambda i, j: (j, 0)),
                  pl.BlockSpec((None, D, te), lambda i, j: (j, 0, 0)),
                  tab_a, tab_a, tab, tab,
                  pl.BlockSpec((tm, D), lambda i, j: (i, 0)),
                  _mod_spec2(T, tm, D)],
        out_specs=pl.BlockSpec((tm, D), lambda i, j: (i, 0)),
        out_shape=jax.ShapeDtypeStruct((M, D), F32),
        scratch_shapes=[pltpu.VMEM((D, tm), F32), pltpu.VMEM((te, tm), F32),
                        pltpu.VMEM((te, tm), BF16)],
        compiler_params=_params("arbitrary", "arbitrary"),
        name="peer_experts",
    )(h, u, vt, *tabs, x, _mod_rows(gate, T, tm))


def _mod_spec2(T, tm, D):
    if tm <= T:
        return pl.BlockSpec((None, 1, D), lambda i, j: (i * tm // T, 0, 0))
    return pl.BlockSpec((None, tm, D), lambda i, j: (0, i, 0))


ROW_TILE = 512


def _trunk(x, c, w, ab_wts, cd_wts, peer_wts, cache):
    B, T, D = x.shape
    M = B * T
    tm = min(ROW_TILE, M)
    depth = w['ada_w'].shape[0]
    past_len = 0 if cache is None else cache['page_table'].shape[1] * PAGE_SIZE
    pos = past_len + jnp.arange(T, dtype=jnp.int32)
    mod = _ada(c, w['ada_w'], w['ada_b'])
    xf = x.reshape(M, D)
    ckv_rows, kr_rows, s5r, s5i, k_rows, v_rows, wkvs, shifts = [], [], [], [], [], [], [], []
    for l in range(depth):
        i = l // 2
        sh1, sc1, g1, sh2, sc2, g2 = jnp.split(mod[l], 6, axis=-1)
        if l % 2 == 0:
            lc = None if cache is None else dict(
                ckv=cache['mla_ckv'], kr=cache['mla_krope'], page_table=cache['page_table'], layer=i,
                s5_re=cache['s5_re'][i], s5_im=cache['s5_im'][i])
            xf, ckv, kr, sr, si = _ab_layer(xf, (sh1, sc1, g1), w['norm1_g'][l], pos, B, T, tm, lc,
                                            ab_wts[i])
            ckv_rows.append(ckv.reshape(B, T, -1))
            kr_rows.append(kr.reshape(B, T, -1))
            state_shape = (B,) + w['s5_lambda_re'].shape[1:]
            s5r.append(sr.reshape(state_shape))
            s5i.append(si.reshape(state_shape))
        else:
            lc = None if cache is None else dict(
                sb_k=cache['sb_k'], sb_v=cache['sb_v'], page_table=cache['page_table'], layer=i,
                wkv=cache['wkv'][i], shift=cache['shift'][i])
            xf, k, v, wkv, shift = _cd_layer(xf, (sh1, sc1, g1), w['norm1_g'][l], B, T, tm, lc,
                                             cd_wts[i])
            k_rows.append(k.reshape(B, T, SB_HEADS, SB_HEAD_DIM))
            v_rows.append(v.reshape(B, T, SB_HEADS, SB_HEAD_DIM))
            wkvs.append(wkv)
            shifts.append(shift)
        pw = peer_wts[l]
        h2, *tabs = _peer_route(xf, w['norm2_g'][l], sc2, sh2, pw['wq'], pw['k1'], pw['k2'], T, tm)
        xf = _peer_experts(h2, pw['u'], pw['vt'], tabs, xf, g2, T, tm)
    y = _rmsnorm(xf, w['final_g'], tm).reshape(B, T, D)
    return (y, jnp.stack(ckv_rows), jnp.stack(kr_rows), jnp.stack(k_rows), jnp.stack(v_rows),
            jnp.stack(s5r), jnp.stack(s5i), jnp.stack(wkvs), jnp.stack(shifts))


def kernel(x_prompt, x_sample, c_prompt, c_sample, cache_mla_ckv, cache_mla_krope, cache_sb_k,
           cache_sb_v, page_table, state_s5_re, state_s5_im, state_rwkv_wkv, state_rwkv_shift,
           ada_w, ada_b, norm1_g, norm2_g, final_g, ab_w_in, ab_w_out, s5_lambda_re, s5_lambda_im,
           s5_log_dt, s5_b_re, s5_b_im, s5_c_re, s5_c_im, s5_d, s5_w_glu, s5_b_glu, mla_g_q,
           mla_w_uq, mla_g_kv, mla_w_uk, mla_w_uv, cd_w_in, cd_w_out, rwkv_mu, rwkv_w0, rwkv_w2,
           rwkv_a0, rwkv_a2, rwkv_g2, rwkv_k_k, rwkv_k_a, rwkv_r_k, rwkv_ln_w, rwkv_ln_b,
           peer_w_q, peer_k1, peer_k2, peer_u, peer_v):
    w = dict(ada_w=ada_w, ada_b=ada_b, norm1_g=norm1_g, norm2_g=norm2_g, final_g=final_g,
             s5_lambda_re=s5_lambda_re)
    ab = dict(ab_w_in=ab_w_in, ab_w_out=ab_w_out, s5_lambda_re=s5_lambda_re,
              s5_lambda_im=s5_lambda_im, s5_log_dt=s5_log_dt, s5_b_re=s5_b_re, s5_b_im=s5_b_im,
              s5_c_re=s5_c_re, s5_c_im=s5_c_im, s5_d=s5_d, s5_w_glu=s5_w_glu, s5_b_glu=s5_b_glu,
              mla_g_q=mla_g_q, mla_w_uq=mla_w_uq, mla_g_kv=mla_g_kv, mla_w_uk=mla_w_uk,
              mla_w_uv=mla_w_uv)
    cd = dict(cd_w_in=cd_w_in, cd_w_out=cd_w_out, rwkv_mu=rwkv_mu, rwkv_w0=rwkv_w0,
              rwkv_w2=rwkv_w2, rwkv_a0=rwkv_a0, rwkv_a2=rwkv_a2, rwkv_g2=rwkv_g2,
              rwkv_k_k=rwkv_k_k, rwkv_k_a=rwkv_k_a, rwkv_r_k=rwkv_r_k, rwkv_ln_w=rwkv_ln_w,
              rwkv_ln_b=rwkv_ln_b)
    depth = ada_w.shape[0]
    ab_wts = [_ab_weights({k: v[i] for k, v in ab.items()}) for i in range((depth + 1) // 2)]
    cd_wts = [_cd_weights({k: v[i] for k, v in cd.items()}) for i in range(depth // 2)]
    n_exp, d_model = peer_v.shape[1:]
    value_blocks = lambda v: jnp.transpose(
        v.reshape(n_exp // EXPERT_TILE, EXPERT_TILE, d_model), (0, 2, 1)).astype(BF16)
    peer_wts = [dict(wq=peer_w_q[l].astype(BF16), k1=peer_k1[l].astype(BF16),
                     k2=peer_k2[l].astype(BF16), u=peer_u[l].astype(BF16),
                     vt=value_blocks(peer_v[l])) for l in range(depth)]
    cache = dict(mla_ckv=cache_mla_ckv, mla_krope=cache_mla_krope, sb_k=cache_sb_k, sb_v=cache_sb_v,
                 page_table=page_table, s5_re=state_s5_re, s5_im=state_s5_im, wkv=state_rwkv_wkv,
                 shift=state_rwkv_shift)
    p = _trunk(x_prompt, c_prompt, w, ab_wts, cd_wts, peer_wts, None)
    s = _trunk(x_sample, c_sample, w, ab_wts, cd_wts, peer_wts, cache)
    return (p[0], s[0]) + p[1:] + s[1:]
```

```python
import functools
import math

import jax
import jax.numpy as jnp
from jax import lax
from jax.experimental import pallas as pl
from jax.experimental.pallas import tpu as pltpu

F32 = jnp.float32
BF16 = jnp.bfloat16

NORM_EPS = 1e-6
ROPE_BASE = 10000.0
PAGE_SIZE = 128
S5_GROUP = 16
S5_STATE = 64
MLA_HEADS = 8
MLA_NOPE = 64
MLA_ROPE = 32
MLA_V = 64
MLA_KV_RANK = 256
MLA_KPAD = 384
SB_HEADS = 8
SB_HEAD_DIM = 64
RW_HEADS = 8
RW_HEAD_DIM = 64
RW_DECAY_RANK = 64
RW_A_RANK = 64
RW_GATE_RANK = 128
RW_GN_EPS = 64e-5
PEER_HEADS = 8
PEER_N_KEYS = 128
PEER_TOPK = 16
NEG_BIG = -3.0e38

VMEM_LIMIT = 48 * 1024 * 1024


def _params(*sem):
    return pltpu.CompilerParams(dimension_semantics=sem, vmem_limit_bytes=VMEM_LIMIT)


def _dot(a, b):
    return jnp.dot(a, b, preferred_element_type=F32)


def _dot_nt(a, b):
    return lax.dot_general(a, b, (((1,), (1,)), ((), ())), preferred_element_type=F32)


def _dot_tn(a, b):
    return lax.dot_general(a, b, (((0,), (0,)), ((), ())), preferred_element_type=F32)


def _split3(a):
    hi = a.astype(BF16)
    r1 = a - hi.astype(F32)
    mid = r1.astype(BF16)
    lo = (r1 - mid.astype(F32)).astype(BF16)
    return hi, mid, lo


def _dotx(dot, a, b):
    a0, a1, a2 = _split3(a)
    b0, b1, b2 = _split3(b)
    return (dot(a0, b0) + (dot(a0, b1) + dot(a1, b0))
            + (dot(a1, b1) + dot(a0, b2) + dot(a2, b0)))


def _dot_exact_rhs(a, b_bf16):
    a0, a1, a2 = _split3(a)
    return _dot(a0, b_bf16) + _dot(a1, b_bf16) + _dot(a2, b_bf16)


def _gelu(x):
    c = math.sqrt(2.0 / math.pi)
    return 0.5 * x * (1.0 + jnp.tanh(c * (x + 0.044715 * (x * x * x))))


def _sigmoid(x):
    return 1.0 / (1.0 + jnp.exp(-x))


def _softplus(x):
    return jnp.maximum(x, 0.0) + jnp.log1p(jnp.exp(-jnp.abs(x)))


def _rms(x):
    return x * lax.rsqrt(jnp.mean(x * x, axis=-1, keepdims=True) + NORM_EPS)


def _ada_kernel(c_ref, w_ref, b_ref, o_ref):
    c = c_ref[...]
    s = (c * _sigmoid(c)).astype(BF16)
    o_ref[...] = _dot(s, w_ref[...].astype(BF16)) + b_ref[...]


def _ada(c, ada_w, ada_b):
    L, D, N = ada_w.shape
    Bc = c.shape[0]
    tn = 1024
    return pl.pallas_call(
        _ada_kernel,
        grid=(L, N // tn),
        in_specs=[pl.BlockSpec((Bc, D), lambda l, j: (0, 0)),
                  pl.BlockSpec((None, D, tn), lambda l, j: (l, 0, j)),
                  pl.BlockSpec((None, 1, tn), lambda l, j: (l, 0, j))],
        out_specs=pl.BlockSpec((None, Bc, tn), lambda l, j: (l, 0, j)),
        out_shape=jax.ShapeDtypeStruct((L, Bc, N), F32),
        compiler_params=_params("arbitrary", "arbitrary"),
        name="ada",
    )(c, ada_w, ada_b.reshape(L, 1, N))


def _mod_spec(T, tm, D):
    if tm <= T:
        return pl.BlockSpec((None, 1, D), lambda i: (i * tm // T, 0, 0))
    return pl.BlockSpec((None, tm, D), lambda i: (0, i, 0))


def _mod_rows(m, T, tm):
    if tm <= T:
        return m[:, None, :]
    return jnp.repeat(m, T, axis=0)[None]


def _norm_mod_mm_kernel(x_ref, g_ref, sc_ref, sh_ref, w_ref, *o_refs):
    h = _rms(x_ref[...]) * g_ref[...] * (1.0 + sc_ref[...]) + sh_ref[...]
    res = _dot(h.astype(BF16), w_ref[...])
    off = 0
    for o_ref in o_refs:
        n = o_ref.shape[1]
        o_ref[...] = res[:, off:off + n]
        off += n


def _norm_mod_mm(x, g, sc, sh, w, T, tm, widths):
    M, D = x.shape
    N = w.shape[1]
    assert sum(widths) == N
    return pl.pallas_call(
        _norm_mod_mm_kernel,
        grid=(M // tm,),
        in_specs=[pl.BlockSpec((tm, D), lambda i: (i, 0)),
                  pl.BlockSpec((1, D), lambda i: (0, 0)),
                  _mod_spec(T, tm, D), _mod_spec(T, tm, D),
                  pl.BlockSpec((D, N), lambda i: (0, 0))],
        out_specs=[pl.BlockSpec((tm, n), lambda i: (i, 0)) for n in widths],
        out_shape=[jax.ShapeDtypeStruct((M, n), F32) for n in widths],
        compiler_params=_params("arbitrary"),
        name="norm_mod_mm",
    )(x, g.reshape(1, D), _mod_rows(sc, T, tm), _mod_rows(sh, T, tm), w)


def _mm_resid_kernel(a_ref, w_ref, x_ref, gate_ref, o_ref):
    o_ref[...] = x_ref[...] + gate_ref[...] * _dot(a_ref[...].astype(BF16), w_ref[...])


def _mm_resid(a, w, x, gate, T, tm):
    M, K = a.shape
    D = w.shape[1]
    return pl.pallas_call(
        _mm_resid_kernel,
        grid=(M // tm,),
        in_specs=[pl.BlockSpec((tm, K), lambda i: (i, 0)),
                  pl.BlockSpec((K, D), lambda i: (0, 0)),
                  pl.BlockSpec((tm, D), lambda i: (i, 0)),
                  _mod_spec(T, tm, D)],
        out_specs=pl.BlockSpec((tm, D), lambda i: (i, 0)),
        out_shape=jax.ShapeDtypeStruct((M, D), F32),
        compiler_params=_params("arbitrary"),
        name="mm_resid",
    )(a, w, x, _mod_rows(gate, T, tm))


def _rmsnorm_kernel(x_ref, g_ref, o_ref):
    o_ref[...] = _rms(x_ref[...]) * g_ref[...]


def _rmsnorm(x, g, tm):
    M, D = x.shape
    return pl.pallas_call(
        _rmsnorm_kernel,
        grid=(M // tm,),
        in_specs=[pl.BlockSpec((tm, D), lambda i: (i, 0)),
                  pl.BlockSpec((1, D), lambda i: (0, 0))],
        out_specs=pl.BlockSpec((tm, D), lambda i: (i, 0)),
        out_shape=jax.ShapeDtypeStruct((M, D), F32),
        compiler_params=_params("arbitrary"),
        name="final_norm",
    )(x, g.reshape(1, D))


def _s5_kernel(u_ref, s0r_ref, s0i_ref, abr_ref, abi_ref, cor_ref, coi_ref,
               wbr_ref, wbi_ref, wcr_ref, wci_ref, d_ref, wg_ref, bg_ref,
               o_ref, sr_ref, si_ref, xr_buf, xi_buf, *, nb, tc):
    c = pl.program_id(0)

    @pl.when(c == 0)
    def _():
        sr_ref[...] = s0r_ref[...]
        si_ref[...] = s0i_ref[...]

    u = u_ref[...]
    ub = u.astype(BF16)
    bu_r = _dot(ub, wbr_ref[...])
    bu_i = _dot(ub, wbi_ref[...])
    cor, coi = cor_ref[...], coi_ref[...]
    xr_buf[...] = cor * bu_r - coi * bu_i
    xi_buf[...] = cor * bu_i + coi * bu_r
    abr, abi = abr_ref[...], abi_ref[...]

    def step(t, carry):
        xr, xi = carry
        rows = pl.ds(pl.multiple_of(t * nb, nb), nb)
        nr = abr * xr - abi * xi + xr_buf[rows, :]
        ni = abr * xi + abi * xr + xi_buf[rows, :]
        xr_buf[rows, :] = nr
        xi_buf[rows, :] = ni
        return nr, ni

    xr, xi = lax.fori_loop(0, tc, step, (sr_ref[...], si_ref[...]))
    sr_ref[...] = xr
    si_ref[...] = xi

    y = (_dot(xr_buf[...].astype(BF16), wcr_ref[...])
         - _dot(xi_buf[...].astype(BF16), wci_ref[...]) + d_ref[...] * u)
    z = _gelu(y)
    o_ref[...] = z * _sigmoid(_dot(z.astype(BF16), wg_ref[...]) + bg_ref[...])


def _s5_consts(lam_re, lam_im, log_dt, b_re, b_im, c_re, c_im):
    G, P = lam_re.shape
    Hc = b_re.shape[-1]
    dt = jnp.exp(log_dt)[:, None]
    mag = jnp.exp(lam_re * dt)
    ab_re, ab_im = mag * jnp.cos(lam_im * dt), mag * jnp.sin(lam_im * dt)
    den = lam_re * lam_re + lam_im * lam_im
    nr = ab_re - 1.0
    co_re = (nr * lam_re + ab_im * lam_im) / den
    co_im = (ab_im * lam_re - nr * lam_im) / den
    eye = jnp.eye(G, dtype=F32)
    wb = lambda b: jnp.einsum('gph,gk->ghkp', b, eye).reshape(G * Hc, G * P).astype(BF16)
    wc = lambda cc: jnp.einsum('ghp,gk->gpkh', cc, eye).reshape(G * P, G * Hc).astype(BF16)
    flat = lambda a: a.reshape(1, G * P)
    return (flat(ab_re), flat(ab_im), flat(co_re), flat(co_im),
            wb(b_re), wb(b_im), wc(c_re), wc(c_im))


def _s5(u_tb, s0_re, s0_im, consts, d, w_glu, b_glu, nb, tc):
    M, W = u_tb.shape
    T = M // nb
    GP = s0_re.shape[1]
    ab_re, ab_im, co_re, co_im, wbr, wbi, wcr, wci = consts
    rows = tc * nb
    full = lambda shape: pl.BlockSpec(shape, lambda c: (0,) * len(shape))
    return pl.pallas_call(
        functools.partial(_s5_kernel, nb=nb, tc=tc),
        grid=(T // tc,),
        in_specs=[pl.BlockSpec((rows, W), lambda c: (c, 0)),
                  full((nb, GP)), full((nb, GP)),
                  full((1, GP)), full((1, GP)), full((1, GP)), full((1, GP)),
                  full((W, GP)), full((W, GP)), full((GP, W)), full((GP, W)),
                  full((1, W)), full((W, W)), full((1, W))],
        out_specs=[pl.BlockSpec((rows, W), lambda c: (c, 0)), full((nb, GP)), full((nb, GP))],
        out_shape=[jax.ShapeDtypeStruct((M, W), F32),
                   jax.ShapeDtypeStruct((nb, GP), F32),
                   jax.ShapeDtypeStruct((nb, GP), F32)],
        scratch_shapes=[pltpu.VMEM((rows, GP), F32), pltpu.VMEM((rows, GP), F32)],
        compiler_params=_params("arbitrary"),
        name="s5",
    )(u_tb, s0_re, s0_im, ab_re, ab_im, co_re, co_im, wbr, wbi, wcr, wci,
      d.reshape(1, W), w_glu.astype(BF16), b_glu.reshape(1, W))


def _rope_tables(pos, width):
    half = MLA_ROPE // 2
    inv = ROPE_BASE ** (-jnp.arange(half, dtype=F32) / half)
    ang = pos.astype(F32)[:, None] * inv[None, :]
    cos, sin = jnp.cos(ang), jnp.sin(ang)
    pad = jnp.zeros((pos.shape[0], width - MLA_ROPE), F32)
    return (jnp.concatenate([cos, cos, pad], axis=1), jnp.concatenate([-sin, sin, pad], axis=1))


def _swap_halves(w):
    half = w.shape[-1] // 2
    return jnp.concatenate([w[..., half:], w[..., :half]], axis=-1)


def _pad_cols(w, width):
    return jnp.pad(w, [(0, 0)] * (w.ndim - 1) + [(0, width - w.shape[-1])])


def _mla_prep_kernel(cq_ref, ckv_ref, kr_ref, krs_ref, c2_ref, s2_ref, gq_ref, gkv_ref,
                     wq_ref, wuk_ref, q_ref, kcat_ref, ckvn_ref, krot_ref):
    H = q_ref.shape[0]
    c2, s2 = c2_ref[...], s2_ref[...]
    cqn = (_rms(cq_ref[...]) * gq_ref[...]).astype(BF16)
    qall = _dot(cqn, wq_ref[...])
    for h in range(H):
        base = h * 384
        nope = qall[:, base:base + 128].astype(BF16)
        rot = qall[:, base + 128:base + 256] * c2 + qall[:, base + 256:base + 384] * s2
        q_ref[h, :, 0:MLA_KV_RANK] = _dot(nope, wuk_ref[h]).astype(BF16)
        q_ref[h, :, MLA_KV_RANK:MLA_KPAD] = rot.astype(BF16)
    ckvn = _rms(ckv_ref[...]) * gkv_ref[...]
    krot = kr_ref[...] * c2 + krs_ref[...] * s2
    ckvn_ref[...] = ckvn
    krot_ref[...] = krot[:, :MLA_ROPE]
    kcat_ref[:, 0:MLA_KV_RANK] = ckvn.astype(BF16)
    kcat_ref[:, MLA_KV_RANK:MLA_KPAD] = krot.astype(BF16)


def _mla_prep(cq, ckv, kr, krs, c2, s2, g_q, g_kv, wq, wuk, tm):
    M = cq.shape[0]
    H = MLA_HEADS
    col = lambda w: pl.BlockSpec((tm, w), lambda i: (i, 0))
    full = lambda shape: pl.BlockSpec(shape, lambda i: (0,) * len(shape))
    return pl.pallas_call(
        _mla_prep_kernel,
        grid=(M // tm,),
        in_specs=[col(384), col(256), col(128), col(128), col(128), col(128),
                  full((1, 384)), full((1, 256)), full((384, H * 384)), full((H, 128, 256))],
        out_specs=[pl.BlockSpec((H, tm, MLA_KPAD), lambda i: (0, i, 0)),
                   col(MLA_KPAD), col(MLA_KV_RANK), col(MLA_ROPE)],
        out_shape=[jax.ShapeDtypeStruct((H, M, MLA_KPAD), BF16),
                   jax.ShapeDtypeStruct((M, MLA_KPAD), BF16),
                   jax.ShapeDtypeStruct((M, MLA_KV_RANK), F32),
                   jax.ShapeDtypeStruct((M, MLA_ROPE), F32)],
        compiler_params=_params("arbitrary"),
        name="mla_prep",
    )(cq, ckv, kr, krs, c2, s2, g_q.reshape(1, -1), g_kv.reshape(1, -1), wq, wuk)


def _mla_weights(w_uq, w_uk):
    C, H, _ = w_uq.shape
    nope = _pad_cols(w_uq[..., :MLA_NOPE], 128)
    rope = w_uq[..., MLA_NOPE:]
    wq = jnp.concatenate([nope, _pad_cols(rope, 128), _pad_cols(_swap_halves(rope), 128)], axis=-1)
    wuk = jnp.pad(jnp.transpose(w_uk, (1, 2, 0)), ((0, 0), (0, 128 - MLA_NOPE), (0, 0)))
    return wq.reshape(C, H * 384).astype(BF16), wuk.astype(BF16)


def _mla_attn_kernel(q_ref, k_ref, wuv_ref, o_ref, m_s, l_s, acc_s, *, tq, tk, scale):
    qi = pl.program_id(1)
    H = q_ref.shape[0]
    rows = H * tq
    q = q_ref[...].reshape(rows, MLA_KPAD)
    m_s[...] = jnp.full((rows, 1), NEG_BIG, F32)
    l_s[...] = jnp.zeros((rows, 1), F32)
    acc_s[...] = jnp.zeros((rows, MLA_KV_RANK), F32)
    row_t = qi * tq + lax.broadcasted_iota(jnp.int32, (H, tq, 1), 1).reshape(rows, 1)
    groups = [slice(g * rows // 2, (g + 1) * rows // 2) for g in range(2)]

    def block(kb, masked):
        k = k_ref[pl.ds(pl.multiple_of(kb * tk, tk), tk), :]
        vals = k[:, :MLA_KV_RANK]
        s = [_dot_nt(q[g], k) * scale for g in groups]
        if masked:
            kpos = kb * tk + lax.broadcasted_iota(jnp.int32, (1, tk), 1)
            s = [jnp.where(kpos <= row_t[g], sg, NEG_BIG) for g, sg in zip(groups, s)]
        m_prev = [m_s[g] for g in groups]
        m_new = [jnp.maximum(mp, jnp.max(sg, axis=-1, keepdims=True)) for mp, sg in zip(m_prev, s)]
        alpha = [jnp.exp(mp - mn) for mp, mn in zip(m_prev, m_new)]
        p = [jnp.exp(sg - mn) for sg, mn in zip(s, m_new)]
        pv = [_dot(pg.astype(BF16), vals) for pg in p]
        for g, al, pg, pvg, mn in zip(groups, alpha, p, pv, m_new):
            l_s[g] = al * l_s[g] + jnp.sum(pg, axis=-1, keepdims=True)
            acc_s[g] = al * acc_s[g] + pvg
            m_s[g] = mn

    n_full = (qi * tq) // tk

    def body(kb, carry):
        block(kb, False)
        return carry

    lax.fori_loop(0, n_full, body, 0)
    block(n_full, True)
    o = (acc_s[...] / l_s[...]).astype(BF16)
    for h in range(H):
        o_ref[:, h * MLA_V:(h + 1) * MLA_V] = _dot(o[h * tq:(h + 1) * tq], wuv_ref[h])


def _mla_attn(q, kcat, wuv, B, T, tq, tk):
    H = q.shape[0]
    nq = T // tq
    scale = (MLA_NOPE + MLA_ROPE) ** -0.5
    return pl.pallas_call(
        functools.partial(_mla_attn_kernel, tq=tq, tk=tk, scale=scale),
        grid=(B, nq),
        in_specs=[pl.BlockSpec((H, tq, MLA_KPAD), lambda b, i: (0, b * nq + i, 0)),
                  pl.BlockSpec((T, MLA_KPAD), lambda b, i: (b, 0)),
                  pl.BlockSpec((H, MLA_KV_RANK, MLA_V), lambda b, i: (0, 0, 0))],
        out_specs=pl.BlockSpec((tq, H * MLA_V), lambda b, i: (b * nq + i, 0)),
        out_shape=jax.ShapeDtypeStruct((B * T, H * MLA_V), F32),
        scratch_shapes=[pltpu.VMEM((H * tq, 1), F32), pltpu.VMEM((H * tq, 1), F32),
                        pltpu.VMEM((H * tq, MLA_KV_RANK), F32)],
        compiler_params=_params("arbitrary", "arbitrary"),
        name="mla_attn",
    )(q, kcat, wuv)


def _mla_decode_kernel(pt_ref, q_ref, knew_ref, *refs, npg, nt, scale):
    ckv_refs = refs[:npg]
    kr_refs = refs[npg:2 * npg]
    o_ref, m_s, l_s, acc_s, kbuf, rbuf = refs[2 * npg:]
    j = pl.program_id(1)
    q = q_ref[...]
    rows = q.shape[0]

    def update(s, vals):
        m_prev = m_s[...]
        m_new = jnp.maximum(m_prev, jnp.max(s, axis=-1, keepdims=True))
        alpha = jnp.exp(m_prev - m_new)
        p = jnp.exp(s - m_new)
        l_s[...] = alpha * l_s[...] + jnp.sum(p, axis=-1, keepdims=True)
        acc_s[...] = alpha * acc_s[...] + _dot(p.astype(BF16), vals)
        m_s[...] = m_new

    @pl.when(j == 0)
    def _():
        m_s[...] = jnp.full((rows, 1), NEG_BIG, F32)
        l_s[...] = jnp.zeros((rows, 1), F32)
        acc_s[...] = jnp.zeros((rows, MLA_KV_RANK), F32)
        knew = knew_ref[...]
        s = _dot_nt(q, knew) * scale
        t_row = lax.broadcasted_iota(jnp.int32, (rows, 1), 0) % nt
        col = lax.broadcasted_iota(jnp.int32, (1, PAGE_SIZE), 1)
        update(jnp.where(col <= t_row, s, NEG_BIG), knew[:, :MLA_KV_RANK])

    for r in range(npg):
        rows_r = slice(r * PAGE_SIZE, (r + 1) * PAGE_SIZE)
        kbuf[rows_r, :] = ckv_refs[r][...].astype(BF16)
        rbuf[:, rows_r] = kr_refs[r][...].astype(BF16)
    lat = kbuf[...]
    s = _dot_nt(q[:, :MLA_KV_RANK], lat) + _dot(q[:, MLA_KV_RANK:MLA_KV_RANK + MLA_ROPE], rbuf[...])
    update(s * scale, lat)

    @pl.when(j == pl.num_programs(1) - 1)
    def _():
        o_ref[...] = acc_s[...] / l_s[...]


def _mla_decode(q, knew, cache_ckv, cache_kr, page_table, layer, nt, npg=16):
    B, rows, _ = q.shape
    n_pages = page_table.shape[1]
    page = lambda r, a, c: pl.BlockSpec(
        (None, None, a, c), lambda b, j, pt: (layer, pt[b, j * npg + r], 0, 0))
    grid_spec = pltpu.PrefetchScalarGridSpec(
        num_scalar_prefetch=1,
        grid=(B, n_pages // npg),
        in_specs=[pl.BlockSpec((None, rows, MLA_KPAD), lambda b, j, pt: (b, 0, 0)),
                  pl.BlockSpec((None, PAGE_SIZE, MLA_KPAD), lambda b, j, pt: (b, 0, 0))]
        + [page(r, PAGE_SIZE, MLA_KV_RANK) for r in range(npg)]
        + [page(r, MLA_ROPE, PAGE_SIZE) for r in range(npg)],
        out_specs=pl.BlockSpec((None, rows, MLA_KV_RANK), lambda b, j, pt: (b, 0, 0)),
        scratch_shapes=[pltpu.VMEM((rows, 1), F32), pltpu.VMEM((rows, 1), F32),
                        pltpu.VMEM((rows, MLA_KV_RANK), F32),
                        pltpu.VMEM((npg * PAGE_SIZE, MLA_KV_RANK), BF16),
                        pltpu.VMEM((MLA_ROPE, npg * PAGE_SIZE), BF16)])
    scale = (MLA_NOPE + MLA_ROPE) ** -0.5
    return pl.pallas_call(
        functools.partial(_mla_decode_kernel, npg=npg, nt=nt, scale=scale),
        grid_spec=grid_spec,
        out_shape=jax.ShapeDtypeStruct((B, rows, MLA_KV_RANK), F32),
        compiler_params=_params("arbitrary", "arbitrary"),
        name="mla_decode",
    )(page_table, q, knew, *([cache_ckv] * npg), *([cache_kr] * npg))


def _bmm_kernel(x_ref, w_ref, o_ref):
    o_ref[...] = _dot(x_ref[...].astype(BF16), w_ref[...])


def _bmm(x, w):
    H, M, K = x.shape
    N = w.shape[2]
    return pl.pallas_call(
        _bmm_kernel,
        grid=(H,),
        in_specs=[pl.BlockSpec((None, M, K), lambda h: (h, 0, 0)),
                  pl.BlockSpec((None, K, N), lambda h: (h, 0, 0))],
        out_specs=pl.BlockSpec((None, M, N), lambda h: (h, 0, 0)),
        out_shape=jax.ShapeDtypeStruct((H, M, N), F32),
        compiler_params=_params("arbitrary"),
        name="bmm",
    )(x, w)


def _ab_in_weights(w_in, s5_width):
    a, b, c = s5_width, s5_width + 384, s5_width + 384 + MLA_KV_RANK
    u, cq, ckv, kr = w_in[:, :a], w_in[:, a:b], w_in[:, b:c], w_in[:, c:]
    return jnp.concatenate(
        [cq, u, ckv, _pad_cols(kr, 128), _pad_cols(_swap_halves(kr), 128)], axis=1).astype(BF16)


def _ab_layer(x, mods, g_norm, pos, B, T, tm, cache, wts, tq=128, tk=512):
    sh1, sc1, g1 = mods
    M = B * T
    W = wts['s5_d'].size
    GP = wts['s5_consts'][0].size
    cq, u, ckv, kr, krs = _norm_mod_mm(x, g_norm, sc1, sh1, wts['w_in'], T, tm,
                                       (384, W, MLA_KV_RANK, 128, 128))
    u_tb = jnp.transpose(u.reshape(B, T, W), (1, 0, 2)).reshape(M, W)
    if cache is None:
        s0_re = jnp.zeros((B, GP), F32)
        s0_im = jnp.zeros((B, GP), F32)
    else:
        s0_re, s0_im = cache['s5_re'].reshape(B, GP), cache['s5_im'].reshape(B, GP)
    s5_tb, s_re, s_im = _s5(u_tb, s0_re, s0_im, wts['s5_consts'], wts['s5_d'], wts['s5_w_glu'],
                            wts['s5_b_glu'], nb=B, tc=min(T, 64))
    s5_out = jnp.transpose(s5_tb.reshape(T, B, W), (1, 0, 2)).reshape(M, W)
    c2, s2 = _rope_tables(pos, 128)
    c2, s2 = jnp.tile(c2, (B, 1)), jnp.tile(s2, (B, 1))
    q, kcat, ckvn, krot = _mla_prep(cq, ckv, kr, krs, c2, s2, wts['mla_g_q'], wts['mla_g_kv'],
                                    wts['mla_wq'], wts['mla_wuk'], tm)
    H = MLA_HEADS
    if cache is None:
        o = _mla_attn(q, kcat, wts['mla_wuv'], B, T, tq, min(tk, T))
    else:
        qd = jnp.transpose(q.reshape(H, B, T, MLA_KPAD), (1, 0, 2, 3)).reshape(B, H * T, MLA_KPAD)
        knew = jnp.pad(kcat.reshape(B, T, MLA_KPAD), ((0, 0), (0, PAGE_SIZE - T), (0, 0)))
        o_lat = _mla_decode(qd, knew, cache['ckv'], jnp.transpose(cache['kr'], (0, 1, 3, 2)),
                            cache['page_table'],
                            cache['layer'], T)
        o_lat = jnp.transpose(o_lat.reshape(B, H, T, MLA_KV_RANK), (1, 0, 2, 3)).reshape(H, M, -1)
        o = jnp.transpose(_bmm(o_lat, wts['mla_wuv']), (1, 0, 2)).reshape(M, H * MLA_V)
    mix = jnp.concatenate([s5_out, o], axis=1)
    x = _mm_resid(mix, wts['w_out'], x, g1, T, tm)
    return x, ckvn, krot, s_re, s_im


def _suffix_sums(ln, tri):
    hi = ln.astype(BF16)
    mid = (ln - hi.astype(F32)).astype(BF16)
    return _dot(hi, tri) + _dot(mid, tri)


def _sb_attn_kernel(q_ref, k_ref, v_ref, tri_ref, o_ref, acc_s, r_s, *, tq, scale):
    qi = pl.program_id(2)
    hs = range(q_ref.shape[0])
    tri = tri_ref[...]
    acc_s[...] = jnp.zeros(acc_s.shape, F32)
    r_s[...] = jnp.zeros(r_s.shape, F32)

    def block(kb, diag):
        rows = pl.ds(pl.multiple_of(kb * tq, tq), tq)
        z = [_dot_nt(q_ref[h], k_ref[h, rows, :]) * scale for h in hs]
        sp = [_softplus(a) for a in z]
        if diag:
            mask = (lax.broadcasted_iota(jnp.int32, (tq, tq), 1)
                    < lax.broadcasted_iota(jnp.int32, (tq, tq), 0))
            ln = [jnp.where(mask, -a, 0.0) for a in sp]
        else:
            ln = [-a for a in sp]
        a = [jnp.exp(z[h] - sp[h] + _suffix_sums(ln[h], tri) + r_s[h]) for h in hs]
        if diag:
            a = [jnp.where(mask, ah, 0.0) for ah in a]
        for h in hs:
            acc_s[h] += _dot(a[h].astype(BF16), v_ref[h, rows, :])
            r_s[h] += jnp.sum(ln[h], axis=-1, keepdims=True)

    block(qi, True)

    def body(i, carry):
        block(qi - 1 - i, False)
        return carry

    lax.fori_loop(0, qi, body, 0)
    o_ref[...] = acc_s[...]


def _tri(n):
    return (lax.broadcasted_iota(jnp.int32, (n, n), 0)
            > lax.broadcasted_iota(jnp.int32, (n, n), 1)).astype(BF16)


def _sb_attn(q, k, v, tq):
    B, H, T, d = q.shape
    hb = 2
    seq = pl.BlockSpec((None, hb, T, d), lambda b, h, i: (b, h, 0, 0))
    blk = pl.BlockSpec((None, hb, tq, d), lambda b, h, i: (b, h, i, 0))
    return pl.pallas_call(
        functools.partial(_sb_attn_kernel, tq=tq, scale=d ** -0.5),
        grid=(B, H // hb, T // tq),
        in_specs=[blk, seq, seq, pl.BlockSpec((tq, tq), lambda b, h, i: (0, 0))],
        out_specs=blk,
        out_shape=jax.ShapeDtypeStruct((B, H, T, d), F32),
        scratch_shapes=[pltpu.VMEM((hb, tq, d), F32), pltpu.VMEM((hb, tq, 1), F32)],
        compiler_params=_params("arbitrary", "arbitrary", "arbitrary"),
        name="sb_attn",
    )(q, k, v, _tri(tq))


def _sb_decode_kernel(pt_ref, q_ref, knew_ref, vnew_ref, tw_ref, *refs, npg, scale):
    k_refs = refs[:npg]
    v_refs = refs[npg:2 * npg]
    o_ref, acc_s, r_s, kbuf, vbuf = refs[2 * npg:]
    j = pl.program_id(1)
    H, qrows, d = q_ref.shape
    hs = range(H)
    tw = tw_ref[...]

    def block(keys, vals, n, fresh):
        cols = n * PAGE_SIZE
        z = jnp.concatenate([_dot(q_ref[h], keys[h]) for h in hs], axis=0) * scale
        sp = _softplus(z)
        if fresh:
            t_row = lax.broadcasted_iota(jnp.int32, (H * qrows, cols), 0) % qrows
            mask = lax.broadcasted_iota(jnp.int32, (H * qrows, cols), 1) < t_row
            ln = jnp.where(mask, -sp, 0.0)
        else:
            ln = -sp
        hi = ln.astype(BF16)
        mid = (ln - hi.astype(F32)).astype(BF16)
        run = r_s[...]
        after = []
        for p in range(n):
            sl = slice(p * PAGE_SIZE, (p + 1) * PAGE_SIZE)
            both = _dot(hi[:, sl], tw) + _dot(mid[:, sl], tw)
            after.append(both[:, :LANES] + run)
            run = run + both[:, LANES:]
        a = jnp.exp(z - sp + jnp.concatenate(after, axis=1))
        if fresh:
            a = jnp.where(mask, a, 0.0)
        a = a.astype(BF16)
        for h in hs:
            acc_s[h] += _dot_nt(a[h * qrows:(h + 1) * qrows], vals[h])
        r_s[...] = run

    @pl.when(j == 0)
    def _():
        acc_s[...] = jnp.zeros(acc_s.shape, F32)
        r_s[...] = jnp.zeros(r_s.shape, F32)
        block(knew_ref, vnew_ref, 1, True)

    for r in range(npg):
        lanes_r = slice(r * PAGE_SIZE, (r + 1) * PAGE_SIZE)
        kbuf[:, :, lanes_r] = k_refs[r][...].astype(BF16)
        vbuf[:, :, lanes_r] = v_refs[r][...].astype(BF16)
    block(kbuf, vbuf, npg, False)

    @pl.when(j == pl.num_programs(1) - 1)
    def _():
        o_ref[...] = acc_s[...]


QROWS = 8


def _sb_decode(q, knew, vnew, cache_kt, cache_vt, page_table, layer, npg=8):
    B, H, qrows, d = q.shape
    assert PAGE_SIZE == LANES
    n_pages = page_table.shape[1]
    page = lambda r: pl.BlockSpec(
        (None, None, H, d, PAGE_SIZE),
        lambda b, j, pt: (layer, pt[b, n_pages - 1 - (j * npg + r)], 0, 0, 0))
    per_b = lambda a, c: pl.BlockSpec((None, H, a, c), lambda b, j, pt: (b, 0, 0, 0))
    ti = lax.broadcasted_iota(jnp.int32, (LANES, 2 * LANES), 0)
    si = lax.broadcasted_iota(jnp.int32, (LANES, 2 * LANES), 1)
    tw = ((ti > si) | (si >= LANES)).astype(BF16)
    grid_spec = pltpu.PrefetchScalarGridSpec(
        num_scalar_prefetch=1,
        grid=(B, n_pages // npg),
        in_specs=[per_b(qrows, d), per_b(d, PAGE_SIZE), per_b(d, PAGE_SIZE),
                  pl.BlockSpec((LANES, 2 * LANES), lambda b, j, pt: (0, 0))]
        + [page(r) for r in range(npg)] * 2,
        out_specs=per_b(qrows, d),
        scratch_shapes=[pltpu.VMEM((H, qrows, d), F32), pltpu.VMEM((H * qrows, LANES), F32),
                        pltpu.VMEM((H, d, npg * PAGE_SIZE), BF16),
                        pltpu.VMEM((H, d, npg * PAGE_SIZE), BF16)])
    return pl.pallas_call(
        functools.partial(_sb_decode_kernel, npg=npg, scale=d ** -0.5),
        grid_spec=grid_spec,
        out_shape=jax.ShapeDtypeStruct((B, H, qrows, d), F32),
        compiler_params=_params("arbitrary", "arbitrary"),
        name="sb_decode",
    )(page_table, q, knew, vnew, tw, *([cache_kt] * npg), *([cache_vt] * npg))


def _rwkv_prep_kernel(p_ref, prev_ref, mu_ref, w0_ref, w2_ref, a0_ref, a2_ref, g2_ref,
                      kkw_ref, ka_ref, ones_ref, r_o, lw_o, k_o, v_o, kk_o, b_o, g_o):
    H, _, N = r_o.shape
    Wd = H * N
    p = p_ref[...]
    ps = p + (prev_ref[...] - p) * mu_ref[...]
    r, k, v = ps[:, :Wd], ps[:, Wd:2 * Wd], ps[:, 2 * Wd:3 * Wd]
    o = 3 * Wd
    xw = ps[:, o:o + RW_DECAY_RANK]
    xa = ps[:, o + RW_DECAY_RANK:o + RW_DECAY_RANK + RW_A_RANK]
    xg = ps[:, o + RW_DECAY_RANK + RW_A_RANK:]
    w_log = -_softplus(-(w0_ref[...] + _dot(jnp.tanh(xw).astype(BF16), w2_ref[...]))) - 0.5
    lw = -jnp.exp(w_log)
    a = _sigmoid(a0_ref[...] + _dot(xa.astype(BF16), a2_ref[...]))
    g = _dot(_sigmoid(xg).astype(BF16), g2_ref[...])
    kk = k * kkw_ref[...]
    ss = _dot_exact_rhs(kk * kk, ones_ref[...])
    kk = kk / jnp.maximum(jnp.sqrt(ss), 1e-12)
    k2 = k * (1.0 + (a - 1.0) * ka_ref[...])
    b = kk * a
    for h in range(H):
        sl = slice(h * N, (h + 1) * N)
        r_o[h] = r[:, sl]
        lw_o[h] = lw[:, sl]
        k_o[h] = k2[:, sl]
        v_o[h] = v[:, sl]
        kk_o[h] = kk[:, sl]
        b_o[h] = b[:, sl]
        g_o[h] = g[:, sl]


def _rwkv_prep(p, prev, wts, tm):
    M, cols = p.shape
    H, N = RW_HEADS, RW_HEAD_DIM
    Wd = H * N
    row = lambda n: pl.BlockSpec((1, n), lambda i: (0, 0))
    mat = lambda a, b: pl.BlockSpec((a, b), lambda i: (0, 0))
    out = pl.BlockSpec((H, tm, N), lambda i: (0, i, 0))
    return pl.pallas_call(
        _rwkv_prep_kernel,
        grid=(M // tm,),
        in_specs=[pl.BlockSpec((tm, cols), lambda i: (i, 0)), pl.BlockSpec((tm, cols), lambda i: (i, 0)),
                  row(cols), row(Wd), mat(RW_DECAY_RANK, Wd), row(Wd), mat(RW_A_RANK, Wd),
                  mat(RW_GATE_RANK, Wd), row(Wd), row(Wd), mat(Wd, Wd)],
        out_specs=[out] * 7,
        out_shape=[jax.ShapeDtypeStruct((H, M, N), F32)] * 7,
        compiler_params=_params("arbitrary"),
        name="rwkv_prep",
    )(p, prev, wts['mu'], wts['w0'], wts['w2'], wts['a0'], wts['a2'], wts['g2'],
      wts['k_k'], wts['k_a'], wts['head_ones'])


def _dot3(dot, a, b):
    a0 = a.astype(BF16)
    a1 = (a - a0.astype(F32)).astype(BF16)
    b0 = b.astype(BF16)
    b1 = (b - b0.astype(F32)).astype(BF16)
    return dot(a0, b0) + (dot(a0, b1) + dot(a1, b0))


SOLVE_BLOCK = 16


def _unit_lower_solve(Ls, Rs, C):
    n = range(len(Ls))
    if C <= SOLVE_BLOCK:
        Us = list(Rs)
        for s in range(C - 1):
            Us = [Us[i] - Ls[i][:, s:s + 1] * Us[i][s:s + 1, :] for i in n]
        return Us
    nb = C // SOLVE_BLOCK
    assert nb * SOLVE_BLOCK == C and nb <= 4
    ti = lax.broadcasted_iota(jnp.int32, (C, C), 0)
    si = lax.broadcasted_iota(jnp.int32, (C, C), 1)
    same = (ti // SOLVE_BLOCK) == (si // SOLVE_BLOCK)
    rep_t = (lax.broadcasted_iota(jnp.int32, (C, SOLVE_BLOCK), 0) % SOLVE_BLOCK
             == lax.broadcasted_iota(jnp.int32, (C, SOLVE_BLOCK), 1))
    rep = (lax.broadcasted_iota(jnp.int32, (SOLVE_BLOCK, C), 1) % SOLVE_BLOCK
           == lax.broadcasted_iota(jnp.int32, (SOLVE_BLOCK, C), 0)).astype(BF16)
    Ld = [jnp.where(same, L, 0.0) for L in Ls]
    Ldc = [_dot_exact_rhs(a, rep_t.astype(BF16)) for a in Ld]
    X = [rep_t.astype(F32) for _ in n]
    for s in range(SOLVE_BLOCK - 1):
        rows = [jnp.broadcast_to(x.reshape(nb, SOLVE_BLOCK, SOLVE_BLOCK)[:, s:s + 1, :],
                                 (nb, SOLVE_BLOCK, SOLVE_BLOCK)).reshape(C, SOLVE_BLOCK) for x in X]
        X = [X[i] - Ldc[i][:, s:s + 1] * rows[i] for i in n]
    T1 = [jnp.where(same, _dot_exact_rhs(x, rep), 0.0) for x in X]
    R1 = [_dot3(_dot, T1[i], Rs[i]) for i in n]
    Mm = [_dot3(_dot, T1[i], Ls[i] - Ld[i]) for i in n]
    M2 = [_dot3(_dot, m, m) for m in Mm]
    Y1 = [R1[i] + _dot3(_dot, M2[i], R1[i]) for i in n]
    return [Y1[i] - _dot3(_dot, Mm[i], Y1[i]) for i in n]


def _rwkv_chunk_kernel(r_ref, lw_ref, k_ref, v_ref, kk_ref, b_ref, g_ref, s0_ref,
                       lnw_ref, lnb_ref, rk_ref, y_ref, s_ref, *, C):
    c = pl.program_id(2)

    @pl.when(c == 0)
    def _():
        s_ref[...] = s0_ref[...]

    hs = range(r_ref.shape[0])
    ti = lax.broadcasted_iota(jnp.int32, (C, C), 0)
    si = lax.broadcasted_iota(jnp.int32, (C, C), 1)
    strict, incl = si < ti, si <= ti
    tril = incl.astype(BF16)
    r, lw, k, v = ([ref[h] for h in hs] for ref in (r_ref, lw_ref, k_ref, v_ref))
    kk, b = ([ref[h] for h in hs] for ref in (kk_ref, b_ref))
    S0 = [s_ref[h] for h in hs]
    parts = [_split3(a) for a in lw]
    cum = [_dot(tril, p0) + _dot(tril, p1) + _dot(tril, p2) for p0, p1, p2 in parts]
    g_in = [jnp.exp(a) for a in cum]
    g_inv = [jnp.exp(-a) for a in cum]
    qt = [kk[h] * jnp.exp(cum[h] - lw[h]) for h in hs]
    bt = [b[h] * g_inv[h] for h in hs]
    kt = [k[h] * g_inv[h] for h in hs]
    rt = [r[h] * g_in[h] for h in hs]
    qr = [jnp.concatenate([qt[h], rt[h]], axis=0) for h in hs]
    bk = [jnp.concatenate([bt[h], kt[h]], axis=0) for h in hs]
    gram = [_dot3(_dot_nt, qr[h], bk[h]) for h in hs]
    Lb = [jnp.where(strict, g[:C, :C], 0.0) for g in gram]
    Lk = [jnp.where(strict, g[:C, C:], 0.0) for g in gram]
    Ab = [jnp.where(incl, g[C:, :C], 0.0) for g in gram]
    Ak = [jnp.where(incl, g[C:, C:], 0.0) for g in gram]
    on_state = [_dot3(_dot_nt, qr[h], S0[h]) for h in hs]
    on_v = [_dot3(_dot, jnp.concatenate([Lk[h], Ak[h]], axis=0), v[h]) for h in hs]
    U = _unit_lower_solve(Lb, [-(on_state[h][:C] + on_v[h][:C]) for h in hs], C)
    y = [on_state[h][C:] + _dot3(_dot, Ab[h], U[h]) + on_v[h][C:] for h in hs]
    for h in hs:
        grown = _dot3(_dot_tn, jnp.concatenate([U[h], v[h]], axis=0), bk[h])
        s_ref[h] = (S0[h] + grown) * g_in[h][C - 1:C, :]
    for h in hs:
        mean = jnp.mean(y[h], axis=-1, keepdims=True)
        var = jnp.mean(jnp.square(y[h] - mean), axis=-1, keepdims=True)
        yn = (y[h] - mean) * lax.rsqrt(var + RW_GN_EPS) * lnw_ref[h] + lnb_ref[h]
        bonus = jnp.sum(r[h] * k[h] * rk_ref[h], axis=-1, keepdims=True) * v[h]
        y_ref[h] = (yn + bonus) * g_ref[h]


def _rwkv_chunks(feats, s0, ln_w, ln_b, r_k, B, T, C, Hb=8):
    H, M, N = feats[0].shape
    nc = T // C
    seq = pl.BlockSpec((Hb, C, N), lambda b, hb, c: (hb, b * nc + c, 0))
    st = pl.BlockSpec((None, Hb, N, N), lambda b, hb, c: (b, hb, 0, 0))
    par = pl.BlockSpec((Hb, 1, N), lambda b, hb, c: (hb, 0, 0))
    return pl.pallas_call(
        functools.partial(_rwkv_chunk_kernel, C=C),
        grid=(B, H // Hb, nc),
        in_specs=[seq] * 7 + [st, par, par, par],
        out_specs=[seq, st],
        out_shape=[jax.ShapeDtypeStruct((H, M, N), F32), jax.ShapeDtypeStruct((B, H, N, N), F32)],
        compiler_params=_params("arbitrary", "arbitrary", "arbitrary"),
        name="rwkv_chunks",
    )(*feats, s0, ln_w.reshape(H, 1, N), ln_b.reshape(H, 1, N), r_k.reshape(H, 1, N))


def _cd_weights(w):
    H, N = RW_HEADS, RW_HEAD_DIM
    Wd = H * N
    row = lambda a: a.reshape(1, -1)
    head = jnp.arange(Wd) // N
    return {
        'w_in': w['cd_w_in'].astype(BF16), 'w_out': w['cd_w_out'].astype(BF16),
        'mu': row(w['rwkv_mu']), 'w0': row(w['rwkv_w0']), 'w2': w['rwkv_w2'].astype(BF16),
        'a0': row(w['rwkv_a0']), 'a2': w['rwkv_a2'].astype(BF16), 'g2': w['rwkv_g2'].astype(BF16),
        'k_k': row(w['rwkv_k_k']), 'k_a': row(w['rwkv_k_a']),
        'head_ones': (head[:, None] == head[None, :]).astype(BF16),
        'ln_w': w['rwkv_ln_w'], 'ln_b': w['rwkv_ln_b'], 'r_k': w['rwkv_r_k'],
    }


def _ab_weights(w):
    wq, wuk = _mla_weights(w['mla_w_uq'], w['mla_w_uk'])
    return {
        'w_in': _ab_in_weights(w['ab_w_in'], w['s5_d'].size), 'w_out': w['ab_w_out'].astype(BF16),
        's5_consts': _s5_consts(w['s5_lambda_re'], w['s5_lambda_im'], w['s5_log_dt'],
                                w['s5_b_re'], w['s5_b_im'], w['s5_c_re'], w['s5_c_im']),
        's5_d': w['s5_d'], 's5_w_glu': w['s5_w_glu'], 's5_b_glu': w['s5_b_glu'],
        'mla_g_q': w['mla_g_q'], 'mla_g_kv': w['mla_g_kv'], 'mla_wq': wq, 'mla_wuk': wuk,
        'mla_wuv': jnp.transpose(w['mla_w_uv'], (1, 0, 2)).astype(BF16),
    }


def _cd_layer(x, mods, g_norm, B, T, tm, cache, wts, tq=256):
    sh1, sc1, g1 = mods
    M = B * T
    H, d = SB_HEADS, SB_HEAD_DIM
    Wsb = H * d
    cols = wts['mu'].shape[1]
    q, k, v, rw = _norm_mod_mm(x, g_norm, sc1, sh1, wts['w_in'], T, tm, (Wsb, Wsb, Wsb, cols))
    if cache is None:
        heads = lambda a: jnp.transpose(a.reshape(B, T, H, d), (0, 2, 1, 3)).astype(BF16)
        sb = _sb_attn(heads(q), heads(k), heads(v), tq)
        sb = jnp.transpose(sb, (0, 2, 1, 3)).reshape(M, Wsb)
    else:
        keys_minor = lambda a: jnp.transpose(a, (0, 1, 3, 4, 2))
        heads = lambda a: jnp.transpose(a.reshape(B, T, H, d), (0, 2, 1, 3))
        qh = jnp.pad(heads(q), ((0, 0), (0, 0), (0, QROWS - T), (0, 0))).astype(BF16)
        fresh = lambda a: jnp.pad(jnp.transpose(heads(a), (0, 1, 3, 2)),
                                  ((0, 0), (0, 0), (0, 0), (0, PAGE_SIZE - T))).astype(BF16)
        sb = _sb_decode(qh, fresh(k), fresh(v), keys_minor(cache['sb_k']),
                        keys_minor(cache['sb_v']), cache['page_table'], cache['layer'])
        sb = jnp.transpose(sb[:, :, :T], (0, 2, 1, 3)).reshape(M, Wsb)
    rw3 = rw.reshape(B, T, cols)
    shift0 = jnp.zeros((B, cols), F32) if cache is None else cache['shift']
    prev = jnp.concatenate([shift0[:, None, :], rw3[:, :-1]], axis=1).reshape(M, cols)
    feats = _rwkv_prep(rw, prev, wts, tm)
    N = RW_HEAD_DIM
    C = min(T, 64)
    if T < 8:
        C = 8
        padt = lambda a: jnp.pad(a.reshape(RW_HEADS, B, T, N),
                                 ((0, 0), (0, 0), (0, C - T), (0, 0))).reshape(RW_HEADS, B * C, N)
        feats = [padt(f) for f in feats]
    s0 = (jnp.zeros((B, RW_HEADS, N, N), F32) if cache is None else cache['wkv'])
    Tp = max(T, C)
    y, wkv = _rwkv_chunks(feats, s0, wts['ln_w'], wts['ln_b'], wts['r_k'], B, Tp, C)
    y = y.reshape(RW_HEADS, B, Tp, N)[:, :, :T]
    rw_out = jnp.transpose(y, (1, 2, 0, 3)).reshape(M, RW_HEADS * N)
    mix = jnp.concatenate([sb, rw_out], axis=1)
    x = _mm_resid(mix, wts['w_out'], x, g1, T, tm)
    return x, k, v, wkv, rw3[:, -1]


LANES = 128


def _top_values(s, n):
    out = []
    rank = jnp.full(s.shape, float(n), F32)
    for i in range(n):
        m = jnp.max(s, axis=0, keepdims=True)
        out.append(m)
        hit = s == m
        rank = jnp.where(hit, float(i), rank)
        s = jnp.where(hit, NEG_BIG, s)
    return out, rank


def _count_ge(s, v, tau):
    assert len(v) == 16
    ge = lambda row: (s + row) >= tau
    t8 = ge(v[7])
    t4 = ge(jnp.where(t8, v[11], v[3]))
    t2 = ge(jnp.where(t8, jnp.where(t4, v[13], v[9]), jnp.where(t4, v[5], v[1])))
    hi = jnp.where(t4, jnp.where(t2, v[14], v[12]), jnp.where(t2, v[10], v[8]))
    lo = jnp.where(t4, jnp.where(t2, v[6], v[4]), jnp.where(t2, v[2], v[0]))
    t1 = ge(jnp.where(t8, hi, lo))
    one = lambda t, n: jnp.where(t, float(n), 0.0)
    return one(t8, 8) + one(t4, 4) + one(t2, 2) + one(t1, 1) + one(ge(v[15]), 1)


def _peer_route_kernel(x_ref, g_ref, sc_ref, sh_ref, wq_ref, k1_ref, k2_ref,
                       h_ref, e1_ref, j_ref, r2_ref, e2_ref, s1_s, s2_s):
    H, G, NK, _ = s1_s.shape
    half = k1_ref.shape[2]
    K = PEER_TOPK
    hf = _rms(x_ref[...]) * g_ref[...] * (1.0 + sc_ref[...]) + sh_ref[...]
    h = hf.astype(BF16)
    h_ref[...] = hf.T.astype(BF16)
    q = _dot(h, wq_ref[...]).astype(BF16)
    for hd in range(H):
        base = hd * 2 * half
        r1 = _dot_nt(k1_ref[hd], q[:, base:base + half])
        r2 = _dot_nt(k2_ref[hd], q[:, base + half:base + 2 * half])
        for g in range(G):
            s1_s[hd, g] = r1[:, g * LANES:(g + 1) * LANES]
            s2_s[hd, g] = r2[:, g * LANES:(g + 1) * LANES]
    row16 = lax.broadcasted_iota(jnp.int32, (K, LANES), 0)
    row8 = lax.broadcasted_iota(jnp.int32, (8, LANES), 0)

    def lane_group(c, carry):
        for hd in range(H):
            s1 = s1_s[hd, c]
            s2 = s2_s[hd, c]
            v1, _ = _top_values(s1, K)
            v2, rank = _top_values(s2, K)
            v2t = jnp.zeros((K, LANES), F32)
            for i in range(K):
                v2t = jnp.where(row16 == i, v2[i], v2t)
            cands = [v1[0] + v2t]
            for i in range(1, K):
                cands.append(jnp.where(row8 < K // (i + 1), v1[i] + v2t[:8], NEG_BIG))
            best = []
            for _ in range(K):
                m8 = jnp.maximum(cands[0][:8], cands[0][8:])
                for cnd in cands[1:]:
                    m8 = jnp.maximum(m8, cnd)
                m = jnp.max(m8, axis=0, keepdims=True)
                best.append(m)
                cands = [jnp.where(cnd == m, NEG_BIG, cnd) for cnd in cands]
            tau = best[K - 1]
            z = jnp.ones((1, LANES), F32)
            for n in range(1, K):
                z = z + jnp.exp(best[n] - best[0])
            e1_ref[hd, c] = jnp.exp(s1 - v1[0]) / z
            j_ref[hd, c] = _count_ge(s1, v2, tau)
            r2_ref[hd, c] = rank.astype(BF16)
            e2_ref[hd, c] = jnp.exp(s2 - v2[0]).astype(BF16)
        return carry

    lax.fori_loop(0, G, lane_group, 0)


def _peer_route(x, g, sc, sh, wq, k1, k2, T, tm):
    M, D = x.shape
    H, NK, half = k1.shape
    nt = M // tm
    G = tm // LANES
    tab = pl.BlockSpec((None, H, G, NK, LANES), lambda i: (i, 0, 0, 0, 0))
    return pl.pallas_call(
        _peer_route_kernel,
        grid=(M // tm,),
        in_specs=[pl.BlockSpec((tm, D), lambda i: (i, 0)),
                  pl.BlockSpec((1, D), lambda i: (0, 0)),
                  _mod_spec(T, tm, D), _mod_spec(T, tm, D),
                  pl.BlockSpec((D, wq.shape[1]), lambda i: (0, 0)),
                  pl.BlockSpec((H, NK, half), lambda i: (0, 0, 0)),
                  pl.BlockSpec((H, NK, half), lambda i: (0, 0, 0))],
        out_specs=[pl.BlockSpec((None, D, tm), lambda i: (i, 0, 0)), tab, tab, tab, tab],
        out_shape=[jax.ShapeDtypeStruct((nt, D, tm), BF16)]
        + [jax.ShapeDtypeStruct((nt, H, G, NK, LANES), dt) for dt in (F32, F32, BF16, BF16)],
        scratch_shapes=[pltpu.VMEM((H, G, NK, LANES), F32), pltpu.VMEM((H, G, NK, LANES), F32)],
        compiler_params=_params("arbitrary"),
        name="peer_route",
    )(x, g.reshape(1, D), _mod_rows(sc, T, tm), _mod_rows(sh, T, tm), wq, k1, k2)


def _peer_expert_kernel(h_ref, u_ref, vt_ref, e1_ref, j_ref, r2_ref, e2_ref, x_ref, gate_ref,
                        o_ref, acc_s, act_s, g_s, *, A):
    j = pl.program_id(1)
    H, G, NK, _ = r2_ref.shape

    @pl.when(j == 0)
    def _():
        acc_s[...] = jnp.zeros(acc_s.shape, F32)

    act = _dot(u_ref[...], h_ref[...])
    for c in range(G):
        act_s[c] = act[:, c * LANES:(c + 1) * LANES]
    GA = 4
    zero = jnp.zeros((NK, LANES), BF16)
    for c in range(G):
        lanes = slice(c * LANES, (c + 1) * LANES)
        for a0 in range(0, A, GA):
            ws = [zero] * GA
            for hd in range(H):
                r2 = r2_ref[hd, c]
                e2 = e2_ref[hd, c]
                for i in range(GA):
                    al = a0 + i
                    e1 = e1_ref[hd, c, al:al + 1, :].astype(BF16)
                    cnt = j_ref[hd, c, al:al + 1, :].astype(BF16)
                    ws[i] = ws[i] + jnp.where(r2 < cnt, e2, zero) * e1
            for i in range(GA):
                rows = slice((a0 + i) * NK, (a0 + i + 1) * NK)
                g_s[rows, lanes] = ws[i] * _gelu(act_s[c, rows, :]).astype(BF16)
    acc_s[...] += _dot(vt_ref[...], g_s[...])

    @pl.when(j == pl.num_programs(1) - 1)
    def _():
        o_ref[...] = x_ref[...] + gate_ref[...] * acc_s[...].T


EXPERT_TILE = 1024


def _peer_experts(h, u, vt, tabs, x, gate, T, tm, te=EXPERT_TILE):
    M, D = x.shape
    E = u.shape[0]
    _, H, G, NK, _ = tabs[0].shape
    A = te // NK
    assert vt.shape == (E // te, D, te) and G * LANES == tm
    tab = pl.BlockSpec((None, H, G, NK, LANES), lambda i, j: (i, 0, 0, 0, 0))
    tab_a = pl.BlockSpec((None, H, G, A, LANES), lambda i, j: (i, 0, 0, j, 0))
    return pl.pallas_call(
        functools.partial(_peer_expert_kernel, A=A),
        grid=(M // tm, E // te),
        in_specs=[pl.BlockSpec((None, D, tm), lambda i, j: (i, 0, 0)),
                  pl.BlockSpec((te, D), lambda i, j: (j, 0)),
                  pl.BlockSpec((None, D, te), lambda i, j: (j, 0, 0)),
                  tab_a, tab_a, tab, tab,
                  pl.BlockSpec((tm, D), lambda i, j: (i, 0)),
                  _mod_spec2(T, tm, D)],
        out_specs=pl.BlockSpec((tm, D), lambda i, j: (i, 0)),
        out_shape=jax.ShapeDtypeStruct((M, D), F32),
        scratch_shapes=[pltpu.VMEM((D, tm), F32), pltpu.VMEM((G, te, LANES), F32),
                        pltpu.VMEM((te, tm), BF16)],
        compiler_params=_params("arbitrary", "arbitrary"),
        name="peer_experts",
    )(h, u, vt, *tabs, x, _mod_rows(gate, T, tm))


def _mod_spec2(T, tm, D):
    if tm <= T:
        return pl.BlockSpec((None, 1, D), lambda i, j: (i * tm // T, 0, 0))
    return pl.BlockSpec((None, tm, D), lambda i, j: (0, i, 0))


ROW_TILE = 512


def _trunk(x, c, w, ab_wts, cd_wts, peer_wts, cache):
    B, T, D = x.shape
    M = B * T
    tm = min(ROW_TILE, M)
    depth = w['ada_w'].shape[0]
    past_len = 0 if cache is None else cache['page_table'].shape[1] * PAGE_SIZE
    pos = past_len + jnp.arange(T, dtype=jnp.int32)
    mod = _ada(c, w['ada_w'], w['ada_b'])
    xf = x.reshape(M, D)
    ckv_rows, kr_rows, s5r, s5i, k_rows, v_rows, wkvs, shifts = [], [], [], [], [], [], [], []
    for l in range(depth):
        i = l // 2
        sh1, sc1, g1, sh2, sc2, g2 = jnp.split(mod[l], 6, axis=-1)
        if l % 2 == 0:
            lc = None if cache is None else dict(
                ckv=cache['mla_ckv'], kr=cache['mla_krope'], page_table=cache['page_table'], layer=i,
                s5_re=cache['s5_re'][i], s5_im=cache['s5_im'][i])
            xf, ckv, kr, sr, si = _ab_layer(xf, (sh1, sc1, g1), w['norm1_g'][l], pos, B, T, tm, lc,
                                            ab_wts[i])
            ckv_rows.append(ckv.reshape(B, T, -1))
            kr_rows.append(kr.reshape(B, T, -1))
            state_shape = (B,) + w['s5_lambda_re'].shape[1:]
            s5r.append(sr.reshape(state_shape))
            s5i.append(si.reshape(state_shape))
        else:
            lc = None if cache is None else dict(
                sb_k=cache['sb_k'], sb_v=cache['sb_v'], page_table=cache['page_table'], layer=i,
                wkv=cache['wkv'][i], shift=cache['shift'][i])
            xf, k, v, wkv, shift = _cd_layer(xf, (sh1, sc1, g1), w['norm1_g'][l], B, T, tm, lc,
                                             cd_wts[i])
            k_rows.append(k.reshape(B, T, SB_HEADS, SB_HEAD_DIM))
            v_rows.append(v.reshape(B, T, SB_HEADS, SB_HEAD_DIM))
            wkvs.append(wkv)
            shifts.append(shift)
        pw = peer_wts[l]
        h2, *tabs = _peer_route(xf, w['norm2_g'][l], sc2, sh2, pw['wq'], pw['k1'], pw['k2'], T, tm)
        xf = _peer_experts(h2, pw['u'], pw['vt'], tabs, xf, g2, T, tm)
    y = _rmsnorm(xf, w['final_g'], tm).reshape(B, T, D)
    return (y, jnp.stack(ckv_rows), jnp.stack(kr_rows), jnp.stack(k_rows), jnp.stack(v_rows),
            jnp.stack(s5r), jnp.stack(s5i), jnp.stack(wkvs), jnp.stack(shifts))


def kernel(x_prompt, x_sample, c_prompt, c_sample, cache_mla_ckv, cache_mla_krope, cache_sb_k,
           cache_sb_v, page_table, state_s5_re, state_s5_im, state_rwkv_wkv, state_rwkv_shift,
           ada_w, ada_b, norm1_g, norm2_g, final_g, ab_w_in, ab_w_out, s5_lambda_re, s5_lambda_im,
           s5_log_dt, s5_b_re, s5_b_im, s5_c_re, s5_c_im, s5_d, s5_w_glu, s5_b_glu, mla_g_q,
           mla_w_uq, mla_g_kv, mla_w_uk, mla_w_uv, cd_w_in, cd_w_out, rwkv_mu, rwkv_w0, rwkv_w2,
           rwkv_a0, rwkv_a2, rwkv_g2, rwkv_k_k, rwkv_k_a, rwkv_r_k, rwkv_ln_w, rwkv_ln_b,
           peer_w_q, peer_k1, peer_k2, peer_u, peer_v):
    w = dict(ada_w=ada_w, ada_b=ada_b, norm1_g=norm1_g, norm2_g=norm2_g, final_g=final_g,
             s5_lambda_re=s5_lambda_re)
    ab = dict(ab_w_in=ab_w_in, ab_w_out=ab_w_out, s5_lambda_re=s5_lambda_re,
              s5_lambda_im=s5_lambda_im, s5_log_dt=s5_log_dt, s5_b_re=s5_b_re, s5_b_im=s5_b_im,
              s5_c_re=s5_c_re, s5_c_im=s5_c_im, s5_d=s5_d, s5_w_glu=s5_w_glu, s5_b_glu=s5_b_glu,
              mla_g_q=mla_g_q, mla_w_uq=mla_w_uq, mla_g_kv=mla_g_kv, mla_w_uk=mla_w_uk,
              mla_w_uv=mla_w_uv)
    cd = dict(cd_w_in=cd_w_in, cd_w_out=cd_w_out, rwkv_mu=rwkv_mu, rwkv_w0=rwkv_w0,
              rwkv_w2=rwkv_w2, rwkv_a0=rwkv_a0, rwkv_a2=rwkv_a2, rwkv_g2=rwkv_g2,
              rwkv_k_k=rwkv_k_k, rwkv_k_a=rwkv_k_a, rwkv_r_k=rwkv_r_k, rwkv_ln_w=rwkv_ln_w,
              rwkv_ln_b=rwkv_ln_b)
    depth = ada_w.shape[0]
    ab_wts = [_ab_weights({k: v[i] for k, v in ab.items()}) for i in range((depth + 1) // 2)]
    cd_wts = [_cd_weights({k: v[i] for k, v in cd.items()}) for i in range(depth // 2)]
    n_exp, d_model = peer_v.shape[1:]
    value_blocks = lambda v: jnp.transpose(
        v.reshape(n_exp // EXPERT_TILE, EXPERT_TILE, d_model), (0, 2, 1)).astype(BF16)
    peer_wts = [dict(wq=peer_w_q[l].astype(BF16), k1=peer_k1[l].astype(BF16),
                     k2=peer_k2[l].astype(BF16), u=peer_u[l].astype(BF16),
                     vt=value_blocks(peer_v[l])) for l in range(depth)]
    cache = dict(mla_ckv=cache_mla_ckv, mla_krope=cache_mla_krope, sb_k=cache_sb_k, sb_v=cache_sb_v,
                 page_table=page_table, s5_re=state_s5_re, s5_im=state_s5_im, wkv=state_rwkv_wkv,
                 shift=state_rwkv_shift)
    p = _trunk(x_prompt, c_prompt, w, ab_wts, cd_wts, peer_wts, None)
    s = _trunk(x_sample, c_sample, w, ab_wts, cd_wts, peer_wts, cache)
    return (p[0], s[0]) + p[1:] + s[1:]
```

```python
import functools
import math

import jax
import jax.numpy as jnp
from jax import lax
from jax.experimental import pallas as pl
from jax.experimental.pallas import tpu as pltpu

F32 = jnp.float32
BF16 = jnp.bfloat16

NORM_EPS = 1e-6
ROPE_BASE = 10000.0
PAGE_SIZE = 128
S5_GROUP = 16
S5_STATE = 64
MLA_HEADS = 8
MLA_NOPE = 64
MLA_ROPE = 32
MLA_V = 64
MLA_KV_RANK = 256
MLA_KPAD = 384
SB_HEADS = 8
SB_HEAD_DIM = 64
RW_HEADS = 8
RW_HEAD_DIM = 64
RW_DECAY_RANK = 64
RW_A_RANK = 64
RW_GATE_RANK = 128
RW_GN_EPS = 64e-5
PEER_HEADS = 8
PEER_N_KEYS = 128
PEER_TOPK = 16
NEG_BIG = -3.0e38

VMEM_LIMIT = 48 * 1024 * 1024


def _params(*sem):
    return pltpu.CompilerParams(dimension_semantics=sem, vmem_limit_bytes=VMEM_LIMIT)


def _dot(a, b):
    return jnp.dot(a, b, preferred_element_type=F32)


def _dot_nt(a, b):
    return lax.dot_general(a, b, (((1,), (1,)), ((), ())), preferred_element_type=F32)


def _dot_tn(a, b):
    return lax.dot_general(a, b, (((0,), (0,)), ((), ())), preferred_element_type=F32)


def _split3(a):
    hi = a.astype(BF16)
    r1 = a - hi.astype(F32)
    mid = r1.astype(BF16)
    lo = (r1 - mid.astype(F32)).astype(BF16)
    return hi, mid, lo


def _dotx(dot, a, b):
    a0, a1, a2 = _split3(a)
    b0, b1, b2 = _split3(b)
    return (dot(a0, b0) + (dot(a0, b1) + dot(a1, b0))
            + (dot(a1, b1) + dot(a0, b2) + dot(a2, b0)))


def _dot_exact_rhs(a, b_bf16):
    a0, a1, a2 = _split3(a)
    return _dot(a0, b_bf16) + _dot(a1, b_bf16) + _dot(a2, b_bf16)


def _gelu(x):
    c = math.sqrt(2.0 / math.pi)
    return 0.5 * x * (1.0 + jnp.tanh(c * (x + 0.044715 * (x * x * x))))


def _sigmoid(x):
    return 1.0 / (1.0 + jnp.exp(-x))


def _softplus(x):
    return jnp.maximum(x, 0.0) + jnp.log1p(jnp.exp(-jnp.abs(x)))


def _rms(x):
    return x * lax.rsqrt(jnp.mean(x * x, axis=-1, keepdims=True) + NORM_EPS)


def _ada_kernel(c_ref, w_ref, b_ref, o_ref):
    c = c_ref[...]
    s = (c * _sigmoid(c)).astype(BF16)
    o_ref[...] = _dot(s, w_ref[...].astype(BF16)) + b_ref[...]


def _ada(c, ada_w, ada_b):
    L, D, N = ada_w.shape
    Bc = c.shape[0]
    tn = 1024
    return pl.pallas_call(
        _ada_kernel,
        grid=(L, N // tn),
        in_specs=[pl.BlockSpec((Bc, D), lambda l, j: (0, 0)),
                  pl.BlockSpec((None, D, tn), lambda l, j: (l, 0, j)),
                  pl.BlockSpec((None, 1, tn), lambda l, j: (l, 0, j))],
        out_specs=pl.BlockSpec((None, Bc, tn), lambda l, j: (l, 0, j)),
        out_shape=jax.ShapeDtypeStruct((L, Bc, N), F32),
        compiler_params=_params("arbitrary", "arbitrary"),
        name="ada",
    )(c, ada_w, ada_b.reshape(L, 1, N))


def _mod_spec(T, tm, D):
    if tm <= T:
        return pl.BlockSpec((None, 1, D), lambda i: (i * tm // T, 0, 0))
    return pl.BlockSpec((None, tm, D), lambda i: (0, i, 0))


def _mod_rows(m, T, tm):
    if tm <= T:
        return m[:, None, :]
    return jnp.repeat(m, T, axis=0)[None]


def _norm_mod_mm_kernel(x_ref, g_ref, sc_ref, sh_ref, w_ref, *o_refs):
    h = _rms(x_ref[...]) * g_ref[...] * (1.0 + sc_ref[...]) + sh_ref[...]
    res = _dot(h.astype(BF16), w_ref[...])
    off = 0
    for o_ref in o_refs:
        n = o_ref.shape[1]
        o_ref[...] = res[:, off:off + n]
        off += n


def _time_major_spec(T, tm, n):
    assert tm <= T
    return pl.BlockSpec((tm, n), lambda i: (i % (T // tm), i * tm // T))


def _norm_mod_mm(x, g, sc, sh, w, T, tm, widths, time_major=()):
    M, D = x.shape
    N = w.shape[1]
    assert sum(widths) == N
    specs = [_time_major_spec(T, tm, n) if k in time_major else pl.BlockSpec((tm, n), lambda i: (i, 0))
             for k, n in enumerate(widths)]
    shapes = [jax.ShapeDtypeStruct((T, M // T * n) if k in time_major else (M, n), F32)
              for k, n in enumerate(widths)]
    return pl.pallas_call(
        _norm_mod_mm_kernel,
        grid=(M // tm,),
        in_specs=[pl.BlockSpec((tm, D), lambda i: (i, 0)),
                  pl.BlockSpec((1, D), lambda i: (0, 0)),
                  _mod_spec(T, tm, D), _mod_spec(T, tm, D),
                  pl.BlockSpec((D, N), lambda i: (0, 0))],
        out_specs=specs,
        out_shape=shapes,
        compiler_params=_params("arbitrary"),
        name="norm_mod_mm",
    )(x, g.reshape(1, D), _mod_rows(sc, T, tm), _mod_rows(sh, T, tm), w)


def _mm_resid_kernel(a1_ref, a2_ref, w1_ref, w2_ref, x_ref, gate_ref, o_ref):
    mixed = (_dot(a1_ref[...].astype(BF16), w1_ref[...])
             + _dot(a2_ref[...].astype(BF16), w2_ref[...]))
    o_ref[...] = x_ref[...] + gate_ref[...] * mixed


def _mm_resid(a1, a2, w, x, gate, T, tm, a1_time_major=False):
    M, D = x.shape
    K2 = a2.shape[1]
    K1 = w.shape[0] - K2
    a1_spec = _time_major_spec(T, tm, K1) if a1_time_major else pl.BlockSpec((tm, K1), lambda i: (i, 0))
    return pl.pallas_call(
        _mm_resid_kernel,
        grid=(M // tm,),
        in_specs=[a1_spec,
                  pl.BlockSpec((tm, K2), lambda i: (i, 0)),
                  pl.BlockSpec((K1, D), lambda i: (0, 0)),
                  pl.BlockSpec((K2, D), lambda i: (0, 0)),
                  pl.BlockSpec((tm, D), lambda i: (i, 0)),
                  _mod_spec(T, tm, D)],
        out_specs=pl.BlockSpec((tm, D), lambda i: (i, 0)),
        out_shape=jax.ShapeDtypeStruct((M, D), F32),
        compiler_params=_params("arbitrary"),
        name="mm_resid",
    )(a1, a2, w[:K1], w[K1:], x, _mod_rows(gate, T, tm))


def _rmsnorm_kernel(x_ref, g_ref, o_ref):
    o_ref[...] = _rms(x_ref[...]) * g_ref[...]


def _rmsnorm(x, g, tm):
    M, D = x.shape
    return pl.pallas_call(
        _rmsnorm_kernel,
        grid=(M // tm,),
        in_specs=[pl.BlockSpec((tm, D), lambda i: (i, 0)),
                  pl.BlockSpec((1, D), lambda i: (0, 0))],
        out_specs=pl.BlockSpec((tm, D), lambda i: (i, 0)),
        out_shape=jax.ShapeDtypeStruct((M, D), F32),
        compiler_params=_params("arbitrary"),
        name="final_norm",
    )(x, g.reshape(1, D))


def _s5_kernel(u_ref, s0r_ref, s0i_ref, abr_ref, abi_ref, cor_ref, coi_ref,
               wbr_ref, wbi_ref, wcr_ref, wci_ref, d_ref, wg_ref, bg_ref,
               o_ref, sr_ref, si_ref, xr_buf, xi_buf, *, nb, tc):
    c = pl.program_id(0)

    @pl.when(c == 0)
    def _():
        sr_ref[...] = s0r_ref[...]
        si_ref[...] = s0i_ref[...]

    u = u_ref[...]
    ub = u.astype(BF16)
    bu_r = _dot(ub, wbr_ref[...])
    bu_i = _dot(ub, wbi_ref[...])
    cor, coi = cor_ref[...], coi_ref[...]
    xr_buf[...] = cor * bu_r - coi * bu_i
    xi_buf[...] = cor * bu_i + coi * bu_r
    abr, abi = abr_ref[...], abi_ref[...]

    def step(t, carry):
        xr, xi = carry
        rows = pl.ds(pl.multiple_of(t * nb, nb), nb)
        nr = abr * xr - abi * xi + xr_buf[rows, :]
        ni = abr * xi + abi * xr + xi_buf[rows, :]
        xr_buf[rows, :] = nr
        xi_buf[rows, :] = ni
        return nr, ni

    xr, xi = lax.fori_loop(0, tc, step, (sr_ref[...], si_ref[...]))
    sr_ref[...] = xr
    si_ref[...] = xi

    y = (_dot(xr_buf[...].astype(BF16), wcr_ref[...])
         - _dot(xi_buf[...].astype(BF16), wci_ref[...]) + d_ref[...] * u)
    z = _gelu(y)
    o_ref[...] = z * _sigmoid(_dot(z.astype(BF16), wg_ref[...]) + bg_ref[...])


def _s5_consts(lam_re, lam_im, log_dt, b_re, b_im, c_re, c_im):
    G, P = lam_re.shape
    Hc = b_re.shape[-1]
    dt = jnp.exp(log_dt)[:, None]
    mag = jnp.exp(lam_re * dt)
    ab_re, ab_im = mag * jnp.cos(lam_im * dt), mag * jnp.sin(lam_im * dt)
    den = lam_re * lam_re + lam_im * lam_im
    nr = ab_re - 1.0
    co_re = (nr * lam_re + ab_im * lam_im) / den
    co_im = (ab_im * lam_re - nr * lam_im) / den
    eye = jnp.eye(G, dtype=F32)
    wb = lambda b: jnp.einsum('gph,gk->ghkp', b, eye).reshape(G * Hc, G * P).astype(BF16)
    wc = lambda cc: jnp.einsum('ghp,gk->gpkh', cc, eye).reshape(G * P, G * Hc).astype(BF16)
    flat = lambda a: a.reshape(1, G * P)
    return (flat(ab_re), flat(ab_im), flat(co_re), flat(co_im),
            wb(b_re), wb(b_im), wc(c_re), wc(c_im))


def _s5(u_tb, s0_re, s0_im, consts, d, w_glu, b_glu, nb, tc):
    M, W = u_tb.shape
    T = M // nb
    GP = s0_re.shape[1]
    ab_re, ab_im, co_re, co_im, wbr, wbi, wcr, wci = consts
    rows = tc * nb
    full = lambda shape: pl.BlockSpec(shape, lambda c: (0,) * len(shape))
    return pl.pallas_call(
        functools.partial(_s5_kernel, nb=nb, tc=tc),
        grid=(T // tc,),
        in_specs=[pl.BlockSpec((rows, W), lambda c: (c, 0)),
                  full((nb, GP)), full((nb, GP)),
                  full((1, GP)), full((1, GP)), full((1, GP)), full((1, GP)),
                  full((W, GP)), full((W, GP)), full((GP, W)), full((GP, W)),
                  full((1, W)), full((W, W)), full((1, W))],
        out_specs=[pl.BlockSpec((rows, W), lambda c: (c, 0)), full((nb, GP)), full((nb, GP))],
        out_shape=[jax.ShapeDtypeStruct((M, W), F32),
                   jax.ShapeDtypeStruct((nb, GP), F32),
                   jax.ShapeDtypeStruct((nb, GP), F32)],
        scratch_shapes=[pltpu.VMEM((rows, GP), F32), pltpu.VMEM((rows, GP), F32)],
        compiler_params=_params("arbitrary"),
        name="s5",
    )(u_tb, s0_re, s0_im, ab_re, ab_im, co_re, co_im, wbr, wbi, wcr, wci,
      d.reshape(1, W), w_glu.astype(BF16), b_glu.reshape(1, W))


def _rope_tables(pos, width):
    half = MLA_ROPE // 2
    inv = ROPE_BASE ** (-jnp.arange(half, dtype=F32) / half)
    ang = pos.astype(F32)[:, None] * inv[None, :]
    cos, sin = jnp.cos(ang), jnp.sin(ang)
    pad = jnp.zeros((pos.shape[0], width - MLA_ROPE), F32)
    return (jnp.concatenate([cos, cos, pad], axis=1), jnp.concatenate([-sin, sin, pad], axis=1))


def _swap_halves(w):
    half = w.shape[-1] // 2
    return jnp.concatenate([w[..., half:], w[..., :half]], axis=-1)


def _pad_cols(w, width):
    return jnp.pad(w, [(0, 0)] * (w.ndim - 1) + [(0, width - w.shape[-1])])


def _mla_prep_kernel(cq_ref, ckv_ref, kr_ref, krs_ref, c2_ref, s2_ref, gq_ref, gkv_ref,
                     wq_ref, wuk_ref, q_ref, kcat_ref, ckvn_ref, krot_ref):
    H = q_ref.shape[0]
    c2, s2 = c2_ref[...], s2_ref[...]
    cqn = (_rms(cq_ref[...]) * gq_ref[...]).astype(BF16)
    qall = _dot(cqn, wq_ref[...])
    for h in range(H):
        base = h * 384
        nope = qall[:, base:base + 128].astype(BF16)
        rot = qall[:, base + 128:base + 256] * c2 + qall[:, base + 256:base + 384] * s2
        q_ref[h, :, 0:MLA_KV_RANK] = _dot(nope, wuk_ref[h]).astype(BF16)
        q_ref[h, :, MLA_KV_RANK:MLA_KPAD] = rot.astype(BF16)
    ckvn = _rms(ckv_ref[...]) * gkv_ref[...]
    krot = kr_ref[...] * c2 + krs_ref[...] * s2
    ckvn_ref[...] = ckvn
    krot_ref[...] = krot[:, :MLA_ROPE]
    kcat_ref[:, 0:MLA_KV_RANK] = ckvn.astype(BF16)
    kcat_ref[:, MLA_KV_RANK:MLA_KPAD] = krot.astype(BF16)


def _mla_prep(cq, ckv, kr, krs, c2, s2, g_q, g_kv, wq, wuk, tm):
    M = cq.shape[0]
    H = MLA_HEADS
    col = lambda w: pl.BlockSpec((tm, w), lambda i: (i, 0))
    full = lambda shape: pl.BlockSpec(shape, lambda i: (0,) * len(shape))
    return pl.pallas_call(
        _mla_prep_kernel,
        grid=(M // tm,),
        in_specs=[col(384), col(256), col(128), col(128), col(128), col(128),
                  full((1, 384)), full((1, 256)), full((384, H * 384)), full((H, 128, 256))],
        out_specs=[pl.BlockSpec((H, tm, MLA_KPAD), lambda i: (0, i, 0)),
                   col(MLA_KPAD), col(MLA_KV_RANK), col(MLA_ROPE)],
        out_shape=[jax.ShapeDtypeStruct((H, M, MLA_KPAD), BF16),
                   jax.ShapeDtypeStruct((M, MLA_KPAD), BF16),
                   jax.ShapeDtypeStruct((M, MLA_KV_RANK), F32),
                   jax.ShapeDtypeStruct((M, MLA_ROPE), F32)],
        compiler_params=_params("arbitrary"),
        name="mla_prep",
    )(cq, ckv, kr, krs, c2, s2, g_q.reshape(1, -1), g_kv.reshape(1, -1), wq, wuk)


def _mla_weights(w_uq, w_uk):
    C, H, _ = w_uq.shape
    nope = _pad_cols(w_uq[..., :MLA_NOPE], 128)
    rope = w_uq[..., MLA_NOPE:]
    wq = jnp.concatenate([nope, _pad_cols(rope, 128), _pad_cols(_swap_halves(rope), 128)], axis=-1)
    wuk = jnp.pad(jnp.transpose(w_uk, (1, 2, 0)), ((0, 0), (0, 128 - MLA_NOPE), (0, 0)))
    return wq.reshape(C, H * 384).astype(BF16), wuk.astype(BF16)


def _mla_attn_kernel(q_ref, k_ref, wuv_ref, o_ref, m_s, l_s, acc_s, *, tq, tk, scale):
    qi = pl.program_id(1)
    H = q_ref.shape[0]
    rows = H * tq
    q = q_ref[...].reshape(rows, MLA_KPAD)
    m_s[...] = jnp.full((rows, 1), NEG_BIG, F32)
    l_s[...] = jnp.zeros((rows, 1), F32)
    acc_s[...] = jnp.zeros((rows, MLA_KV_RANK), F32)
    row_t = qi * tq + lax.broadcasted_iota(jnp.int32, (H, tq, 1), 1).reshape(rows, 1)
    groups = [slice(g * rows // 2, (g + 1) * rows // 2) for g in range(2)]

    def block(kb, masked):
        k = k_ref[pl.ds(pl.multiple_of(kb * tk, tk), tk), :]
        vals = k[:, :MLA_KV_RANK]
        s = [_dot_nt(q[g], k) * scale for g in groups]
        if masked:
            kpos = kb * tk + lax.broadcasted_iota(jnp.int32, (1, tk), 1)
            s = [jnp.where(kpos <= row_t[g], sg, NEG_BIG) for g, sg in zip(groups, s)]
        m_prev = [m_s[g] for g in groups]
        m_new = [jnp.maximum(mp, jnp.max(sg, axis=-1, keepdims=True)) for mp, sg in zip(m_prev, s)]
        alpha = [jnp.exp(mp - mn) for mp, mn in zip(m_prev, m_new)]
        p = [jnp.exp(sg - mn) for sg, mn in zip(s, m_new)]
        pv = [_dot(pg.astype(BF16), vals) for pg in p]
        for g, al, pg, pvg, mn in zip(groups, alpha, p, pv, m_new):
            l_s[g] = al * l_s[g] + jnp.sum(pg, axis=-1, keepdims=True)
            acc_s[g] = al * acc_s[g] + pvg
            m_s[g] = mn

    n_full = (qi * tq) // tk

    def body(kb, carry):
        block(kb, False)
        return carry

    lax.fori_loop(0, n_full, body, 0)
    block(n_full, True)
    o = (acc_s[...] / l_s[...]).astype(BF16)
    for h in range(H):
        o_ref[:, h * MLA_V:(h + 1) * MLA_V] = _dot(o[h * tq:(h + 1) * tq], wuv_ref[h])


def _mla_attn(q, kcat, wuv, B, T, tq, tk):
    H = q.shape[0]
    nq = T // tq
    scale = (MLA_NOPE + MLA_ROPE) ** -0.5
    return pl.pallas_call(
        functools.partial(_mla_attn_kernel, tq=tq, tk=tk, scale=scale),
        grid=(B, nq),
        in_specs=[pl.BlockSpec((H, tq, MLA_KPAD), lambda b, i: (0, b * nq + i, 0)),
                  pl.BlockSpec((T, MLA_KPAD), lambda b, i: (b, 0)),
                  pl.BlockSpec((H, MLA_KV_RANK, MLA_V), lambda b, i: (0, 0, 0))],
        out_specs=pl.BlockSpec((tq, H * MLA_V), lambda b, i: (b * nq + i, 0)),
        out_shape=jax.ShapeDtypeStruct((B * T, H * MLA_V), F32),
        scratch_shapes=[pltpu.VMEM((H * tq, 1), F32), pltpu.VMEM((H * tq, 1), F32),
                        pltpu.VMEM((H * tq, MLA_KV_RANK), F32)],
        compiler_params=_params("arbitrary", "arbitrary"),
        name="mla_attn",
    )(q, kcat, wuv)


def _mla_decode_kernel(pt_ref, q_ref, knew_ref, *refs, npg, nt, scale):
    ckv_refs = refs[:npg]
    kr_refs = refs[npg:2 * npg]
    o_ref, m_s, l_s, acc_s, kbuf, rbuf = refs[2 * npg:]
    j = pl.program_id(1)
    q = q_ref[...]
    rows = q.shape[0]

    def update(s, vals):
        m_prev = m_s[...]
        m_new = jnp.maximum(m_prev, jnp.max(s, axis=-1, keepdims=True))
        alpha = jnp.exp(m_prev - m_new)
        p = jnp.exp(s - m_new)
        l_s[...] = alpha * l_s[...] + jnp.sum(p, axis=-1, keepdims=True)
        acc_s[...] = alpha * acc_s[...] + _dot(p.astype(BF16), vals)
        m_s[...] = m_new

    @pl.when(j == 0)
    def _():
        m_s[...] = jnp.full((rows, 1), NEG_BIG, F32)
        l_s[...] = jnp.zeros((rows, 1), F32)
        acc_s[...] = jnp.zeros((rows, MLA_KV_RANK), F32)
        knew = knew_ref[...]
        s = _dot_nt(q, knew) * scale
        t_row = lax.broadcasted_iota(jnp.int32, (rows, 1), 0) % nt
        col = lax.broadcasted_iota(jnp.int32, (1, PAGE_SIZE), 1)
        update(jnp.where(col <= t_row, s, NEG_BIG), knew[:, :MLA_KV_RANK])

    for r in range(npg):
        rows_r = slice(r * PAGE_SIZE, (r + 1) * PAGE_SIZE)
        kbuf[rows_r, :] = ckv_refs[r][...].astype(BF16)
        rbuf[:, rows_r] = kr_refs[r][...].astype(BF16)
    lat = kbuf[...]
    s = _dot_nt(q[:, :MLA_KV_RANK], lat) + _dot(q[:, MLA_KV_RANK:MLA_KV_RANK + MLA_ROPE], rbuf[...])
    update(s * scale, lat)

    @pl.when(j == pl.num_programs(1) - 1)
    def _():
        o_ref[...] = acc_s[...] / l_s[...]


def _mla_decode(q, knew, cache_ckv, cache_kr, page_table, layer, nt, npg=16):
    B, rows, _ = q.shape
    n_pages = page_table.shape[1]
    page = lambda r, a, c: pl.BlockSpec(
        (None, None, a, c), lambda b, j, pt: (layer, pt[b, j * npg + r], 0, 0))
    grid_spec = pltpu.PrefetchScalarGridSpec(
        num_scalar_prefetch=1,
        grid=(B, n_pages // npg),
        in_specs=[pl.BlockSpec((None, rows, MLA_KPAD), lambda b, j, pt: (b, 0, 0)),
                  pl.BlockSpec((None, PAGE_SIZE, MLA_KPAD), lambda b, j, pt: (b, 0, 0))]
        + [page(r, PAGE_SIZE, MLA_KV_RANK) for r in range(npg)]
        + [page(r, MLA_ROPE, PAGE_SIZE) for r in range(npg)],
        out_specs=pl.BlockSpec((None, rows, MLA_KV_RANK), lambda b, j, pt: (b, 0, 0)),
        scratch_shapes=[pltpu.VMEM((rows, 1), F32), pltpu.VMEM((rows, 1), F32),
                        pltpu.VMEM((rows, MLA_KV_RANK), F32),
                        pltpu.VMEM((npg * PAGE_SIZE, MLA_KV_RANK), BF16),
                        pltpu.VMEM((MLA_ROPE, npg * PAGE_SIZE), BF16)])
    scale = (MLA_NOPE + MLA_ROPE) ** -0.5
    return pl.pallas_call(
        functools.partial(_mla_decode_kernel, npg=npg, nt=nt, scale=scale),
        grid_spec=grid_spec,
        out_shape=jax.ShapeDtypeStruct((B, rows, MLA_KV_RANK), F32),
        compiler_params=_params("arbitrary", "arbitrary"),
        name="mla_decode",
    )(page_table, q, knew, *([cache_ckv] * npg), *([cache_kr] * npg))


def _bmm_kernel(x_ref, w_ref, o_ref):
    o_ref[...] = _dot(x_ref[...].astype(BF16), w_ref[...])


def _bmm(x, w):
    H, M, K = x.shape
    N = w.shape[2]
    return pl.pallas_call(
        _bmm_kernel,
        grid=(H,),
        in_specs=[pl.BlockSpec((None, M, K), lambda h: (h, 0, 0)),
                  pl.BlockSpec((None, K, N), lambda h: (h, 0, 0))],
        out_specs=pl.BlockSpec((None, M, N), lambda h: (h, 0, 0)),
        out_shape=jax.ShapeDtypeStruct((H, M, N), F32),
        compiler_params=_params("arbitrary"),
        name="bmm",
    )(x, w)


def _ab_in_weights(w_in, s5_width):
    a, b, c = s5_width, s5_width + 384, s5_width + 384 + MLA_KV_RANK
    u, cq, ckv, kr = w_in[:, :a], w_in[:, a:b], w_in[:, b:c], w_in[:, c:]
    return jnp.concatenate(
        [cq, u, ckv, _pad_cols(kr, 128), _pad_cols(_swap_halves(kr), 128)], axis=1).astype(BF16)


def _ab_layer(x, mods, g_norm, pos, B, T, tm, cache, wts, tq=128, tk=512):
    sh1, sc1, g1 = mods
    M = B * T
    W = wts['s5_d'].size
    GP = wts['s5_consts'][0].size
    time_major = tm <= T
    cq, u, ckv, kr, krs = _norm_mod_mm(x, g_norm, sc1, sh1, wts['w_in'], T, tm,
                                       (384, W, MLA_KV_RANK, 128, 128),
                                       time_major=(1,) if time_major else ())
    if time_major:
        u_tb = u.reshape(M, W)
    else:
        u_tb = jnp.transpose(u.reshape(B, T, W), (1, 0, 2)).reshape(M, W)
    if cache is None:
        s0_re = jnp.zeros((B, GP), F32)
        s0_im = jnp.zeros((B, GP), F32)
    else:
        s0_re, s0_im = cache['s5_re'].reshape(B, GP), cache['s5_im'].reshape(B, GP)
    s5_tb, s_re, s_im = _s5(u_tb, s0_re, s0_im, wts['s5_consts'], wts['s5_d'], wts['s5_w_glu'],
                            wts['s5_b_glu'], nb=B, tc=min(T, 64))
    if time_major:
        s5_out = s5_tb.reshape(T, B * W)
    else:
        s5_out = jnp.transpose(s5_tb.reshape(T, B, W), (1, 0, 2)).reshape(M, W)
    c2, s2 = _rope_tables(pos, 128)
    c2, s2 = jnp.tile(c2, (B, 1)), jnp.tile(s2, (B, 1))
    q, kcat, ckvn, krot = _mla_prep(cq, ckv, kr, krs, c2, s2, wts['mla_g_q'], wts['mla_g_kv'],
                                    wts['mla_wq'], wts['mla_wuk'], tm)
    H = MLA_HEADS
    if cache is None:
        o = _mla_attn(q, kcat, wts['mla_wuv'], B, T, tq, min(tk, T))
    else:
        qd = jnp.transpose(q.reshape(H, B, T, MLA_KPAD), (1, 0, 2, 3)).reshape(B, H * T, MLA_KPAD)
        knew = jnp.pad(kcat.reshape(B, T, MLA_KPAD), ((0, 0), (0, PAGE_SIZE - T), (0, 0)))
        o_lat = _mla_decode(qd, knew, cache['ckv'], jnp.transpose(cache['kr'], (0, 1, 3, 2)),
                            cache['page_table'],
                            cache['layer'], T)
        o_lat = jnp.transpose(o_lat.reshape(B, H, T, MLA_KV_RANK), (1, 0, 2, 3)).reshape(H, M, -1)
        o = jnp.transpose(_bmm(o_lat, wts['mla_wuv']), (1, 0, 2)).reshape(M, H * MLA_V)
    x = _mm_resid(s5_out, o, wts['w_out'], x, g1, T, tm, a1_time_major=time_major)
    return x, ckvn, krot, s_re, s_im


def _suffix_sums(ln, tri):
    hi = ln.astype(BF16)
    mid = (ln - hi.astype(F32)).astype(BF16)
    return _dot(hi, tri) + _dot(mid, tri)


def _sb_attn_kernel(q_ref, k_ref, v_ref, tri_ref, o_ref, acc_s, r_s, *, tq, scale):
    qi = pl.program_id(2)
    hs = range(acc_s.shape[0])
    d = acc_s.shape[2]
    tri = tri_ref[...]
    acc_s[...] = jnp.zeros(acc_s.shape, F32)
    r_s[...] = jnp.zeros(r_s.shape, F32)
    heads = lambda a: [a[:, h * d:(h + 1) * d].astype(BF16) for h in hs]
    q = heads(q_ref[...])

    def block(kb, diag):
        rows = pl.ds(pl.multiple_of(kb * tq, tq), tq)
        k = heads(k_ref[rows, :])
        v = heads(v_ref[rows, :])
        z = [_dot_nt(q[h], k[h]) * scale for h in hs]
        sp = [_softplus(a) for a in z]
        if diag:
            mask = (lax.broadcasted_iota(jnp.int32, (tq, tq), 1)
                    < lax.broadcasted_iota(jnp.int32, (tq, tq), 0))
            ln = [jnp.where(mask, -a, 0.0) for a in sp]
        else:
            ln = [-a for a in sp]
        a = [jnp.exp(z[h] - sp[h] + _suffix_sums(ln[h], tri) + r_s[h]) for h in hs]
        if diag:
            a = [jnp.where(mask, ah, 0.0) for ah in a]
        for h in hs:
            acc_s[h] += _dot(a[h].astype(BF16), v[h])
            r_s[h] += jnp.sum(ln[h], axis=-1, keepdims=True)

    block(qi, True)

    def body(i, carry):
        block(qi - 1 - i, False)
        return carry

    lax.fori_loop(0, qi, body, 0)
    for h in hs:
        o_ref[:, h * d:(h + 1) * d] = acc_s[h]


def _tri(n):
    return (lax.broadcasted_iota(jnp.int32, (n, n), 0)
            > lax.broadcasted_iota(jnp.int32, (n, n), 1)).astype(BF16)


def _sb_attn(q, k, v, B, T, d, tq):
    M, W = q.shape
    hb = LANES // d
    nq = T // tq
    seq = pl.BlockSpec((T, LANES), lambda b, h, i: (b, h))
    blk = pl.BlockSpec((tq, LANES), lambda b, h, i: (b * nq + i, h))
    return pl.pallas_call(
        functools.partial(_sb_attn_kernel, tq=tq, scale=d ** -0.5),
        grid=(B, W // LANES, nq),
        in_specs=[blk, seq, seq, pl.BlockSpec((tq, tq), lambda b, h, i: (0, 0))],
        out_specs=blk,
        out_shape=jax.ShapeDtypeStruct((M, W), F32),
        scratch_shapes=[pltpu.VMEM((hb, tq, d), F32), pltpu.VMEM((hb, tq, 1), F32)],
        compiler_params=_params("arbitrary", "arbitrary", "arbitrary"),
        name="sb_attn",
    )(q, k, v, _tri(tq))


def _sb_decode_kernel(pt_ref, q_ref, knew_ref, vnew_ref, tw_ref, *refs, npg, scale):
    k_refs = refs[:npg]
    v_refs = refs[npg:2 * npg]
    o_ref, acc_s, r_s, kbuf, vbuf = refs[2 * npg:]
    j = pl.program_id(1)
    H, qrows, d = q_ref.shape
    hs = range(H)
    tw = tw_ref[...]

    def block(keys, vals, n, fresh):
        cols = n * PAGE_SIZE
        z = jnp.concatenate([_dot(q_ref[h], keys[h]) for h in hs], axis=0) * scale
        sp = _softplus(z)
        if fresh:
            t_row = lax.broadcasted_iota(jnp.int32, (H * qrows, cols), 0) % qrows
            mask = lax.broadcasted_iota(jnp.int32, (H * qrows, cols), 1) < t_row
            ln = jnp.where(mask, -sp, 0.0)
        else:
            ln = -sp
        hi = ln.astype(BF16)
        mid = (ln - hi.astype(F32)).astype(BF16)
        run = r_s[...]
        after = []
        for p in range(n):
            sl = slice(p * PAGE_SIZE, (p + 1) * PAGE_SIZE)
            both = _dot(hi[:, sl], tw) + _dot(mid[:, sl], tw)
            after.append(both[:, :LANES] + run)
            run = run + both[:, LANES:]
        a = jnp.exp(z - sp + jnp.concatenate(after, axis=1))
        if fresh:
            a = jnp.where(mask, a, 0.0)
        a = a.astype(BF16)
        for h in hs:
            acc_s[h] += _dot_nt(a[h * qrows:(h + 1) * qrows], vals[h])
        r_s[...] = run

    @pl.when(j == 0)
    def _():
        acc_s[...] = jnp.zeros(acc_s.shape, F32)
        r_s[...] = jnp.zeros(r_s.shape, F32)
        block(knew_ref, vnew_ref, 1, True)

    for r in range(npg):
        lanes_r = slice(r * PAGE_SIZE, (r + 1) * PAGE_SIZE)
        kbuf[:, :, lanes_r] = k_refs[r][...].astype(BF16)
        vbuf[:, :, lanes_r] = v_refs[r][...].astype(BF16)
    block(kbuf, vbuf, npg, False)

    @pl.when(j == pl.num_programs(1) - 1)
    def _():
        o_ref[...] = acc_s[...]


QROWS = 8


def _sb_decode(q, knew, vnew, cache_kt, cache_vt, page_table, layer, npg=16):
    B, H, qrows, d = q.shape
    assert PAGE_SIZE == LANES
    n_pages = page_table.shape[1]
    page = lambda r: pl.BlockSpec(
        (None, None, H, d, PAGE_SIZE),
        lambda b, j, pt: (layer, pt[b, n_pages - 1 - (j * npg + r)], 0, 0, 0))
    per_b = lambda a, c: pl.BlockSpec((None, H, a, c), lambda b, j, pt: (b, 0, 0, 0))
    ti = lax.broadcasted_iota(jnp.int32, (LANES, 2 * LANES), 0)
    si = lax.broadcasted_iota(jnp.int32, (LANES, 2 * LANES), 1)
    tw = ((ti > si) | (si >= LANES)).astype(BF16)
    grid_spec = pltpu.PrefetchScalarGridSpec(
        num_scalar_prefetch=1,
        grid=(B, n_pages // npg),
        in_specs=[per_b(qrows, d), per_b(d, PAGE_SIZE), per_b(d, PAGE_SIZE),
                  pl.BlockSpec((LANES, 2 * LANES), lambda b, j, pt: (0, 0))]
        + [page(r) for r in range(npg)] * 2,
        out_specs=per_b(qrows, d),
        scratch_shapes=[pltpu.VMEM((H, qrows, d), F32), pltpu.VMEM((H * qrows, LANES), F32),
                        pltpu.VMEM((H, d, npg * PAGE_SIZE), BF16),
                        pltpu.VMEM((H, d, npg * PAGE_SIZE), BF16)])
    return pl.pallas_call(
        functools.partial(_sb_decode_kernel, npg=npg, scale=d ** -0.5),
        grid_spec=grid_spec,
        out_shape=jax.ShapeDtypeStruct((B, H, qrows, d), F32),
        compiler_params=_params("arbitrary", "arbitrary"),
        name="sb_decode",
    )(page_table, q, knew, vnew, tw, *([cache_kt] * npg), *([cache_vt] * npg))


def _rwkv_prep_kernel(p_ref, prev_ref, mu_ref, w0_ref, w2_ref, a0_ref, a2_ref, g2_ref,
                      kkw_ref, ka_ref, ones_ref, r_o, lw_o, k_o, v_o, kk_o, b_o, g_o):
    H, _, N = r_o.shape
    Wd = H * N
    p = p_ref[...]
    ps = p + (prev_ref[...] - p) * mu_ref[...]
    r, k, v = ps[:, :Wd], ps[:, Wd:2 * Wd], ps[:, 2 * Wd:3 * Wd]
    o = 3 * Wd
    xw = ps[:, o:o + RW_DECAY_RANK]
    xa = ps[:, o + RW_DECAY_RANK:o + RW_DECAY_RANK + RW_A_RANK]
    xg = ps[:, o + RW_DECAY_RANK + RW_A_RANK:]
    w_log = -_softplus(-(w0_ref[...] + _dot(jnp.tanh(xw).astype(BF16), w2_ref[...]))) - 0.5
    lw = -jnp.exp(w_log)
    a = _sigmoid(a0_ref[...] + _dot(xa.astype(BF16), a2_ref[...]))
    g = _dot(_sigmoid(xg).astype(BF16), g2_ref[...])
    kk = k * kkw_ref[...]
    ss = _dot_exact_rhs(kk * kk, ones_ref[...])
    kk = kk / jnp.maximum(jnp.sqrt(ss), 1e-12)
    k2 = k * (1.0 + (a - 1.0) * ka_ref[...])
    b = kk * a
    for h in range(H):
        sl = slice(h * N, (h + 1) * N)
        r_o[h] = r[:, sl]
        lw_o[h] = lw[:, sl]
        k_o[h] = k2[:, sl]
        v_o[h] = v[:, sl]
        kk_o[h] = kk[:, sl]
        b_o[h] = b[:, sl]
        g_o[h] = g[:, sl]


def _rwkv_prep(p, prev, wts, tm):
    M, cols = p.shape
    H, N = RW_HEADS, RW_HEAD_DIM
    Wd = H * N
    row = lambda n: pl.BlockSpec((1, n), lambda i: (0, 0))
    mat = lambda a, b: pl.BlockSpec((a, b), lambda i: (0, 0))
    out = pl.BlockSpec((H, tm, N), lambda i: (0, i, 0))
    return pl.pallas_call(
        _rwkv_prep_kernel,
        grid=(M // tm,),
        in_specs=[pl.BlockSpec((tm, cols), lambda i: (i, 0)), pl.BlockSpec((tm, cols), lambda i: (i, 0)),
                  row(cols), row(Wd), mat(RW_DECAY_RANK, Wd), row(Wd), mat(RW_A_RANK, Wd),
                  mat(RW_GATE_RANK, Wd), row(Wd), row(Wd), mat(Wd, Wd)],
        out_specs=[out] * 7,
        out_shape=[jax.ShapeDtypeStruct((H, M, N), F32)] * 7,
        compiler_params=_params("arbitrary"),
        name="rwkv_prep",
    )(p, prev, wts['mu'], wts['w0'], wts['w2'], wts['a0'], wts['a2'], wts['g2'],
      wts['k_k'], wts['k_a'], wts['head_ones'])


def _dot3(dot, a, b):
    a0 = a.astype(BF16)
    a1 = (a - a0.astype(F32)).astype(BF16)
    b0 = b.astype(BF16)
    b1 = (b - b0.astype(F32)).astype(BF16)
    return dot(a0, b0) + (dot(a0, b1) + dot(a1, b0))


SOLVE_BLOCK = 16


def _unit_lower_solve(Ls, Rs, C):
    n = range(len(Ls))
    if C <= SOLVE_BLOCK:
        Us = list(Rs)
        for s in range(C - 1):
            Us = [Us[i] - Ls[i][:, s:s + 1] * Us[i][s:s + 1, :] for i in n]
        return Us
    nb = C // SOLVE_BLOCK
    assert nb * SOLVE_BLOCK == C and nb <= 4
    ti = lax.broadcasted_iota(jnp.int32, (C, C), 0)
    si = lax.broadcasted_iota(jnp.int32, (C, C), 1)
    same = (ti // SOLVE_BLOCK) == (si // SOLVE_BLOCK)
    rep_t = (lax.broadcasted_iota(jnp.int32, (C, SOLVE_BLOCK), 0) % SOLVE_BLOCK
             == lax.broadcasted_iota(jnp.int32, (C, SOLVE_BLOCK), 1))
    rep = (lax.broadcasted_iota(jnp.int32, (SOLVE_BLOCK, C), 1) % SOLVE_BLOCK
           == lax.broadcasted_iota(jnp.int32, (SOLVE_BLOCK, C), 0)).astype(BF16)
    Ld = [jnp.where(same, L, 0.0) for L in Ls]
    Ldc = [_dot_exact_rhs(a, rep_t.astype(BF16)) for a in Ld]
    X = [rep_t.astype(F32) for _ in n]
    for s in range(SOLVE_BLOCK - 1):
        rows = [jnp.broadcast_to(x.reshape(nb, SOLVE_BLOCK, SOLVE_BLOCK)[:, s:s + 1, :],
                                 (nb, SOLVE_BLOCK, SOLVE_BLOCK)).reshape(C, SOLVE_BLOCK) for x in X]
        X = [X[i] - Ldc[i][:, s:s + 1] * rows[i] for i in n]
    T1 = [jnp.where(same, _dot_exact_rhs(x, rep), 0.0) for x in X]
    R1 = [_dot3(_dot, T1[i], Rs[i]) for i in n]
    Mm = [_dot3(_dot, T1[i], Ls[i] - Ld[i]) for i in n]
    M2 = [_dot3(_dot, m, m) for m in Mm]
    Y1 = [R1[i] + _dot3(_dot, M2[i], R1[i]) for i in n]
    return [Y1[i] - _dot3(_dot, Mm[i], Y1[i]) for i in n]


def _rwkv_chunk_kernel(r_ref, lw_ref, k_ref, v_ref, kk_ref, b_ref, g_ref, s0_ref,
                       lnw_ref, lnb_ref, rk_ref, y_ref, s_ref, *, C):
    c = pl.program_id(2)

    @pl.when(c == 0)
    def _():
        s_ref[...] = s0_ref[...]

    hs = range(r_ref.shape[0])
    ti = lax.broadcasted_iota(jnp.int32, (C, C), 0)
    si = lax.broadcasted_iota(jnp.int32, (C, C), 1)
    strict, incl = si < ti, si <= ti
    tril = incl.astype(BF16)
    r, lw, k, v = ([ref[h] for h in hs] for ref in (r_ref, lw_ref, k_ref, v_ref))
    kk, b = ([ref[h] for h in hs] for ref in (kk_ref, b_ref))
    S0 = [s_ref[h] for h in hs]
    parts = [_split3(a) for a in lw]
    cum = [_dot(tril, p0) + _dot(tril, p1) + _dot(tril, p2) for p0, p1, p2 in parts]
    g_in = [jnp.exp(a) for a in cum]
    g_inv = [jnp.exp(-a) for a in cum]
    qt = [kk[h] * jnp.exp(cum[h] - lw[h]) for h in hs]
    bt = [b[h] * g_inv[h] for h in hs]
    kt = [k[h] * g_inv[h] for h in hs]
    rt = [r[h] * g_in[h] for h in hs]
    qr = [jnp.concatenate([qt[h], rt[h]], axis=0) for h in hs]
    bk = [jnp.concatenate([bt[h], kt[h]], axis=0) for h in hs]
    gram = [_dot3(_dot_nt, qr[h], bk[h]) for h in hs]
    Lb = [jnp.where(strict, g[:C, :C], 0.0) for g in gram]
    Lk = [jnp.where(strict, g[:C, C:], 0.0) for g in gram]
    Ab = [jnp.where(incl, g[C:, :C], 0.0) for g in gram]
    Ak = [jnp.where(incl, g[C:, C:], 0.0) for g in gram]
    on_state = [_dot3(_dot_nt, qr[h], S0[h]) for h in hs]
    on_v = [_dot3(_dot, jnp.concatenate([Lk[h], Ak[h]], axis=0), v[h]) for h in hs]
    U = _unit_lower_solve(Lb, [-(on_state[h][:C] + on_v[h][:C]) for h in hs], C)
    y = [on_state[h][C:] + _dot3(_dot, Ab[h], U[h]) + on_v[h][C:] for h in hs]
    for h in hs:
        grown = _dot3(_dot_tn, jnp.concatenate([U[h], v[h]], axis=0), bk[h])
        s_ref[h] = (S0[h] + grown) * g_in[h][C - 1:C, :]
    for h in hs:
        mean = jnp.mean(y[h], axis=-1, keepdims=True)
        var = jnp.mean(jnp.square(y[h] - mean), axis=-1, keepdims=True)
        yn = (y[h] - mean) * lax.rsqrt(var + RW_GN_EPS) * lnw_ref[h] + lnb_ref[h]
        bonus = jnp.sum(r[h] * k[h] * rk_ref[h], axis=-1, keepdims=True) * v[h]
        n = y[h].shape[1]
        y_ref[:, h * n:(h + 1) * n] = (yn + bonus) * g_ref[h]


def _rwkv_chunks(feats, s0, ln_w, ln_b, r_k, B, T, C, Hb=8):
    H, M, N = feats[0].shape
    nc = T // C
    seq = pl.BlockSpec((Hb, C, N), lambda b, hb, c: (hb, b * nc + c, 0))
    out = pl.BlockSpec((C, Hb * N), lambda b, hb, c: (b * nc + c, hb))
    st = pl.BlockSpec((None, Hb, N, N), lambda b, hb, c: (b, hb, 0, 0))
    par = pl.BlockSpec((Hb, 1, N), lambda b, hb, c: (hb, 0, 0))
    return pl.pallas_call(
        functools.partial(_rwkv_chunk_kernel, C=C),
        grid=(B, H // Hb, nc),
        in_specs=[seq] * 7 + [st, par, par, par],
        out_specs=[out, st],
        out_shape=[jax.ShapeDtypeStruct((M, H * N), F32), jax.ShapeDtypeStruct((B, H, N, N), F32)],
        compiler_params=_params("arbitrary", "arbitrary", "arbitrary"),
        name="rwkv_chunks",
    )(*feats, s0, ln_w.reshape(H, 1, N), ln_b.reshape(H, 1, N), r_k.reshape(H, 1, N))


def _cd_weights(w):
    H, N = RW_HEADS, RW_HEAD_DIM
    Wd = H * N
    row = lambda a: a.reshape(1, -1)
    head = jnp.arange(Wd) // N
    return {
        'w_in': w['cd_w_in'].astype(BF16), 'w_out': w['cd_w_out'].astype(BF16),
        'mu': row(w['rwkv_mu']), 'w0': row(w['rwkv_w0']), 'w2': w['rwkv_w2'].astype(BF16),
        'a0': row(w['rwkv_a0']), 'a2': w['rwkv_a2'].astype(BF16), 'g2': w['rwkv_g2'].astype(BF16),
        'k_k': row(w['rwkv_k_k']), 'k_a': row(w['rwkv_k_a']),
        'head_ones': (head[:, None] == head[None, :]).astype(BF16),
        'ln_w': w['rwkv_ln_w'], 'ln_b': w['rwkv_ln_b'], 'r_k': w['rwkv_r_k'],
    }


def _ab_weights(w):
    wq, wuk = _mla_weights(w['mla_w_uq'], w['mla_w_uk'])
    return {
        'w_in': _ab_in_weights(w['ab_w_in'], w['s5_d'].size), 'w_out': w['ab_w_out'].astype(BF16),
        's5_consts': _s5_consts(w['s5_lambda_re'], w['s5_lambda_im'], w['s5_log_dt'],
                                w['s5_b_re'], w['s5_b_im'], w['s5_c_re'], w['s5_c_im']),
        's5_d': w['s5_d'], 's5_w_glu': w['s5_w_glu'], 's5_b_glu': w['s5_b_glu'],
        'mla_g_q': w['mla_g_q'], 'mla_g_kv': w['mla_g_kv'], 'mla_wq': wq, 'mla_wuk': wuk,
        'mla_wuv': jnp.transpose(w['mla_w_uv'], (1, 0, 2)).astype(BF16),
    }


def _cd_layer(x, mods, g_norm, B, T, tm, cache, wts, tq=256):
    sh1, sc1, g1 = mods
    M = B * T
    H, d = SB_HEADS, SB_HEAD_DIM
    Wsb = H * d
    cols = wts['mu'].shape[1]
    q, k, v, rw = _norm_mod_mm(x, g_norm, sc1, sh1, wts['w_in'], T, tm, (Wsb, Wsb, Wsb, cols))
    if cache is None:
        sb = _sb_attn(q, k, v, B, T, d, tq)
    else:
        keys_minor = lambda a: jnp.transpose(a, (0, 1, 3, 4, 2))
        heads = lambda a: jnp.transpose(a.reshape(B, T, H, d), (0, 2, 1, 3))
        qh = jnp.pad(heads(q), ((0, 0), (0, 0), (0, QROWS - T), (0, 0))).astype(BF16)
        fresh = lambda a: jnp.pad(jnp.transpose(heads(a), (0, 1, 3, 2)),
                                  ((0, 0), (0, 0), (0, 0), (0, PAGE_SIZE - T))).astype(BF16)
        sb = _sb_decode(qh, fresh(k), fresh(v), keys_minor(cache['sb_k']),
                        keys_minor(cache['sb_v']), cache['page_table'], cache['layer'])
        sb = jnp.transpose(sb[:, :, :T], (0, 2, 1, 3)).reshape(M, Wsb)
    rw3 = rw.reshape(B, T, cols)
    shift0 = jnp.zeros((B, cols), F32) if cache is None else cache['shift']
    prev = jnp.concatenate([shift0[:, None, :], rw3[:, :-1]], axis=1).reshape(M, cols)
    feats = _rwkv_prep(rw, prev, wts, tm)
    N = RW_HEAD_DIM
    C = min(T, 64)
    if T < 8:
        C = 8
        padt = lambda a: jnp.pad(a.reshape(RW_HEADS, B, T, N),
                                 ((0, 0), (0, 0), (0, C - T), (0, 0))).reshape(RW_HEADS, B * C, N)
        feats = [padt(f) for f in feats]
    s0 = (jnp.zeros((B, RW_HEADS, N, N), F32) if cache is None else cache['wkv'])
    Tp = max(T, C)
    y, wkv = _rwkv_chunks(feats, s0, wts['ln_w'], wts['ln_b'], wts['r_k'], B, Tp, C)
    rw_out = y.reshape(B, Tp, RW_HEADS * N)[:, :T].reshape(M, RW_HEADS * N)
    x = _mm_resid(sb, rw_out, wts['w_out'], x, g1, T, tm)
    return x, k, v, wkv, rw3[:, -1]


LANES = 128


def _top_values(s, n):
    out = []
    rank = jnp.full(s.shape, float(n), F32)
    for i in range(n):
        m = jnp.max(s, axis=0, keepdims=True)
        out.append(m)
        hit = s == m
        rank = jnp.where(hit, float(i), rank)
        s = jnp.where(hit, NEG_BIG, s)
    return out, rank


def _count_ge(s, v, tau):
    assert len(v) == 16
    ge = lambda row: (s + row) >= tau
    t8 = ge(v[7])
    t4 = ge(jnp.where(t8, v[11], v[3]))
    t2 = ge(jnp.where(t8, jnp.where(t4, v[13], v[9]), jnp.where(t4, v[5], v[1])))
    hi = jnp.where(t4, jnp.where(t2, v[14], v[12]), jnp.where(t2, v[10], v[8]))
    lo = jnp.where(t4, jnp.where(t2, v[6], v[4]), jnp.where(t2, v[2], v[0]))
    t1 = ge(jnp.where(t8, hi, lo))
    one = lambda t, n: jnp.where(t, float(n), 0.0)
    return one(t8, 8) + one(t4, 4) + one(t2, 2) + one(t1, 1) + one(ge(v[15]), 1)


def _peer_route_kernel(x_ref, g_ref, sc_ref, sh_ref, wq_ref, k1_ref, k2_ref,
                       h_ref, e1_ref, j_ref, r2_ref, e2_ref, s1_s, s2_s):
    H, NK, tm = s1_s.shape
    half = k1_ref.shape[2]
    K = PEER_TOPK
    hf = _rms(x_ref[...]) * g_ref[...] * (1.0 + sc_ref[...]) + sh_ref[...]
    h = hf.astype(BF16)
    h_ref[...] = hf.T.astype(BF16)
    q = _dot(h, wq_ref[...]).astype(BF16)
    for hd in range(H):
        base = hd * 2 * half
        s1_s[hd] = _dot_nt(k1_ref[hd], q[:, base:base + half])
        s2_s[hd] = _dot_nt(k2_ref[hd], q[:, base + half:base + 2 * half])
    row16 = lax.broadcasted_iota(jnp.int32, (K, LANES), 0)
    row8 = lax.broadcasted_iota(jnp.int32, (8, LANES), 0)

    def lane_group(c, carry):
        lanes = pl.ds(pl.multiple_of(c * LANES, LANES), LANES)
        for hd in range(H):
            s1 = s1_s[hd, :, lanes]
            s2 = s2_s[hd, :, lanes]
            v1, _ = _top_values(s1, K)
            v2, rank = _top_values(s2, K)
            v2t = jnp.zeros((K, LANES), F32)
            for i in range(K):
                v2t = jnp.where(row16 == i, v2[i], v2t)
            cands = [v1[0] + v2t]
            for i in range(1, K):
                cands.append(jnp.where(row8 < K // (i + 1), v1[i] + v2t[:8], NEG_BIG))
            best = []
            for _ in range(K):
                m8 = jnp.maximum(cands[0][:8], cands[0][8:])
                for cnd in cands[1:]:
                    m8 = jnp.maximum(m8, cnd)
                m = jnp.max(m8, axis=0, keepdims=True)
                best.append(m)
                cands = [jnp.where(cnd == m, NEG_BIG, cnd) for cnd in cands]
            tau = best[K - 1]
            z = jnp.ones((1, LANES), F32)
            for n in range(1, K):
                z = z + jnp.exp(best[n] - best[0])
            e1_ref[hd, :, lanes] = jnp.exp(s1 - v1[0]) / z
            j_ref[hd, :, lanes] = _count_ge(s1, v2, tau)
            r2_ref[hd, :, lanes] = rank.astype(BF16)
            e2_ref[hd, :, lanes] = jnp.exp(s2 - v2[0]).astype(BF16)
        return carry

    lax.fori_loop(0, tm // LANES, lane_group, 0)


def _peer_route(x, g, sc, sh, wq, k1, k2, T, tm):
    M, D = x.shape
    H, NK, half = k1.shape
    nt = M // tm
    tab = pl.BlockSpec((None, H, NK, tm), lambda i: (i, 0, 0, 0))
    return pl.pallas_call(
        _peer_route_kernel,
        grid=(M // tm,),
        in_specs=[pl.BlockSpec((tm, D), lambda i: (i, 0)),
                  pl.BlockSpec((1, D), lambda i: (0, 0)),
                  _mod_spec(T, tm, D), _mod_spec(T, tm, D),
                  pl.BlockSpec((D, wq.shape[1]), lambda i: (0, 0)),
                  pl.BlockSpec((H, NK, half), lambda i: (0, 0, 0)),
                  pl.BlockSpec((H, NK, half), lambda i: (0, 0, 0))],
        out_specs=[pl.BlockSpec((None, D, tm), lambda i: (i, 0, 0)), tab, tab, tab, tab],
        out_shape=[jax.ShapeDtypeStruct((nt, D, tm), BF16)]
        + [jax.ShapeDtypeStruct((nt, H, NK, tm), dt) for dt in (F32, F32, BF16, BF16)],
        scratch_shapes=[pltpu.VMEM((H, NK, tm), F32), pltpu.VMEM((H, NK, tm), F32)],
        compiler_params=_params("arbitrary"),
        name="peer_route",
    )(x, g.reshape(1, D), _mod_rows(sc, T, tm), _mod_rows(sh, T, tm), wq, k1, k2)


def _peer_expert_kernel(h_ref, u_ref, vt_ref, e1_ref, j_ref, r2_ref, e2_ref, x_ref, gate_ref,
                        o_ref, acc_s, act_s, g_s, *, A):
    j = pl.program_id(1)
    H, NK, tm = r2_ref.shape

    @pl.when(j == 0)
    def _():
        acc_s[...] = jnp.zeros(acc_s.shape, F32)

    act_s[...] = _dot(u_ref[...], h_ref[...])
    GA = 4
    zero = jnp.zeros((NK, LANES), BF16)
    for c in range(tm // LANES):
        lanes = slice(c * LANES, (c + 1) * LANES)
        for a0 in range(0, A, GA):
            ws = [zero] * GA
            for hd in range(H):
                r2 = r2_ref[hd, :, lanes]
                e2 = e2_ref[hd, :, lanes]
                for i in range(GA):
                    al = a0 + i
                    e1 = e1_ref[hd, al:al + 1, lanes].astype(BF16)
                    cnt = j_ref[hd, al:al + 1, lanes].astype(BF16)
                    ws[i] = ws[i] + jnp.where(r2 < cnt, e2, zero) * e1
            for i in range(GA):
                rows = slice((a0 + i) * NK, (a0 + i + 1) * NK)
                g_s[rows, lanes] = ws[i] * _gelu(act_s[rows, lanes]).astype(BF16)
    acc_s[...] += _dot(vt_ref[...], g_s[...])

    @pl.when(j == pl.num_programs(1) - 1)
    def _():
        o_ref[...] = x_ref[...] + gate_ref[...] * acc_s[...].T


EXPERT_TILE = 1024


def _peer_experts(h, u, vt, tabs, x, gate, T, tm, te=EXPERT_TILE):
    M, D = x.shape
    E = u.shape[0]
    _, H, NK, _ = tabs[0].shape
    A = te // NK
    assert vt.shape == (E // te, D, te)
    tab = pl.BlockSpec((None, H, NK, tm), lambda i, j: (i, 0, 0, 0))
    tab_a = pl.BlockSpec((None, H, A, tm), lambda i, j: (i, 0, j, 0))
    return pl.pallas_call(
        functools.partial(_peer_expert_kernel, A=A),
        grid=(M // tm, E // te),
        in_specs=[pl.BlockSpec((None, D, tm), lambda i, j: (i, 0, 0)),
                  pl.BlockSpec((te, D), lambda i, j: (j, 0)),
                  pl.BlockSpec((None, D, te), lambda i, j: (j, 0, 0)),
                  tab_a, tab_a, tab, tab,
                  pl.BlockSpec((tm, D), lambda i, j: (i, 0)),
                  _mod_spec2(T, tm, D)],
        out_specs=pl.BlockSpec((tm, D), lambda i, j: (i, 0)),
        out_shape=jax.ShapeDtypeStruct((M, D), F32),
        scratch_shapes=[pltpu.VMEM((D, tm), F32), pltpu.VMEM((te, tm), F32),
                        pltpu.VMEM((te, tm), BF16)],
        compiler_params=_params("arbitrary", "arbitrary"),
        name="peer_experts",
    )(h, u, vt, *tabs, x, _mod_rows(gate, T, tm))


def _mod_spec2(T, tm, D):
    if tm <= T:
        return pl.BlockSpec((None, 1, D), lambda i, j: (i * tm // T, 0, 0))
    return pl.BlockSpec((None, tm, D), lambda i, j: (0, i, 0))


ROW_TILE = 512


def _trunk(x, c, w, ab_wts, cd_wts, peer_wts, cache):
    B, T, D = x.shape
    M = B * T
    tm = min(ROW_TILE, M)
    depth = w['ada_w'].shape[0]
    past_len = 0 if cache is None else cache['page_table'].shape[1] * PAGE_SIZE
    pos = past_len + jnp.arange(T, dtype=jnp.int32)
    mod = _ada(c, w['ada_w'], w['ada_b'])
    xf = x.reshape(M, D)
    ckv_rows, kr_rows, s5r, s5i, k_rows, v_rows, wkvs, shifts = [], [], [], [], [], [], [], []
    for l in range(depth):
        i = l // 2
        sh1, sc1, g1, sh2, sc2, g2 = jnp.split(mod[l], 6, axis=-1)
        if l % 2 == 0:
            lc = None if cache is None else dict(
                ckv=cache['mla_ckv'], kr=cache['mla_krope'], page_table=cache['page_table'], layer=i,
                s5_re=cache['s5_re'][i], s5_im=cache['s5_im'][i])
            xf, ckv, kr, sr, si = _ab_layer(xf, (sh1, sc1, g1), w['norm1_g'][l], pos, B, T, tm, lc,
                                            ab_wts[i])
            ckv_rows.append(ckv.reshape(B, T, -1))
            kr_rows.append(kr.reshape(B, T, -1))
            state_shape = (B,) + w['s5_lambda_re'].shape[1:]
            s5r.append(sr.reshape(state_shape))
            s5i.append(si.reshape(state_shape))
        else:
            lc = None if cache is None else dict(
                sb_k=cache['sb_k'], sb_v=cache['sb_v'], page_table=cache['page_table'], layer=i,
                wkv=cache['wkv'][i], shift=cache['shift'][i])
            xf, k, v, wkv, shift = _cd_layer(xf, (sh1, sc1, g1), w['norm1_g'][l], B, T, tm, lc,
                                             cd_wts[i])
            k_rows.append(k.reshape(B, T, SB_HEADS, SB_HEAD_DIM))
            v_rows.append(v.reshape(B, T, SB_HEADS, SB_HEAD_DIM))
            wkvs.append(wkv)
            shifts.append(shift)
        pw = peer_wts[l]
        h2, *tabs = _peer_route(xf, w['norm2_g'][l], sc2, sh2, pw['wq'], pw['k1'], pw['k2'], T, tm)
        xf = _peer_experts(h2, pw['u'], pw['vt'], tabs, xf, g2, T, tm)
    y = _rmsnorm(xf, w['final_g'], tm).reshape(B, T, D)
    return (y, jnp.stack(ckv_rows), jnp.stack(kr_rows), jnp.stack(k_rows), jnp.stack(v_rows),
            jnp.stack(s5r), jnp.stack(s5i), jnp.stack(wkvs), jnp.stack(shifts))


def kernel(x_prompt, x_sample, c_prompt, c_sample, cache_mla_ckv, cache_mla_krope, cache_sb_k,
           cache_sb_v, page_table, state_s5_re, state_s5_im, state_rwkv_wkv, state_rwkv_shift,
           ada_w, ada_b, norm1_g, norm2_g, final_g, ab_w_in, ab_w_out, s5_lambda_re, s5_lambda_im,
           s5_log_dt, s5_b_re, s5_b_im, s5_c_re, s5_c_im, s5_d, s5_w_glu, s5_b_glu, mla_g_q,
           mla_w_uq, mla_g_kv, mla_w_uk, mla_w_uv, cd_w_in, cd_w_out, rwkv_mu, rwkv_w0, rwkv_w2,
           rwkv_a0, rwkv_a2, rwkv_g2, rwkv_k_k, rwkv_k_a, rwkv_r_k, rwkv_ln_w, rwkv_ln_b,
           peer_w_q, peer_k1, peer_k2, peer_u, peer_v):
    w = dict(ada_w=ada_w, ada_b=ada_b, norm1_g=norm1_g, norm2_g=norm2_g, final_g=final_g,
             s5_lambda_re=s5_lambda_re)
    ab = dict(ab_w_in=ab_w_in, ab_w_out=ab_w_out, s5_lambda_re=s5_lambda_re,
              s5_lambda_im=s5_lambda_im, s5_log_dt=s5_log_dt, s5_b_re=s5_b_re, s5_b_im=s5_b_im,
              s5_c_re=s5_c_re, s5_c_im=s5_c_im, s5_d=s5_d, s5_w_glu=s5_w_glu, s5_b_glu=s5_b_glu,
              mla_g_q=mla_g_q, mla_w_uq=mla_w_uq, mla_g_kv=mla_g_kv, mla_w_uk=mla_w_uk,
              mla_w_uv=mla_w_uv)
    cd = dict(cd_w_in=cd_w_in, cd_w_out=cd_w_out, rwkv_mu=rwkv_mu, rwkv_w0=rwkv_w0,
              rwkv_w2=rwkv_w2, rwkv_a0=rwkv_a0, rwkv_a2=rwkv_a2, rwkv_g2=rwkv_g2,
              rwkv_k_k=rwkv_k_k, rwkv_k_a=rwkv_k_a, rwkv_r_k=rwkv_r_k, rwkv_ln_w=rwkv_ln_w,
              rwkv_ln_b=rwkv_ln_b)
    depth = ada_w.shape[0]
    ab_wts = [_ab_weights({k: v[i] for k, v in ab.items()}) for i in range((depth + 1) // 2)]
    cd_wts = [_cd_weights({k: v[i] for k, v in cd.items()}) for i in range(depth // 2)]
    n_exp, d_model = peer_v.shape[1:]
    value_blocks = lambda v: jnp.transpose(
        v.reshape(n_exp // EXPERT_TILE, EXPERT_TILE, d_model), (0, 2, 1)).astype(BF16)
    peer_wts = [dict(wq=peer_w_q[l].astype(BF16), k1=peer_k1[l].astype(BF16),
                     k2=peer_k2[l].astype(BF16), u=peer_u[l].astype(BF16),
                     vt=value_blocks(peer_v[l])) for l in range(depth)]
    cache = dict(mla_ckv=cache_mla_ckv, mla_krope=cache_mla_krope, sb_k=cache_sb_k, sb_v=cache_sb_v,
                 page_table=page_table, s5_re=state_s5_re, s5_im=state_s5_im, wkv=state_rwkv_wkv,
                 shift=state_rwkv_shift)
    p = _trunk(x_prompt, c_prompt, w, ab_wts, cd_wts, peer_wts, None)
    s = _trunk(x_sample, c_sample, w, ab_wts, cd_wts, peer_wts, cache)
    return (p[0], s[0]) + p[1:] + s[1:]
```

```python
import functools
import math

import jax
import jax.numpy as jnp
from jax import lax
from jax.experimental import pallas as pl
from jax.experimental.pallas import tpu as pltpu

F32 = jnp.float32
BF16 = jnp.bfloat16

NORM_EPS = 1e-6
ROPE_BASE = 10000.0
PAGE_SIZE = 128
S5_GROUP = 16
S5_STATE = 64
MLA_HEADS = 8
MLA_NOPE = 64
MLA_ROPE = 32
MLA_V = 64
MLA_KV_RANK = 256
MLA_KPAD = 384
SB_HEADS = 8
SB_HEAD_DIM = 64
RW_HEADS = 8
RW_HEAD_DIM = 64
RW_DECAY_RANK = 64
RW_A_RANK = 64
RW_GATE_RANK = 128
RW_GN_EPS = 64e-5
PEER_HEADS = 8
PEER_N_KEYS = 128
PEER_TOPK = 16
NEG_BIG = -3.0e38

VMEM_LIMIT = 48 * 1024 * 1024


def _params(*sem):
    return pltpu.CompilerParams(dimension_semantics=sem, vmem_limit_bytes=VMEM_LIMIT)


def _dot(a, b):
    return jnp.dot(a, b, preferred_element_type=F32)


def _dot_nt(a, b):
    return lax.dot_general(a, b, (((1,), (1,)), ((), ())), preferred_element_type=F32)


def _dot_tn(a, b):
    return lax.dot_general(a, b, (((0,), (0,)), ((), ())), preferred_element_type=F32)


def _split3(a):
    hi = a.astype(BF16)
    r1 = a - hi.astype(F32)
    mid = r1.astype(BF16)
    lo = (r1 - mid.astype(F32)).astype(BF16)
    return hi, mid, lo


def _dotx(dot, a, b):
    a0, a1, a2 = _split3(a)
    b0, b1, b2 = _split3(b)
    return (dot(a0, b0) + (dot(a0, b1) + dot(a1, b0))
            + (dot(a1, b1) + dot(a0, b2) + dot(a2, b0)))


def _dot_exact_rhs(a, b_bf16):
    a0, a1, a2 = _split3(a)
    return _dot(a0, b_bf16) + _dot(a1, b_bf16) + _dot(a2, b_bf16)


def _gelu(x):
    c = math.sqrt(2.0 / math.pi)
    return 0.5 * x * (1.0 + jnp.tanh(c * (x + 0.044715 * (x * x * x))))


def _sigmoid(x):
    return 1.0 / (1.0 + jnp.exp(-x))


def _softplus(x):
    return jnp.maximum(x, 0.0) + jnp.log1p(jnp.exp(-jnp.abs(x)))


def _rms(x):
    return x * lax.rsqrt(jnp.mean(x * x, axis=-1, keepdims=True) + NORM_EPS)


def _ada_kernel(c_ref, w_ref, b_ref, o_ref):
    c = c_ref[...]
    s = (c * _sigmoid(c)).astype(BF16)
    o_ref[...] = _dot(s, w_ref[...].astype(BF16)) + b_ref[...]


def _ada(c, ada_w, ada_b):
    L, D, N = ada_w.shape
    Bc = c.shape[0]
    tn = 1024
    return pl.pallas_call(
        _ada_kernel,
        grid=(L, N // tn),
        in_specs=[pl.BlockSpec((Bc, D), lambda l, j: (0, 0)),
                  pl.BlockSpec((None, D, tn), lambda l, j: (l, 0, j)),
                  pl.BlockSpec((None, 1, tn), lambda l, j: (l, 0, j))],
        out_specs=pl.BlockSpec((None, Bc, tn), lambda l, j: (l, 0, j)),
        out_shape=jax.ShapeDtypeStruct((L, Bc, N), F32),
        compiler_params=_params("arbitrary", "arbitrary"),
        name="ada",
    )(c, ada_w, ada_b.reshape(L, 1, N))


def _mod_spec(T, tm, D):
    if tm <= T:
        return pl.BlockSpec((None, 1, D), lambda i: (i * tm // T, 0, 0))
    return pl.BlockSpec((None, tm, D), lambda i: (0, i, 0))


def _mod_rows(m, T, tm):
    if tm <= T:
        return m[:, None, :]
    return jnp.repeat(m, T, axis=0)[None]


def _norm_mod_mm_kernel(x_ref, g_ref, sc_ref, sh_ref, w_ref, *o_refs):
    h = _rms(x_ref[...]) * g_ref[...] * (1.0 + sc_ref[...]) + sh_ref[...]
    res = _dot(h.astype(BF16), w_ref[...])
    off = 0
    for o_ref in o_refs:
        n = o_ref.shape[1]
        o_ref[...] = res[:, off:off + n]
        off += n


def _time_major_spec(T, tm, n):
    assert tm <= T
    return pl.BlockSpec((tm, n), lambda i: (i % (T // tm), i * tm // T))


def _norm_mod_mm(x, g, sc, sh, w, T, tm, widths, time_major=()):
    M, D = x.shape
    N = w.shape[1]
    assert sum(widths) == N
    specs = [_time_major_spec(T, tm, n) if k in time_major else pl.BlockSpec((tm, n), lambda i: (i, 0))
             for k, n in enumerate(widths)]
    shapes = [jax.ShapeDtypeStruct((T, M // T * n) if k in time_major else (M, n), F32)
              for k, n in enumerate(widths)]
    return pl.pallas_call(
        _norm_mod_mm_kernel,
        grid=(M // tm,),
        in_specs=[pl.BlockSpec((tm, D), lambda i: (i, 0)),
                  pl.BlockSpec((1, D), lambda i: (0, 0)),
                  _mod_spec(T, tm, D), _mod_spec(T, tm, D),
                  pl.BlockSpec((D, N), lambda i: (0, 0))],
        out_specs=specs,
        out_shape=shapes,
        compiler_params=_params("arbitrary"),
        name="norm_mod_mm",
    )(x, g.reshape(1, D), _mod_rows(sc, T, tm), _mod_rows(sh, T, tm), w)


def _mm_resid_kernel(a1_ref, a2_ref, w1_ref, w2_ref, x_ref, gate_ref, o_ref):
    mixed = (_dot(a1_ref[...].astype(BF16), w1_ref[...])
             + _dot(a2_ref[...].astype(BF16), w2_ref[...]))
    o_ref[...] = x_ref[...] + gate_ref[...] * mixed


def _mm_resid(a1, a2, w, x, gate, T, tm, a1_time_major=False):
    M, D = x.shape
    K2 = a2.shape[1]
    K1 = w.shape[0] - K2
    a1_spec = _time_major_spec(T, tm, K1) if a1_time_major else pl.BlockSpec((tm, K1), lambda i: (i, 0))
    return pl.pallas_call(
        _mm_resid_kernel,
        grid=(M // tm,),
        in_specs=[a1_spec,
                  pl.BlockSpec((tm, K2), lambda i: (i, 0)),
                  pl.BlockSpec((K1, D), lambda i: (0, 0)),
                  pl.BlockSpec((K2, D), lambda i: (0, 0)),
                  pl.BlockSpec((tm, D), lambda i: (i, 0)),
                  _mod_spec(T, tm, D)],
        out_specs=pl.BlockSpec((tm, D), lambda i: (i, 0)),
        out_shape=jax.ShapeDtypeStruct((M, D), F32),
        compiler_params=_params("arbitrary"),
        name="mm_resid",
    )(a1, a2, w[:K1], w[K1:], x, _mod_rows(gate, T, tm))


def _rmsnorm_kernel(x_ref, g_ref, o_ref):
    o_ref[...] = _rms(x_ref[...]) * g_ref[...]


def _rmsnorm(x, g, tm):
    M, D = x.shape
    return pl.pallas_call(
        _rmsnorm_kernel,
        grid=(M // tm,),
        in_specs=[pl.BlockSpec((tm, D), lambda i: (i, 0)),
                  pl.BlockSpec((1, D), lambda i: (0, 0))],
        out_specs=pl.BlockSpec((tm, D), lambda i: (i, 0)),
        out_shape=jax.ShapeDtypeStruct((M, D), F32),
        compiler_params=_params("arbitrary"),
        name="final_norm",
    )(x, g.reshape(1, D))


def _s5_kernel(u_ref, s0r_ref, s0i_ref, abr_ref, abi_ref, cor_ref, coi_ref,
               wbr_ref, wbi_ref, wcr_ref, wci_ref, d_ref, wg_ref, bg_ref,
               o_ref, sr_ref, si_ref, xr_buf, xi_buf, *, nb, tc):
    c = pl.program_id(0)

    @pl.when(c == 0)
    def _():
        sr_ref[...] = s0r_ref[...]
        si_ref[...] = s0i_ref[...]

    u = u_ref[...]
    ub = u.astype(BF16)
    bu_r = _dot(ub, wbr_ref[...])
    bu_i = _dot(ub, wbi_ref[...])
    cor, coi = cor_ref[...], coi_ref[...]
    xr_buf[...] = cor * bu_r - coi * bu_i
    xi_buf[...] = cor * bu_i + coi * bu_r
    abr, abi = abr_ref[...], abi_ref[...]

    def step(t, carry):
        xr, xi = carry
        rows = pl.ds(pl.multiple_of(t * nb, nb), nb)
        nr = abr * xr - abi * xi + xr_buf[rows, :]
        ni = abr * xi + abi * xr + xi_buf[rows, :]
        xr_buf[rows, :] = nr
        xi_buf[rows, :] = ni
        return nr, ni

    xr, xi = lax.fori_loop(0, tc, step, (sr_ref[...], si_ref[...]))
    sr_ref[...] = xr
    si_ref[...] = xi

    y = (_dot(xr_buf[...].astype(BF16), wcr_ref[...])
         - _dot(xi_buf[...].astype(BF16), wci_ref[...]) + d_ref[...] * u)
    z = _gelu(y)
    o_ref[...] = z * _sigmoid(_dot(z.astype(BF16), wg_ref[...]) + bg_ref[...])


def _s5_consts(lam_re, lam_im, log_dt, b_re, b_im, c_re, c_im):
    G, P = lam_re.shape
    Hc = b_re.shape[-1]
    dt = jnp.exp(log_dt)[:, None]
    mag = jnp.exp(lam_re * dt)
    ab_re, ab_im = mag * jnp.cos(lam_im * dt), mag * jnp.sin(lam_im * dt)
    den = lam_re * lam_re + lam_im * lam_im
    nr = ab_re - 1.0
    co_re = (nr * lam_re + ab_im * lam_im) / den
    co_im = (ab_im * lam_re - nr * lam_im) / den
    eye = jnp.eye(G, dtype=F32)
    wb = lambda b: jnp.einsum('gph,gk->ghkp', b, eye).reshape(G * Hc, G * P).astype(BF16)
    wc = lambda cc: jnp.einsum('ghp,gk->gpkh', cc, eye).reshape(G * P, G * Hc).astype(BF16)
    flat = lambda a: a.reshape(1, G * P)
    return (flat(ab_re), flat(ab_im), flat(co_re), flat(co_im),
            wb(b_re), wb(b_im), wc(c_re), wc(c_im))


def _s5(u_tb, s0_re, s0_im, consts, d, w_glu, b_glu, nb, tc):
    M, W = u_tb.shape
    T = M // nb
    GP = s0_re.shape[1]
    ab_re, ab_im, co_re, co_im, wbr, wbi, wcr, wci = consts
    rows = tc * nb
    full = lambda shape: pl.BlockSpec(shape, lambda c: (0,) * len(shape))
    return pl.pallas_call(
        functools.partial(_s5_kernel, nb=nb, tc=tc),
        grid=(T // tc,),
        in_specs=[pl.BlockSpec((rows, W), lambda c: (c, 0)),
                  full((nb, GP)), full((nb, GP)),
                  full((1, GP)), full((1, GP)), full((1, GP)), full((1, GP)),
                  full((W, GP)), full((W, GP)), full((GP, W)), full((GP, W)),
                  full((1, W)), full((W, W)), full((1, W))],
        out_specs=[pl.BlockSpec((rows, W), lambda c: (c, 0)), full((nb, GP)), full((nb, GP))],
        out_shape=[jax.ShapeDtypeStruct((M, W), F32),
                   jax.ShapeDtypeStruct((nb, GP), F32),
                   jax.ShapeDtypeStruct((nb, GP), F32)],
        scratch_shapes=[pltpu.VMEM((rows, GP), F32), pltpu.VMEM((rows, GP), F32)],
        compiler_params=_params("arbitrary"),
        name="s5",
    )(u_tb, s0_re, s0_im, ab_re, ab_im, co_re, co_im, wbr, wbi, wcr, wci,
      d.reshape(1, W), w_glu.astype(BF16), b_glu.reshape(1, W))


def _rope_tables(pos, width):
    half = MLA_ROPE // 2
    inv = ROPE_BASE ** (-jnp.arange(half, dtype=F32) / half)
    ang = pos.astype(F32)[:, None] * inv[None, :]
    cos, sin = jnp.cos(ang), jnp.sin(ang)
    pad = jnp.zeros((pos.shape[0], width - MLA_ROPE), F32)
    return (jnp.concatenate([cos, cos, pad], axis=1), jnp.concatenate([-sin, sin, pad], axis=1))


def _swap_halves(w):
    half = w.shape[-1] // 2
    return jnp.concatenate([w[..., half:], w[..., :half]], axis=-1)


def _pad_cols(w, width):
    return jnp.pad(w, [(0, 0)] * (w.ndim - 1) + [(0, width - w.shape[-1])])


def _mla_prep_kernel(cq_ref, ckv_ref, kr_ref, krs_ref, c2_ref, s2_ref, gq_ref, gkv_ref,
                     wq_ref, wuk_ref, q_ref, kcat_ref, ckvn_ref, krot_ref):
    H = q_ref.shape[0]
    c2, s2 = c2_ref[...], s2_ref[...]
    cqn = (_rms(cq_ref[...]) * gq_ref[...]).astype(BF16)
    qall = _dot(cqn, wq_ref[...])
    for h in range(H):
        base = h * 384
        nope = qall[:, base:base + 128].astype(BF16)
        rot = qall[:, base + 128:base + 256] * c2 + qall[:, base + 256:base + 384] * s2
        q_ref[h, :, 0:MLA_KV_RANK] = _dot(nope, wuk_ref[h]).astype(BF16)
        q_ref[h, :, MLA_KV_RANK:MLA_KPAD] = rot.astype(BF16)
    ckvn = _rms(ckv_ref[...]) * gkv_ref[...]
    krot = kr_ref[...] * c2 + krs_ref[...] * s2
    ckvn_ref[...] = ckvn
    krot_ref[...] = krot[:, :MLA_ROPE]
    kcat_ref[:, 0:MLA_KV_RANK] = ckvn.astype(BF16)
    kcat_ref[:, MLA_KV_RANK:MLA_KPAD] = krot.astype(BF16)


def _mla_prep(cq, ckv, kr, krs, c2, s2, g_q, g_kv, wq, wuk, tm):
    M = cq.shape[0]
    H = MLA_HEADS
    col = lambda w: pl.BlockSpec((tm, w), lambda i: (i, 0))
    full = lambda shape: pl.BlockSpec(shape, lambda i: (0,) * len(shape))
    return pl.pallas_call(
        _mla_prep_kernel,
        grid=(M // tm,),
        in_specs=[col(384), col(256), col(128), col(128), col(128), col(128),
                  full((1, 384)), full((1, 256)), full((384, H * 384)), full((H, 128, 256))],
        out_specs=[pl.BlockSpec((H, tm, MLA_KPAD), lambda i: (0, i, 0)),
                   col(MLA_KPAD), col(MLA_KV_RANK), col(MLA_ROPE)],
        out_shape=[jax.ShapeDtypeStruct((H, M, MLA_KPAD), BF16),
                   jax.ShapeDtypeStruct((M, MLA_KPAD), BF16),
                   jax.ShapeDtypeStruct((M, MLA_KV_RANK), F32),
                   jax.ShapeDtypeStruct((M, MLA_ROPE), F32)],
        compiler_params=_params("arbitrary"),
        name="mla_prep",
    )(cq, ckv, kr, krs, c2, s2, g_q.reshape(1, -1), g_kv.reshape(1, -1), wq, wuk)


def _mla_weights(w_uq, w_uk):
    C, H, _ = w_uq.shape
    nope = _pad_cols(w_uq[..., :MLA_NOPE], 128)
    rope = w_uq[..., MLA_NOPE:]
    wq = jnp.concatenate([nope, _pad_cols(rope, 128), _pad_cols(_swap_halves(rope), 128)], axis=-1)
    wuk = jnp.pad(jnp.transpose(w_uk, (1, 2, 0)), ((0, 0), (0, 128 - MLA_NOPE), (0, 0)))
    return wq.reshape(C, H * 384).astype(BF16), wuk.astype(BF16)


def _mla_attn_kernel(q_ref, k_ref, wuv_ref, o_ref, m_s, l_s, acc_s, *, tq, tk, scale):
    qi = pl.program_id(1)
    H = q_ref.shape[0]
    rows = H * tq
    q = q_ref[...].reshape(rows, MLA_KPAD)
    m_s[...] = jnp.full((rows, 1), NEG_BIG, F32)
    l_s[...] = jnp.zeros((rows, 1), F32)
    acc_s[...] = jnp.zeros((rows, MLA_KV_RANK), F32)
    row_t = qi * tq + lax.broadcasted_iota(jnp.int32, (H, tq, 1), 1).reshape(rows, 1)
    groups = [slice(g * rows // 4, (g + 1) * rows // 4) for g in range(4)]

    def block(kb, masked):
        k = k_ref[pl.ds(pl.multiple_of(kb * tk, tk), tk), :]
        vals = k[:, :MLA_KV_RANK]
        s = [_dot_nt(q[g], k) * scale for g in groups]
        if masked:
            kpos = kb * tk + lax.broadcasted_iota(jnp.int32, (1, tk), 1)
            s = [jnp.where(kpos <= row_t[g], sg, NEG_BIG) for g, sg in zip(groups, s)]
        m_prev = [m_s[g] for g in groups]
        m_new = [jnp.maximum(mp, jnp.max(sg, axis=-1, keepdims=True)) for mp, sg in zip(m_prev, s)]
        alpha = [jnp.exp(mp - mn) for mp, mn in zip(m_prev, m_new)]
        p = [jnp.exp(sg - mn) for sg, mn in zip(s, m_new)]
        pv = [_dot(pg.astype(BF16), vals) for pg in p]
        for g, al, pg, pvg, mn in zip(groups, alpha, p, pv, m_new):
            l_s[g] = al * l_s[g] + jnp.sum(pg, axis=-1, keepdims=True)
            acc_s[g] = al * acc_s[g] + pvg
            m_s[g] = mn

    n_full = (qi * tq) // tk

    def body(kb, carry):
        block(kb, False)
        return carry

    lax.fori_loop(0, n_full, body, 0)
    block(n_full, True)
    o = (acc_s[...] / l_s[...]).astype(BF16)
    for h in range(H):
        o_ref[:, h * MLA_V:(h + 1) * MLA_V] = _dot(o[h * tq:(h + 1) * tq], wuv_ref[h])


def _mla_attn(q, kcat, wuv, B, T, tq, tk):
    H = q.shape[0]
    nq = T // tq
    scale = (MLA_NOPE + MLA_ROPE) ** -0.5
    return pl.pallas_call(
        functools.partial(_mla_attn_kernel, tq=tq, tk=tk, scale=scale),
        grid=(B, nq),
        in_specs=[pl.BlockSpec((H, tq, MLA_KPAD), lambda b, i: (0, b * nq + i, 0)),
                  pl.BlockSpec((T, MLA_KPAD), lambda b, i: (b, 0)),
                  pl.BlockSpec((H, MLA_KV_RANK, MLA_V), lambda b, i: (0, 0, 0))],
        out_specs=pl.BlockSpec((tq, H * MLA_V), lambda b, i: (b * nq + i, 0)),
        out_shape=jax.ShapeDtypeStruct((B * T, H * MLA_V), F32),
        scratch_shapes=[pltpu.VMEM((H * tq, 1), F32), pltpu.VMEM((H * tq, 1), F32),
                        pltpu.VMEM((H * tq, MLA_KV_RANK), F32)],
        compiler_params=_params("arbitrary", "arbitrary"),
        name="mla_attn",
    )(q, kcat, wuv)


def _mla_decode_kernel(pt_ref, q_ref, knew_ref, *refs, npg, nt, scale):
    ckv_refs = refs[:npg]
    kr_refs = refs[npg:2 * npg]
    o_ref, m_s, l_s, acc_s, kbuf, rbuf = refs[2 * npg:]
    j = pl.program_id(1)
    q = q_ref[...]
    rows = q.shape[0]

    def update(s, vals):
        m_prev = m_s[...]
        m_new = jnp.maximum(m_prev, jnp.max(s, axis=-1, keepdims=True))
        alpha = jnp.exp(m_prev - m_new)
        p = jnp.exp(s - m_new)
        l_s[...] = alpha * l_s[...] + jnp.sum(p, axis=-1, keepdims=True)
        acc_s[...] = alpha * acc_s[...] + _dot(p.astype(BF16), vals)
        m_s[...] = m_new

    @pl.when(j == 0)
    def _():
        m_s[...] = jnp.full((rows, 1), NEG_BIG, F32)
        l_s[...] = jnp.zeros((rows, 1), F32)
        acc_s[...] = jnp.zeros((rows, MLA_KV_RANK), F32)
        knew = knew_ref[...]
        s = _dot_nt(q, knew) * scale
        t_row = lax.broadcasted_iota(jnp.int32, (rows, 1), 0) % nt
        col = lax.broadcasted_iota(jnp.int32, (1, PAGE_SIZE), 1)
        update(jnp.where(col <= t_row, s, NEG_BIG), knew[:, :MLA_KV_RANK])

    for r in range(npg):
        rows_r = slice(r * PAGE_SIZE, (r + 1) * PAGE_SIZE)
        kbuf[rows_r, :] = ckv_refs[r][...].astype(BF16)
        rbuf[:, rows_r] = kr_refs[r][...].astype(BF16)
    lat = kbuf[...]
    s = _dot_nt(q[:, :MLA_KV_RANK], lat) + _dot(q[:, MLA_KV_RANK:MLA_KV_RANK + MLA_ROPE], rbuf[...])
    update(s * scale, lat)

    @pl.when(j == pl.num_programs(1) - 1)
    def _():
        o_ref[...] = acc_s[...] / l_s[...]


def _mla_decode(q, knew, cache_ckv, cache_kr, page_table, layer, nt, npg=16):
    B, rows, _ = q.shape
    n_pages = page_table.shape[1]
    page = lambda r, a, c: pl.BlockSpec(
        (None, None, a, c), lambda b, j, pt: (layer, pt[b, j * npg + r], 0, 0))
    grid_spec = pltpu.PrefetchScalarGridSpec(
        num_scalar_prefetch=1,
        grid=(B, n_pages // npg),
        in_specs=[pl.BlockSpec((None, rows, MLA_KPAD), lambda b, j, pt: (b, 0, 0)),
                  pl.BlockSpec((None, PAGE_SIZE, MLA_KPAD), lambda b, j, pt: (b, 0, 0))]
        + [page(r, PAGE_SIZE, MLA_KV_RANK) for r in range(npg)]
        + [page(r, MLA_ROPE, PAGE_SIZE) for r in range(npg)],
        out_specs=pl.BlockSpec((None, rows, MLA_KV_RANK), lambda b, j, pt: (b, 0, 0)),
        scratch_shapes=[pltpu.VMEM((rows, 1), F32), pltpu.VMEM((rows, 1), F32),
                        pltpu.VMEM((rows, MLA_KV_RANK), F32),
                        pltpu.VMEM((npg * PAGE_SIZE, MLA_KV_RANK), BF16),
                        pltpu.VMEM((MLA_ROPE, npg * PAGE_SIZE), BF16)])
    scale = (MLA_NOPE + MLA_ROPE) ** -0.5
    return pl.pallas_call(
        functools.partial(_mla_decode_kernel, npg=npg, nt=nt, scale=scale),
        grid_spec=grid_spec,
        out_shape=jax.ShapeDtypeStruct((B, rows, MLA_KV_RANK), F32),
        compiler_params=_params("arbitrary", "arbitrary"),
        name="mla_decode",
    )(page_table, q, knew, *([cache_ckv] * npg), *([cache_kr] * npg))


def _bmm_kernel(x_ref, w_ref, o_ref):
    o_ref[...] = _dot(x_ref[...].astype(BF16), w_ref[...])


def _bmm(x, w):
    H, M, K = x.shape
    N = w.shape[2]
    return pl.pallas_call(
        _bmm_kernel,
        grid=(H,),
        in_specs=[pl.BlockSpec((None, M, K), lambda h: (h, 0, 0)),
                  pl.BlockSpec((None, K, N), lambda h: (h, 0, 0))],
        out_specs=pl.BlockSpec((None, M, N), lambda h: (h, 0, 0)),
        out_shape=jax.ShapeDtypeStruct((H, M, N), F32),
        compiler_params=_params("arbitrary"),
        name="bmm",
    )(x, w)


def _ab_in_weights(w_in, s5_width):
    a, b, c = s5_width, s5_width + 384, s5_width + 384 + MLA_KV_RANK
    u, cq, ckv, kr = w_in[:, :a], w_in[:, a:b], w_in[:, b:c], w_in[:, c:]
    return jnp.concatenate(
        [cq, u, ckv, _pad_cols(kr, 128), _pad_cols(_swap_halves(kr), 128)], axis=1).astype(BF16)


def _ab_layer(x, mods, g_norm, pos, B, T, tm, cache, wts, tq=128, tk=512):
    sh1, sc1, g1 = mods
    M = B * T
    W = wts['s5_d'].size
    GP = wts['s5_consts'][0].size
    time_major = tm <= T
    cq, u, ckv, kr, krs = _norm_mod_mm(x, g_norm, sc1, sh1, wts['w_in'], T, tm,
                                       (384, W, MLA_KV_RANK, 128, 128),
                                       time_major=(1,) if time_major else ())
    if time_major:
        u_tb = u.reshape(M, W)
    else:
        u_tb = jnp.transpose(u.reshape(B, T, W), (1, 0, 2)).reshape(M, W)
    if cache is None:
        s0_re = jnp.zeros((B, GP), F32)
        s0_im = jnp.zeros((B, GP), F32)
    else:
        s0_re, s0_im = cache['s5_re'].reshape(B, GP), cache['s5_im'].reshape(B, GP)
    s5_tb, s_re, s_im = _s5(u_tb, s0_re, s0_im, wts['s5_consts'], wts['s5_d'], wts['s5_w_glu'],
                            wts['s5_b_glu'], nb=B, tc=min(T, 64))
    if time_major:
        s5_out = s5_tb.reshape(T, B * W)
    else:
        s5_out = jnp.transpose(s5_tb.reshape(T, B, W), (1, 0, 2)).reshape(M, W)
    c2, s2 = _rope_tables(pos, 128)
    c2, s2 = jnp.tile(c2, (B, 1)), jnp.tile(s2, (B, 1))
    q, kcat, ckvn, krot = _mla_prep(cq, ckv, kr, krs, c2, s2, wts['mla_g_q'], wts['mla_g_kv'],
                                    wts['mla_wq'], wts['mla_wuk'], tm)
    H = MLA_HEADS
    if cache is None:
        o = _mla_attn(q, kcat, wts['mla_wuv'], B, T, tq, min(tk, T))
    else:
        qd = jnp.transpose(q.reshape(H, B, T, MLA_KPAD), (1, 0, 2, 3)).reshape(B, H * T, MLA_KPAD)
        knew = jnp.pad(kcat.reshape(B, T, MLA_KPAD), ((0, 0), (0, PAGE_SIZE - T), (0, 0)))
        o_lat = _mla_decode(qd, knew, cache['ckv'], jnp.transpose(cache['kr'], (0, 1, 3, 2)),
                            cache['page_table'],
                            cache['layer'], T)
        o_lat = jnp.transpose(o_lat.reshape(B, H, T, MLA_KV_RANK), (1, 0, 2, 3)).reshape(H, M, -1)
        o = jnp.transpose(_bmm(o_lat, wts['mla_wuv']), (1, 0, 2)).reshape(M, H * MLA_V)
    x = _mm_resid(s5_out, o, wts['w_out'], x, g1, T, tm, a1_time_major=time_major)
    return x, ckvn, krot, s_re, s_im


def _suffix_sums(ln, tri):
    hi = ln.astype(BF16)
    mid = (ln - hi.astype(F32)).astype(BF16)
    return _dot(hi, tri) + _dot(mid, tri)


def _sb_attn_kernel(q_ref, k_ref, v_ref, tri_ref, o_ref, acc_s, r_s, *, tq, scale):
    qi = pl.program_id(2)
    hs = range(acc_s.shape[0])
    d = acc_s.shape[2]
    tri = tri_ref[...]
    acc_s[...] = jnp.zeros(acc_s.shape, F32)
    r_s[...] = jnp.zeros(r_s.shape, F32)
    heads = lambda a: [a[:, h * d:(h + 1) * d].astype(BF16) for h in hs]
    q = heads(q_ref[...])

    def block(kb, diag):
        rows = pl.ds(pl.multiple_of(kb * tq, tq), tq)
        k = heads(k_ref[rows, :])
        v = heads(v_ref[rows, :])
        z = [_dot_nt(q[h], k[h]) * scale for h in hs]
        sp = [_softplus(a) for a in z]
        if diag:
            mask = (lax.broadcasted_iota(jnp.int32, (tq, tq), 1)
                    < lax.broadcasted_iota(jnp.int32, (tq, tq), 0))
            ln = [jnp.where(mask, -a, 0.0) for a in sp]
        else:
            ln = [-a for a in sp]
        a = [jnp.exp(z[h] - sp[h] + _suffix_sums(ln[h], tri) + r_s[h]) for h in hs]
        if diag:
            a = [jnp.where(mask, ah, 0.0) for ah in a]
        for h in hs:
            acc_s[h] += _dot(a[h].astype(BF16), v[h])
            r_s[h] += jnp.sum(ln[h], axis=-1, keepdims=True)

    block(qi, True)

    def body(i, carry):
        block(qi - 1 - i, False)
        return carry

    lax.fori_loop(0, qi, body, 0)
    for h in hs:
        o_ref[:, h * d:(h + 1) * d] = acc_s[h]


def _tri(n):
    return (lax.broadcasted_iota(jnp.int32, (n, n), 0)
            > lax.broadcasted_iota(jnp.int32, (n, n), 1)).astype(BF16)


def _sb_attn(q, k, v, B, T, d, tq):
    M, W = q.shape
    cw = 2 * LANES
    hb = cw // d
    nq = T // tq
    seq = pl.BlockSpec((T, cw), lambda b, h, i: (b, h))
    blk = pl.BlockSpec((tq, cw), lambda b, h, i: (b * nq + i, h))
    return pl.pallas_call(
        functools.partial(_sb_attn_kernel, tq=tq, scale=d ** -0.5),
        grid=(B, W // cw, nq),
        in_specs=[blk, seq, seq, pl.BlockSpec((tq, tq), lambda b, h, i: (0, 0))],
        out_specs=blk,
        out_shape=jax.ShapeDtypeStruct((M, W), F32),
        scratch_shapes=[pltpu.VMEM((hb, tq, d), F32), pltpu.VMEM((hb, tq, 1), F32)],
        compiler_params=_params("arbitrary", "arbitrary", "arbitrary"),
        name="sb_attn",
    )(q, k, v, _tri(tq))


def _sb_decode_kernel(pt_ref, q_ref, knew_ref, vnew_ref, tw_ref, *refs, npg, scale):
    k_refs = refs[:npg]
    v_refs = refs[npg:2 * npg]
    o_ref, acc_s, r_s, kbuf, vbuf = refs[2 * npg:]
    j = pl.program_id(1)
    H, qrows, d = q_ref.shape
    hs = range(H)
    tw = tw_ref[...]

    def block(keys, vals, n, fresh):
        cols = n * PAGE_SIZE
        z = jnp.concatenate([_dot(q_ref[h], keys[h]) for h in hs], axis=0) * scale
        sp = _softplus(z)
        if fresh:
            t_row = lax.broadcasted_iota(jnp.int32, (H * qrows, cols), 0) % qrows
            mask = lax.broadcasted_iota(jnp.int32, (H * qrows, cols), 1) < t_row
            ln = jnp.where(mask, -sp, 0.0)
        else:
            ln = -sp
        hi = ln.astype(BF16)
        mid = (ln - hi.astype(F32)).astype(BF16)
        run = r_s[...]
        after = []
        for p in range(n):
            sl = slice(p * PAGE_SIZE, (p + 1) * PAGE_SIZE)
            both = _dot(hi[:, sl], tw) + _dot(mid[:, sl], tw)
            after.append(both[:, :LANES] + run)
            run = run + both[:, LANES:]
        a = jnp.exp(z - sp + jnp.concatenate(after, axis=1))
        if fresh:
            a = jnp.where(mask, a, 0.0)
        a = a.astype(BF16)
        for h in hs:
            acc_s[h] += _dot_nt(a[h * qrows:(h + 1) * qrows], vals[h])
        r_s[...] = run

    @pl.when(j == 0)
    def _():
        acc_s[...] = jnp.zeros(acc_s.shape, F32)
        r_s[...] = jnp.zeros(r_s.shape, F32)
        block(knew_ref, vnew_ref, 1, True)

    for r in range(npg):
        lanes_r = slice(r * PAGE_SIZE, (r + 1) * PAGE_SIZE)
        kbuf[:, :, lanes_r] = k_refs[r][...].astype(BF16)
        vbuf[:, :, lanes_r] = v_refs[r][...].astype(BF16)
    block(kbuf, vbuf, npg, False)

    @pl.when(j == pl.num_programs(1) - 1)
    def _():
        o_ref[...] = acc_s[...]


QROWS = 8


def _sb_decode(q, knew, vnew, cache_kt, cache_vt, page_table, layer, npg=16):
    B, H, qrows, d = q.shape
    assert PAGE_SIZE == LANES
    n_pages = page_table.shape[1]
    page = lambda r: pl.BlockSpec(
        (None, None, H, d, PAGE_SIZE),
        lambda b, j, pt: (layer, pt[b, n_pages - 1 - (j * npg + r)], 0, 0, 0))
    per_b = lambda a, c: pl.BlockSpec((None, H, a, c), lambda b, j, pt: (b, 0, 0, 0))
    ti = lax.broadcasted_iota(jnp.int32, (LANES, 2 * LANES), 0)
    si = lax.broadcasted_iota(jnp.int32, (LANES, 2 * LANES), 1)
    tw = ((ti > si) | (si >= LANES)).astype(BF16)
    grid_spec = pltpu.PrefetchScalarGridSpec(
        num_scalar_prefetch=1,
        grid=(B, n_pages // npg),
        in_specs=[per_b(qrows, d), per_b(d, PAGE_SIZE), per_b(d, PAGE_SIZE),
                  pl.BlockSpec((LANES, 2 * LANES), lambda b, j, pt: (0, 0))]
        + [page(r) for r in range(npg)] * 2,
        out_specs=per_b(qrows, d),
        scratch_shapes=[pltpu.VMEM((H, qrows, d), F32), pltpu.VMEM((H * qrows, LANES), F32),
                        pltpu.VMEM((H, d, npg * PAGE_SIZE), BF16),
                        pltpu.VMEM((H, d, npg * PAGE_SIZE), BF16)])
    return pl.pallas_call(
        functools.partial(_sb_decode_kernel, npg=npg, scale=d ** -0.5),
        grid_spec=grid_spec,
        out_shape=jax.ShapeDtypeStruct((B, H, qrows, d), F32),
        compiler_params=_params("arbitrary", "arbitrary"),
        name="sb_decode",
    )(page_table, q, knew, vnew, tw, *([cache_kt] * npg), *([cache_vt] * npg))


def _rwkv_prep_kernel(p_ref, prev_ref, mu_ref, w0_ref, w2_ref, a0_ref, a2_ref, g2_ref,
                      kkw_ref, ka_ref, ones_ref, r_o, lw_o, k_o, v_o, kk_o, b_o, g_o):
    H, _, N = r_o.shape
    Wd = H * N
    p = p_ref[...]
    ps = p + (prev_ref[...] - p) * mu_ref[...]
    r, k, v = ps[:, :Wd], ps[:, Wd:2 * Wd], ps[:, 2 * Wd:3 * Wd]
    o = 3 * Wd
    xw = ps[:, o:o + RW_DECAY_RANK]
    xa = ps[:, o + RW_DECAY_RANK:o + RW_DECAY_RANK + RW_A_RANK]
    xg = ps[:, o + RW_DECAY_RANK + RW_A_RANK:]
    w_log = -_softplus(-(w0_ref[...] + _dot(jnp.tanh(xw).astype(BF16), w2_ref[...]))) - 0.5
    lw = -jnp.exp(w_log)
    a = _sigmoid(a0_ref[...] + _dot(xa.astype(BF16), a2_ref[...]))
    g = _dot(_sigmoid(xg).astype(BF16), g2_ref[...])
    kk = k * kkw_ref[...]
    ss = _dot_exact_rhs(kk * kk, ones_ref[...])
    kk = kk / jnp.maximum(jnp.sqrt(ss), 1e-12)
    k2 = k * (1.0 + (a - 1.0) * ka_ref[...])
    b = kk * a
    for h in range(H):
        sl = slice(h * N, (h + 1) * N)
        r_o[h] = r[:, sl]
        lw_o[h] = lw[:, sl]
        k_o[h] = k2[:, sl]
        v_o[h] = v[:, sl]
        kk_o[h] = kk[:, sl]
        b_o[h] = b[:, sl]
        g_o[h] = g[:, sl]


def _rwkv_prep(p, prev, wts, tm):
    M, cols = p.shape
    H, N = RW_HEADS, RW_HEAD_DIM
    Wd = H * N
    row = lambda n: pl.BlockSpec((1, n), lambda i: (0, 0))
    mat = lambda a, b: pl.BlockSpec((a, b), lambda i: (0, 0))
    out = pl.BlockSpec((H, tm, N), lambda i: (0, i, 0))
    return pl.pallas_call(
        _rwkv_prep_kernel,
        grid=(M // tm,),
        in_specs=[pl.BlockSpec((tm, cols), lambda i: (i, 0)), pl.BlockSpec((tm, cols), lambda i: (i, 0)),
                  row(cols), row(Wd), mat(RW_DECAY_RANK, Wd), row(Wd), mat(RW_A_RANK, Wd),
                  mat(RW_GATE_RANK, Wd), row(Wd), row(Wd), mat(Wd, Wd)],
        out_specs=[out] * 7,
        out_shape=[jax.ShapeDtypeStruct((H, M, N), F32)] * 7,
        compiler_params=_params("arbitrary"),
        name="rwkv_prep",
    )(p, prev, wts['mu'], wts['w0'], wts['w2'], wts['a0'], wts['a2'], wts['g2'],
      wts['k_k'], wts['k_a'], wts['head_ones'])


def _dot3(dot, a, b):
    a0 = a.astype(BF16)
    a1 = (a - a0.astype(F32)).astype(BF16)
    b0 = b.astype(BF16)
    b1 = (b - b0.astype(F32)).astype(BF16)
    return dot(a0, b0) + (dot(a0, b1) + dot(a1, b0))


SOLVE_BLOCK = 16


def _unit_lower_solve(Ls, Rs, C):
    n = range(len(Ls))
    if C <= SOLVE_BLOCK:
        Us = list(Rs)
        for s in range(C - 1):
            Us = [Us[i] - Ls[i][:, s:s + 1] * Us[i][s:s + 1, :] for i in n]
        return Us
    nb = C // SOLVE_BLOCK
    assert nb * SOLVE_BLOCK == C and nb <= 4
    ti = lax.broadcasted_iota(jnp.int32, (C, C), 0)
    si = lax.broadcasted_iota(jnp.int32, (C, C), 1)
    same = (ti // SOLVE_BLOCK) == (si // SOLVE_BLOCK)
    rep_t = (lax.broadcasted_iota(jnp.int32, (C, SOLVE_BLOCK), 0) % SOLVE_BLOCK
             == lax.broadcasted_iota(jnp.int32, (C, SOLVE_BLOCK), 1))
    rep = (lax.broadcasted_iota(jnp.int32, (SOLVE_BLOCK, C), 1) % SOLVE_BLOCK
           == lax.broadcasted_iota(jnp.int32, (SOLVE_BLOCK, C), 0)).astype(BF16)
    Ld = [jnp.where(same, L, 0.0) for L in Ls]
    Ldc = [_dot_exact_rhs(a, rep_t.astype(BF16)) for a in Ld]
    X = [rep_t.astype(F32) for _ in n]
    for s in range(SOLVE_BLOCK - 1):
        rows = [jnp.broadcast_to(x.reshape(nb, SOLVE_BLOCK, SOLVE_BLOCK)[:, s:s + 1, :],
                                 (nb, SOLVE_BLOCK, SOLVE_BLOCK)).reshape(C, SOLVE_BLOCK) for x in X]
        X = [X[i] - Ldc[i][:, s:s + 1] * rows[i] for i in n]
    T1 = [jnp.where(same, _dot_exact_rhs(x, rep), 0.0) for x in X]
    R1 = [_dot3(_dot, T1[i], Rs[i]) for i in n]
    Mm = [_dot3(_dot, T1[i], Ls[i] - Ld[i]) for i in n]
    M2 = [_dot3(_dot, m, m) for m in Mm]
    Y1 = [R1[i] + _dot3(_dot, M2[i], R1[i]) for i in n]
    return [Y1[i] - _dot3(_dot, Mm[i], Y1[i]) for i in n]


def _rwkv_chunk_kernel(r_ref, lw_ref, k_ref, v_ref, kk_ref, b_ref, g_ref, s0_ref,
                       lnw_ref, lnb_ref, rk_ref, y_ref, s_ref, *, C):
    c = pl.program_id(2)

    @pl.when(c == 0)
    def _():
        s_ref[...] = s0_ref[...]

    hs = range(r_ref.shape[0])
    ti = lax.broadcasted_iota(jnp.int32, (C, C), 0)
    si = lax.broadcasted_iota(jnp.int32, (C, C), 1)
    strict, incl = si < ti, si <= ti
    tril = incl.astype(BF16)
    r, lw, k, v = ([ref[h] for h in hs] for ref in (r_ref, lw_ref, k_ref, v_ref))
    kk, b = ([ref[h] for h in hs] for ref in (kk_ref, b_ref))
    S0 = [s_ref[h] for h in hs]
    parts = [_split3(a) for a in lw]
    cum = [_dot(tril, p0) + _dot(tril, p1) + _dot(tril, p2) for p0, p1, p2 in parts]
    g_in = [jnp.exp(a) for a in cum]
    g_inv = [jnp.exp(-a) for a in cum]
    qt = [kk[h] * jnp.exp(cum[h] - lw[h]) for h in hs]
    bt = [b[h] * g_inv[h] for h in hs]
    kt = [k[h] * g_inv[h] for h in hs]
    rt = [r[h] * g_in[h] for h in hs]
    qr = [jnp.concatenate([qt[h], rt[h]], axis=0) for h in hs]
    bk = [jnp.concatenate([bt[h], kt[h]], axis=0) for h in hs]
    gram = [_dot3(_dot_nt, qr[h], bk[h]) for h in hs]
    Lb = [jnp.where(strict, g[:C, :C], 0.0) for g in gram]
    Lk = [jnp.where(strict, g[:C, C:], 0.0) for g in gram]
    Ab = [jnp.where(incl, g[C:, :C], 0.0) for g in gram]
    Ak = [jnp.where(incl, g[C:, C:], 0.0) for g in gram]
    on_state = [_dot3(_dot_nt, qr[h], S0[h]) for h in hs]
    on_v = [_dot3(_dot, jnp.concatenate([Lk[h], Ak[h]], axis=0), v[h]) for h in hs]
    U = _unit_lower_solve(Lb, [-(on_state[h][:C] + on_v[h][:C]) for h in hs], C)
    y = [on_state[h][C:] + _dot3(_dot, Ab[h], U[h]) + on_v[h][C:] for h in hs]
    for h in hs:
        grown = _dot3(_dot_tn, jnp.concatenate([U[h], v[h]], axis=0), bk[h])
        s_ref[h] = (S0[h] + grown) * g_in[h][C - 1:C, :]
    for h in hs:
        mean = jnp.mean(y[h], axis=-1, keepdims=True)
        var = jnp.mean(jnp.square(y[h] - mean), axis=-1, keepdims=True)
        yn = (y[h] - mean) * lax.rsqrt(var + RW_GN_EPS) * lnw_ref[h] + lnb_ref[h]
        bonus = jnp.sum(r[h] * k[h] * rk_ref[h], axis=-1, keepdims=True) * v[h]
        n = y[h].shape[1]
        y_ref[:, h * n:(h + 1) * n] = (yn + bonus) * g_ref[h]


def _rwkv_chunks(feats, s0, ln_w, ln_b, r_k, B, T, C, Hb=8):
    H, M, N = feats[0].shape
    nc = T // C
    seq = pl.BlockSpec((Hb, C, N), lambda b, hb, c: (hb, b * nc + c, 0))
    out = pl.BlockSpec((C, Hb * N), lambda b, hb, c: (b * nc + c, hb))
    st = pl.BlockSpec((None, Hb, N, N), lambda b, hb, c: (b, hb, 0, 0))
    par = pl.BlockSpec((Hb, 1, N), lambda b, hb, c: (hb, 0, 0))
    return pl.pallas_call(
        functools.partial(_rwkv_chunk_kernel, C=C),
        grid=(B, H // Hb, nc),
        in_specs=[seq] * 7 + [st, par, par, par],
        out_specs=[out, st],
        out_shape=[jax.ShapeDtypeStruct((M, H * N), F32), jax.ShapeDtypeStruct((B, H, N, N), F32)],
        compiler_params=_params("arbitrary", "arbitrary", "arbitrary"),
        name="rwkv_chunks",
    )(*feats, s0, ln_w.reshape(H, 1, N), ln_b.reshape(H, 1, N), r_k.reshape(H, 1, N))


def _cd_weights(w):
    H, N = RW_HEADS, RW_HEAD_DIM
    Wd = H * N
    row = lambda a: a.reshape(1, -1)
    head = jnp.arange(Wd) // N
    return {
        'w_in': w['cd_w_in'].astype(BF16), 'w_out': w['cd_w_out'].astype(BF16),
        'mu': row(w['rwkv_mu']), 'w0': row(w['rwkv_w0']), 'w2': w['rwkv_w2'].astype(BF16),
        'a0': row(w['rwkv_a0']), 'a2': w['rwkv_a2'].astype(BF16), 'g2': w['rwkv_g2'].astype(BF16),
        'k_k': row(w['rwkv_k_k']), 'k_a': row(w['rwkv_k_a']),
        'head_ones': (head[:, None] == head[None, :]).astype(BF16),
        'ln_w': w['rwkv_ln_w'], 'ln_b': w['rwkv_ln_b'], 'r_k': w['rwkv_r_k'],
    }


def _ab_weights(w):
    wq, wuk = _mla_weights(w['mla_w_uq'], w['mla_w_uk'])
    return {
        'w_in': _ab_in_weights(w['ab_w_in'], w['s5_d'].size), 'w_out': w['ab_w_out'].astype(BF16),
        's5_consts': _s5_consts(w['s5_lambda_re'], w['s5_lambda_im'], w['s5_log_dt'],
                                w['s5_b_re'], w['s5_b_im'], w['s5_c_re'], w['s5_c_im']),
        's5_d': w['s5_d'], 's5_w_glu': w['s5_w_glu'], 's5_b_glu': w['s5_b_glu'],
        'mla_g_q': w['mla_g_q'], 'mla_g_kv': w['mla_g_kv'], 'mla_wq': wq, 'mla_wuk': wuk,
        'mla_wuv': jnp.transpose(w['mla_w_uv'], (1, 0, 2)).astype(BF16),
    }


def _cd_layer(x, mods, g_norm, B, T, tm, cache, wts, tq=256):
    sh1, sc1, g1 = mods
    M = B * T
    H, d = SB_HEADS, SB_HEAD_DIM
    Wsb = H * d
    cols = wts['mu'].shape[1]
    q, k, v, rw = _norm_mod_mm(x, g_norm, sc1, sh1, wts['w_in'], T, tm, (Wsb, Wsb, Wsb, cols))
    if cache is None:
        sb = _sb_attn(q, k, v, B, T, d, tq)
    else:
        keys_minor = lambda a: jnp.transpose(a, (0, 1, 3, 4, 2))
        heads = lambda a: jnp.transpose(a.reshape(B, T, H, d), (0, 2, 1, 3))
        qh = jnp.pad(heads(q), ((0, 0), (0, 0), (0, QROWS - T), (0, 0))).astype(BF16)
        fresh = lambda a: jnp.pad(jnp.transpose(heads(a), (0, 1, 3, 2)),
                                  ((0, 0), (0, 0), (0, 0), (0, PAGE_SIZE - T))).astype(BF16)
        sb = _sb_decode(qh, fresh(k), fresh(v), keys_minor(cache['sb_k']),
                        keys_minor(cache['sb_v']), cache['page_table'], cache['layer'])
        sb = jnp.transpose(sb[:, :, :T], (0, 2, 1, 3)).reshape(M, Wsb)
    rw3 = rw.reshape(B, T, cols)
    shift0 = jnp.zeros((B, cols), F32) if cache is None else cache['shift']
    prev = jnp.concatenate([shift0[:, None, :], rw3[:, :-1]], axis=1).reshape(M, cols)
    feats = _rwkv_prep(rw, prev, wts, tm)
    N = RW_HEAD_DIM
    C = min(T, 64)
    if T < 8:
        C = 8
        padt = lambda a: jnp.pad(a.reshape(RW_HEADS, B, T, N),
                                 ((0, 0), (0, 0), (0, C - T), (0, 0))).reshape(RW_HEADS, B * C, N)
        feats = [padt(f) for f in feats]
    s0 = (jnp.zeros((B, RW_HEADS, N, N), F32) if cache is None else cache['wkv'])
    Tp = max(T, C)
    y, wkv = _rwkv_chunks(feats, s0, wts['ln_w'], wts['ln_b'], wts['r_k'], B, Tp, C)
    rw_out = y.reshape(B, Tp, RW_HEADS * N)[:, :T].reshape(M, RW_HEADS * N)
    x = _mm_resid(sb, rw_out, wts['w_out'], x, g1, T, tm)
    return x, k, v, wkv, rw3[:, -1]


LANES = 128


def _top_values(s, n):
    out = []
    rank = jnp.full(s.shape, float(n), F32)
    for i in range(n):
        m = jnp.max(s, axis=0, keepdims=True)
        out.append(m)
        hit = s == m
        rank = jnp.where(hit, float(i), rank)
        s = jnp.where(hit, NEG_BIG, s)
    return out, rank


def _count_ge(s, v, tau):
    assert len(v) == 16
    ge = lambda row: (s + row) >= tau
    t8 = ge(v[7])
    t4 = ge(jnp.where(t8, v[11], v[3]))
    t2 = ge(jnp.where(t8, jnp.where(t4, v[13], v[9]), jnp.where(t4, v[5], v[1])))
    hi = jnp.where(t4, jnp.where(t2, v[14], v[12]), jnp.where(t2, v[10], v[8]))
    lo = jnp.where(t4, jnp.where(t2, v[6], v[4]), jnp.where(t2, v[2], v[0]))
    t1 = ge(jnp.where(t8, hi, lo))
    one = lambda t, n: jnp.where(t, float(n), 0.0)
    return one(t8, 8) + one(t4, 4) + one(t2, 2) + one(t1, 1) + one(ge(v[15]), 1)


def _peer_route_kernel(x_ref, g_ref, sc_ref, sh_ref, wq_ref, k1_ref, k2_ref,
                       h_ref, e1_ref, j_ref, r2_ref, e2_ref, s1_s, s2_s):
    H, NK, tm = s1_s.shape
    half = k1_ref.shape[2]
    K = PEER_TOPK
    hf = _rms(x_ref[...]) * g_ref[...] * (1.0 + sc_ref[...]) + sh_ref[...]
    h = hf.astype(BF16)
    h_ref[...] = hf.T.astype(BF16)
    q = _dot(h, wq_ref[...]).astype(BF16)
    for hd in range(H):
        base = hd * 2 * half
        s1_s[hd] = _dot_nt(k1_ref[hd], q[:, base:base + half])
        s2_s[hd] = _dot_nt(k2_ref[hd], q[:, base + half:base + 2 * half])
    row16 = lax.broadcasted_iota(jnp.int32, (K, LANES), 0)
    row8 = lax.broadcasted_iota(jnp.int32, (8, LANES), 0)

    def lane_group(c, carry):
        lanes = pl.ds(pl.multiple_of(c * LANES, LANES), LANES)
        for hd in range(H):
            s1 = s1_s[hd, :, lanes]
            s2 = s2_s[hd, :, lanes]
            v1, _ = _top_values(s1, K)
            v2, rank = _top_values(s2, K)
            v2t = jnp.zeros((K, LANES), F32)
            for i in range(K):
                v2t = jnp.where(row16 == i, v2[i], v2t)
            cands = [v1[0] + v2t]
            for i in range(1, K):
                cands.append(jnp.where(row8 < K // (i + 1), v1[i] + v2t[:8], NEG_BIG))
            best = []
            for _ in range(K):
                m8 = jnp.maximum(cands[0][:8], cands[0][8:])
                for cnd in cands[1:]:
                    m8 = jnp.maximum(m8, cnd)
                m = jnp.max(m8, axis=0, keepdims=True)
                best.append(m)
                cands = [jnp.where(cnd == m, NEG_BIG, cnd) for cnd in cands]
            tau = best[K - 1]
            z = jnp.ones((1, LANES), F32)
            for n in range(1, K):
                z = z + jnp.exp(best[n] - best[0])
            e1_ref[hd, :, lanes] = jnp.exp(s1 - v1[0]) / z
            j_ref[hd, :, lanes] = _count_ge(s1, v2, tau)
            r2_ref[hd, :, lanes] = rank.astype(BF16)
            e2_ref[hd, :, lanes] = jnp.exp(s2 - v2[0]).astype(BF16)
        return carry

    lax.fori_loop(0, tm // LANES, lane_group, 0)


def _peer_route(x, g, sc, sh, wq, k1, k2, T, tm):
    M, D = x.shape
    H, NK, half = k1.shape
    nt = M // tm
    tab = pl.BlockSpec((None, H, NK, tm), lambda i: (i, 0, 0, 0))
    return pl.pallas_call(
        _peer_route_kernel,
        grid=(M // tm,),
        in_specs=[pl.BlockSpec((tm, D), lambda i: (i, 0)),
                  pl.BlockSpec((1, D), lambda i: (0, 0)),
                  _mod_spec(T, tm, D), _mod_spec(T, tm, D),
                  pl.BlockSpec((D, wq.shape[1]), lambda i: (0, 0)),
                  pl.BlockSpec((H, NK, half), lambda i: (0, 0, 0)),
                  pl.BlockSpec((H, NK, half), lambda i: (0, 0, 0))],
        out_specs=[pl.BlockSpec((None, D, tm), lambda i: (i, 0, 0)), tab, tab, tab, tab],
        out_shape=[jax.ShapeDtypeStruct((nt, D, tm), BF16)]
        + [jax.ShapeDtypeStruct((nt, H, NK, tm), dt) for dt in (F32, F32, BF16, BF16)],
        scratch_shapes=[pltpu.VMEM((H, NK, tm), F32), pltpu.VMEM((H, NK, tm), F32)],
        compiler_params=_params("arbitrary"),
        name="peer_route",
    )(x, g.reshape(1, D), _mod_rows(sc, T, tm), _mod_rows(sh, T, tm), wq, k1, k2)


def _peer_expert_kernel(h_ref, u_ref, vt_ref, e1_ref, j_ref, r2_ref, e2_ref, x_ref, gate_ref,
                        o_ref, acc_s, act_s, g_s, *, A):
    j = pl.program_id(1)
    H, NK, tm = r2_ref.shape

    @pl.when(j == 0)
    def _():
        acc_s[...] = jnp.zeros(acc_s.shape, F32)

    act_s[...] = _dot(u_ref[...], h_ref[...])
    GA = 4
    zero = jnp.zeros((NK, LANES), BF16)
    for c in range(tm // LANES):
        lanes = slice(c * LANES, (c + 1) * LANES)
        for a0 in range(0, A, GA):
            ws = [zero] * GA
            for hd in range(H):
                r2 = r2_ref[hd, :, lanes]
                e2 = e2_ref[hd, :, lanes]
                for i in range(GA):
                    al = a0 + i
                    e1 = e1_ref[hd, al:al + 1, lanes].astype(BF16)
                    cnt = j_ref[hd, al:al + 1, lanes].astype(BF16)
                    ws[i] = ws[i] + jnp.where(r2 < cnt, e2, zero) * e1
            for i in range(GA):
                rows = slice((a0 + i) * NK, (a0 + i + 1) * NK)
                g_s[rows, lanes] = ws[i] * _gelu(act_s[rows, lanes]).astype(BF16)
    acc_s[...] += _dot(vt_ref[...], g_s[...])

    @pl.when(j == pl.num_programs(1) - 1)
    def _():
        o_ref[...] = x_ref[...] + gate_ref[...] * acc_s[...].T


EXPERT_TILE = 1024


def _peer_experts(h, u, vt, tabs, x, gate, T, tm, te=EXPERT_TILE):
    M, D = x.shape
    E = u.shape[0]
    _, H, NK, _ = tabs[0].shape
    A = te // NK
    assert vt.shape == (E // te, D, te)
    tab = pl.BlockSpec((None, H, NK, tm), lambda i, j: (i, 0, 0, 0))
    tab_a = pl.BlockSpec((None, H, A, tm), lambda i, j: (i, 0, j, 0))
    return pl.pallas_call(
        functools.partial(_peer_expert_kernel, A=A),
        grid=(M // tm, E // te),
        in_specs=[pl.BlockSpec((None, D, tm), lambda i, j: (i, 0, 0)),
                  pl.BlockSpec((te, D), lambda i, j: (j, 0)),
                  pl.BlockSpec((None, D, te), lambda i, j: (j, 0, 0)),
                  tab_a, tab_a, tab, tab,
                  pl.BlockSpec((tm, D), lambda i, j: (i, 0)),
                  _mod_spec2(T, tm, D)],
        out_specs=pl.BlockSpec((tm, D), lambda i, j: (i, 0)),
        out_shape=jax.ShapeDtypeStruct((M, D), F32),
        scratch_shapes=[pltpu.VMEM((D, tm), F32), pltpu.VMEM((te, tm), F32),
                        pltpu.VMEM((te, tm), BF16)],
        compiler_params=_params("arbitrary", "arbitrary"),
        name="peer_experts",
    )(h, u, vt, *tabs, x, _mod_rows(gate, T, tm))


def _mod_spec2(T, tm, D):
    if tm <= T:
        return pl.BlockSpec((None, 1, D), lambda i, j: (i * tm // T, 0, 0))
    return pl.BlockSpec((None, tm, D), lambda i, j: (0, i, 0))


ROW_TILE = 512


def _trunk(x, c, w, ab_wts, cd_wts, peer_wts, cache):
    B, T, D = x.shape
    M = B * T
    tm = min(ROW_TILE, M)
    depth = w['ada_w'].shape[0]
    past_len = 0 if cache is None else cache['page_table'].shape[1] * PAGE_SIZE
    pos = past_len + jnp.arange(T, dtype=jnp.int32)
    mod = _ada(c, w['ada_w'], w['ada_b'])
    xf = x.reshape(M, D)
    ckv_rows, kr_rows, s5r, s5i, k_rows, v_rows, wkvs, shifts = [], [], [], [], [], [], [], []
    for l in range(depth):
        i = l // 2
        sh1, sc1, g1, sh2, sc2, g2 = jnp.split(mod[l], 6, axis=-1)
        if l % 2 == 0:
            lc = None if cache is None else dict(
                ckv=cache['mla_ckv'], kr=cache['mla_krope'], page_table=cache['page_table'], layer=i,
                s5_re=cache['s5_re'][i], s5_im=cache['s5_im'][i])
            xf, ckv, kr, sr, si = _ab_layer(xf, (sh1, sc1, g1), w['norm1_g'][l], pos, B, T, tm, lc,
                                            ab_wts[i])
            ckv_rows.append(ckv.reshape(B, T, -1))
            kr_rows.append(kr.reshape(B, T, -1))
            state_shape = (B,) + w['s5_lambda_re'].shape[1:]
            s5r.append(sr.reshape(state_shape))
            s5i.append(si.reshape(state_shape))
        else:
            lc = None if cache is None else dict(
                sb_k=cache['sb_k'], sb_v=cache['sb_v'], page_table=cache['page_table'], layer=i,
                wkv=cache['wkv'][i], shift=cache['shift'][i])
            xf, k, v, wkv, shift = _cd_layer(xf, (sh1, sc1, g1), w['norm1_g'][l], B, T, tm, lc,
                                             cd_wts[i])
            k_rows.append(k.reshape(B, T, SB_HEADS, SB_HEAD_DIM))
            v_rows.append(v.reshape(B, T, SB_HEADS, SB_HEAD_DIM))
            wkvs.append(wkv)
            shifts.append(shift)
        pw = peer_wts[l]
        h2, *tabs = _peer_route(xf, w['norm2_g'][l], sc2, sh2, pw['wq'], pw['k1'], pw['k2'], T, tm)
        xf = _peer_experts(h2, pw['u'], pw['vt'], tabs, xf, g2, T, tm)
    y = _rmsnorm(xf, w['final_g'], tm).reshape(B, T, D)
    return (y, jnp.stack(ckv_rows), jnp.stack(kr_rows), jnp.stack(k_rows), jnp.stack(v_rows),
            jnp.stack(s5r), jnp.stack(s5i), jnp.stack(wkvs), jnp.stack(shifts))


def kernel(x_prompt, x_sample, c_prompt, c_sample, cache_mla_ckv, cache_mla_krope, cache_sb_k,
           cache_sb_v, page_table, state_s5_re, state_s5_im, state_rwkv_wkv, state_rwkv_shift,
           ada_w, ada_b, norm1_g, norm2_g, final_g, ab_w_in, ab_w_out, s5_lambda_re, s5_lambda_im,
           s5_log_dt, s5_b_re, s5_b_im, s5_c_re, s5_c_im, s5_d, s5_w_glu, s5_b_glu, mla_g_q,
           mla_w_uq, mla_g_kv, mla_w_uk, mla_w_uv, cd_w_in, cd_w_out, rwkv_mu, rwkv_w0, rwkv_w2,
           rwkv_a0, rwkv_a2, rwkv_g2, rwkv_k_k, rwkv_k_a, rwkv_r_k, rwkv_ln_w, rwkv_ln_b,
           peer_w_q, peer_k1, peer_k2, peer_u, peer_v):
    w = dict(ada_w=ada_w, ada_b=ada_b, norm1_g=norm1_g, norm2_g=norm2_g, final_g=final_g,
             s5_lambda_re=s5_lambda_re)
    ab = dict(ab_w_in=ab_w_in, ab_w_out=ab_w_out, s5_lambda_re=s5_lambda_re,
              s5_lambda_im=s5_lambda_im, s5_log_dt=s5_log_dt, s5_b_re=s5_b_re, s5_b_im=s5_b_im,
              s5_c_re=s5_c_re, s5_c_im=s5_c_im, s5_d=s5_d, s5_w_glu=s5_w_glu, s5_b_glu=s5_b_glu,
              mla_g_q=mla_g_q, mla_w_uq=mla_w_uq, mla_g_kv=mla_g_kv, mla_w_uk=mla_w_uk,
              mla_w_uv=mla_w_uv)
    cd = dict(cd_w_in=cd_w_in, cd_w_out=cd_w_out, rwkv_mu=rwkv_mu, rwkv_w0=rwkv_w0,
              rwkv_w2=rwkv_w2, rwkv_a0=rwkv_a0, rwkv_a2=rwkv_a2, rwkv_g2=rwkv_g2,
              rwkv_k_k=rwkv_k_k, rwkv_k_a=rwkv_k_a, rwkv_r_k=rwkv_r_k, rwkv_ln_w=rwkv_ln_w,
              rwkv_ln_b=rwkv_ln_b)
    depth = ada_w.shape[0]
    ab_wts = [_ab_weights({k: v[i] for k, v in ab.items()}) for i in range((depth + 1) // 2)]
    cd_wts = [_cd_weights({k: v[i] for k, v in cd.items()}) for i in range(depth // 2)]
    n_exp, d_model = peer_v.shape[1:]
    value_blocks = lambda v: jnp.transpose(
        v.reshape(n_exp // EXPERT_TILE, EXPERT_TILE, d_model), (0, 2, 1)).astype(BF16)
    peer_wts = [dict(wq=peer_w_q[l].astype(BF16), k1=peer_k1[l].astype(BF16),
                     k2=peer_k2[l].astype(BF16), u=peer_u[l].astype(BF16),
                     vt=value_blocks(peer_v[l])) for l in range(depth)]
    cache = dict(mla_ckv=cache_mla_ckv, mla_krope=cache_mla_krope, sb_k=cache_sb_k, sb_v=cache_sb_v,
                 page_table=page_table, s5_re=state_s5_re, s5_im=state_s5_im, wkv=state_rwkv_wkv,
                 shift=state_rwkv_shift)
    p = _trunk(x_prompt, c_prompt, w, ab_wts, cd_wts, peer_wts, None)
    s = _trunk(x_sample, c_sample, w, ab_wts, cd_wts, peer_wts, cache)
    return (p[0], s[0]) + p[1:] + s[1:]
```
